```python
import math
import jax, jax.numpy as jnp
from jax import lax
import numpy as np

D_MODEL = 1024
BATCH = 4
SEQ = 4096
DEPTH = 2
DEC_BATCH = 32
DEC_SEQ = 4
PAST_LEN = 8192
PAGE_SIZE = 128

CHUNK = 128
A_WIDTH = D_MODEL // 2
A_GROUPS = 4
A_GROUP_DIM = A_WIDTH // A_GROUPS
B_HEADS = 8
B_HEAD_DIM = 64
B_WIDTH = B_HEADS * B_HEAD_DIM
MOBA_BLOCK = 256
MOBA_TOPK = 3
Q_CHUNK = 16
N_BUCKETS = 32
MAX_DISTANCE = 128
D_FF = 2816
CONV_W = 3
EPS = 1e-6
N_IN = 2 * A_WIDTH + 3 * B_WIDTH + 2 * D_MODEL

kernel_name = "hybrid_chunkmlp_moba_convffn_step"


def rmsnorm(x, g):
    xf = x.astype(jnp.float32)
    y = xf * lax.rsqrt(jnp.mean(xf * xf, axis=-1, keepdims=True) + EPS)
    return (y * g.astype(jnp.float32)).astype(x.dtype)


def layernorm(x, g, b):
    xf = x.astype(jnp.float32)
    mu = jnp.mean(xf, axis=-1, keepdims=True)
    xc = xf - mu
    var = jnp.mean(xc * xc, axis=-1, keepdims=True)
    y = xc * lax.rsqrt(var + EPS) * g.astype(jnp.float32) + b.astype(jnp.float32)
    return y.astype(x.dtype)


def t5_bucket(rel):
    max_exact = N_BUCKETS // 2
    relf = jnp.maximum(rel, 1).astype(jnp.float32)
    large = max_exact + (jnp.log(relf / max_exact) / math.log(MAX_DISTANCE / max_exact)
                         * (N_BUCKETS - max_exact)).astype(jnp.int32)
    large = jnp.minimum(large, N_BUCKETS - 1)
    return jnp.where(rel < max_exact, rel, large)


def spatial_gate(u, v, w_s, b_s):
    bt, t, _ = v.shape
    lc = min(t, CHUNK)
    mask = jnp.tril(jnp.ones((lc, lc), dtype=bool))
    w = jnp.where(mask[None], w_s[:, :lc, :lc], 0)
    vc = v.reshape(bt, t // lc, lc, A_GROUPS, A_GROUP_DIM)
    mixed = jnp.einsum('gts,bnsgc->bntgc', w, vc) + b_s[:, :lc].T[None, None, :, :, None]
    return u * mixed.reshape(bt, t, A_WIDTH)


def moba_blocks(k_full, v_full):
    bt, l = k_full.shape[:2]
    nb = -(-l // MOBA_BLOCK)
    pad = nb * MOBA_BLOCK - l
    kp = jnp.pad(k_full, ((0, 0), (0, pad), (0, 0), (0, 0)))
    vp = jnp.pad(v_full, ((0, 0), (0, pad), (0, 0), (0, 0)))
    kb = kp.reshape(bt, nb, MOBA_BLOCK, B_HEADS, B_HEAD_DIM).transpose(0, 3, 1, 2, 4)
    vb = vp.reshape(bt, nb, MOBA_BLOCK, B_HEADS, B_HEAD_DIM).transpose(0, 3, 1, 2, 4)
    k_mean = jnp.mean(kb.astype(jnp.float32), axis=3)
    return kb, vb, k_mean


def moba_attend(q, q_pos, kb, vb, k_mean, rel_bias):
    bt, tq = q.shape[:2]
    nb = kb.shape[2]
    qf = q.transpose(0, 2, 1, 3).astype(jnp.float32)
    own = q_pos // MOBA_BLOCK
    past = jnp.arange(nb)[None, :] < own[:, None]
    gate = jnp.einsum('bhqd,bhnd->bhqn', qf, k_mean)
    gate = jnp.where(past[None, None], gate, -jnp.inf)
    k_top = min(MOBA_TOPK, nb)
    _, top_idx = lax.top_k(gate, k_top)
    sel_valid = top_idx < own[None, None, :, None]
    own_idx = jnp.broadcast_to(own[None, None, :, None], (bt, B_HEADS, tq, 1))
    idx = jnp.concatenate([top_idx, own_idx], axis=-1)
    slot_valid = jnp.concatenate([sel_valid, jnp.ones_like(own_idx, dtype=bool)], axis=-1)
    bi = jnp.arange(bt)[:, None, None, None]
    hi = jnp.arange(B_HEADS)[None, :, None, None]
    k_sel = kb[bi, hi, idx]
    v_sel = vb[bi, hi, idx]
    key_pos = idx[..., None] * MOBA_BLOCK + jnp.arange(MOBA_BLOCK)
    rel = q_pos[None, None, :, None, None] - key_pos
    allowed = slot_valid[..., None] & (rel >= 0)
    bias = rel_bias.T.astype(jnp.float32)[hi[..., None], t5_bucket(jnp.maximum(rel, 0))]
    scale = 1.0 / math.sqrt(B_HEAD_DIM)
    logits = jnp.einsum('bhqd,bhqjsd->bhqjs', qf, k_sel.astype(jnp.float32)) * scale + bias
    logits = jnp.where(allowed, logits, -jnp.inf).reshape(bt, B_HEADS, tq, -1)
    p = jax.nn.softmax(logits, axis=-1).reshape(bt, B_HEADS, tq, k_top + 1, MOBA_BLOCK)
    out = jnp.einsum('bhqjs,bhqjsd->bqhd', p, v_sel.astype(jnp.float32))
    return out.astype(q.dtype)


def mixer_block(x, w_in, ln_v_g, ln_v_b, w_s, b_s, w_pa, w_pb, w_o, norm1_g, rel_bias, k_past, v_past):
    bt, t, _ = x.shape
    h = rmsnorm(x, norm1_g) @ w_in
    splits = [A_WIDTH, 2 * A_WIDTH, 2 * A_WIDTH + B_WIDTH, 2 * A_WIDTH + 2 * B_WIDTH,
              2 * A_WIDTH + 3 * B_WIDTH, 2 * A_WIDTH + 3 * B_WIDTH + D_MODEL]
    u, va, q, k, vb, ga, gb = jnp.split(h, splits, axis=-1)
    u = jax.nn.gelu(u)
    va = layernorm(jax.nn.gelu(va), ln_v_g, ln_v_b)
    ya = spatial_gate(u, va, w_s, b_s)
    q = q.reshape(bt, t, B_HEADS, B_HEAD_DIM)
    k = k.reshape(bt, t, B_HEADS, B_HEAD_DIM)
    vb = vb.reshape(bt, t, B_HEADS, B_HEAD_DIM)
    if k_past is None:
        k_full, v_full, pos0 = k, vb, 0
    else:
        k_full = jnp.concatenate([k_past, k], axis=1)
        v_full = jnp.concatenate([v_past, vb], axis=1)
        pos0 = k_past.shape[1]
    kb, vbl, k_mean = moba_blocks(k_full, v_full)
    q_pos = pos0 + jnp.arange(t, dtype=jnp.int32)
    if t > Q_CHUNK and t % Q_CHUNK == 0:
        nq = t // Q_CHUNK
        qs = q.reshape(bt, nq, Q_CHUNK, B_HEADS, B_HEAD_DIM).transpose(1, 0, 2, 3, 4)
        ps = q_pos.reshape(nq, Q_CHUNK)
        outs = lax.map(lambda a: moba_attend(a[0], a[1], kb, vbl, k_mean, rel_bias), (qs, ps))
        yb = outs.transpose(1, 0, 2, 3, 4).reshape(bt, t, B_WIDTH)
    else:
        yb = moba_attend(q, q_pos, kb, vbl, k_mean, rel_bias).reshape(bt, t, B_WIDTH)
    m = jax.nn.sigmoid(ga) * (ya @ w_pa) + jax.nn.sigmoid(gb) * (yb @ w_pb)
    return x + m @ w_o, k, vb, va


def conv_ffn(x, w_up, conv_w, conv_b, w_down, conv_state):
    t = x.shape[1]
    a, g = jnp.split(x @ w_up, 2, axis=-1)
    a_ext = jnp.concatenate([conv_state.astype(a.dtype), a], axis=1)
    c = conv_b
    for i in range(CONV_W):
        c = c + a_ext[:, i:i + t] * conv_w[i]
    y = (jax.nn.gelu(c) * g) @ w_down
    return y, a_ext[:, -(CONV_W - 1):]


def setup_inputs(seed: int = 0) -> dict:
    key = jax.random.key(seed)
    ks = jax.random.split(key, 24)
    n_pages = PAST_LEN // PAGE_SIZE
    n_pool = (5 * DEC_BATCH * n_pages) // 4
    nrm = jax.random.normal
    page_table = jax.random.permutation(ks[0], n_pool)[:DEC_BATCH * n_pages].reshape(DEC_BATCH, n_pages).astype(jnp.int32)
    return {
        "x_prompt": nrm(ks[1], (BATCH, SEQ, D_MODEL), jnp.float32),
        "x_sample": nrm(ks[2], (DEC_BATCH, DEC_SEQ, D_MODEL), jnp.float32),
        "cache_k": nrm(ks[3], (DEPTH, n_pool, PAGE_SIZE, B_HEADS, B_HEAD_DIM), jnp.float32),
        "cache_v": nrm(ks[4], (DEPTH, n_pool, PAGE_SIZE, B_HEADS, B_HEAD_DIM), jnp.float32),
        "state_conv": nrm(ks[5], (DEPTH, DEC_BATCH, CONV_W - 1, D_FF), jnp.float32),
        "page_table": page_table,
        "w_in": nrm(ks[6], (DEPTH, D_MODEL, N_IN), jnp.float32) * D_MODEL ** -0.5,
        "ln_v_g": 1.0 + 0.01 * nrm(ks[7], (DEPTH, A_WIDTH), jnp.float32),
        "ln_v_b": 0.01 * nrm(ks[8], (DEPTH, A_WIDTH), jnp.float32),
        "w_s": 0.05 * nrm(ks[9], (DEPTH, A_GROUPS, CHUNK, CHUNK), jnp.float32),
        "b_s": 1.0 + 0.01 * nrm(ks[10], (DEPTH, A_GROUPS, CHUNK), jnp.float32),
        "w_pa": nrm(ks[11], (DEPTH, A_WIDTH, D_MODEL), jnp.float32) * A_WIDTH ** -0.5,
        "w_pb": nrm(ks[12], (DEPTH, B_WIDTH, D_MODEL), jnp.float32) * B_WIDTH ** -0.5,
        "w_o": nrm(ks[13], (DEPTH, D_MODEL, D_MODEL), jnp.float32) * D_MODEL ** -0.5,
        "norm1_g": 1.0 + 0.01 * nrm(ks[14], (DEPTH, D_MODEL), jnp.float32),
        "norm2_g": 1.0 + 0.01 * nrm(ks[15], (DEPTH, D_MODEL), jnp.float32),
        "w_up": nrm(ks[16], (DEPTH, D_MODEL, 2 * D_FF), jnp.float32) * D_MODEL ** -0.5,
        "conv_w": 0.5 * nrm(ks[17], (DEPTH, CONV_W, D_FF), jnp.float32),
        "conv_b": 0.01 * nrm(ks[18], (DEPTH, D_FF), jnp.float32),
        "w_down": nrm(ks[19], (DEPTH, D_FF, D_MODEL), jnp.float32) * D_FF ** -0.5,
        "rel_bias": 0.1 * nrm(ks[20], (N_BUCKETS, B_HEADS), jnp.float32),
        "norm_f": 1.0 + 0.01 * nrm(ks[21], (D_MODEL,), jnp.float32),
    }


def reference(x_prompt, x_sample, cache_k, cache_v, state_conv, page_table, w_in, ln_v_g, ln_v_b, w_s, b_s,
              w_pa, w_pb, w_o, norm1_g, norm2_g, w_up, conv_w, conv_b, w_down, rel_bias, norm_f):
    xp, xs = x_prompt, x_sample
    n_db, n_pages = page_table.shape
    past_len = n_pages * cache_k.shape[2]
    kp_l, vp_l, ks_l, vs_l, as_l, cp_l, cs_l = [], [], [], [], [], [], []
    for l in range(DEPTH):
        xp, kp, vp, _ = mixer_block(xp, w_in[l], ln_v_g[l], ln_v_b[l], w_s[l], b_s[l], w_pa[l], w_pb[l],
                                    w_o[l], norm1_g[l], rel_bias, None, None)
        conv0 = jnp.zeros((xp.shape[0], CONV_W - 1, D_FF), xp.dtype)
        yf, cp = conv_ffn(rmsnorm(xp, norm2_g[l]), w_up[l], conv_w[l], conv_b[l], w_down[l], conv0)
        xp = xp + yf
        k_past = cache_k[l][page_table].reshape(n_db, past_len, B_HEADS, B_HEAD_DIM)
        v_past = cache_v[l][page_table].reshape(n_db, past_len, B_HEADS, B_HEAD_DIM)
        xs, ksn, vsn, vas = mixer_block(xs, w_in[l], ln_v_g[l], ln_v_b[l], w_s[l], b_s[l], w_pa[l], w_pb[l],
                                        w_o[l], norm1_g[l], rel_bias, k_past, v_past)
        yf, cs = conv_ffn(rmsnorm(xs, norm2_g[l]), w_up[l], conv_w[l], conv_b[l], w_down[l], state_conv[l])
        xs = xs + yf
        kp_l.append(kp); vp_l.append(vp); ks_l.append(ksn); vs_l.append(vsn)
        as_l.append(vas); cp_l.append(cp); cs_l.append(cs)
    y_prompt = rmsnorm(xp, norm_f)
    y_sample = rmsnorm(xs, norm_f)
    k_prompt = jnp.stack(kp_l, 0)
    v_prompt = jnp.stack(vp_l, 0)
    k_sample = jnp.stack(ks_l, 0)
    v_sample = jnp.stack(vs_l, 0)
    chunk_v_sample = jnp.stack(as_l, 0)
    conv_prompt = jnp.stack(cp_l, 0)
    conv_sample = jnp.stack(cs_l, 0)
    return (y_prompt, y_sample, k_prompt, v_prompt, k_sample, v_sample, chunk_v_sample, conv_prompt, conv_sample)
```

```python
import functools
import math

import jax
import jax.numpy as jnp
import numpy as np
from jax import lax
from jax.experimental import pallas as pl
from jax.experimental.pallas import tpu as pltpu

D_MODEL = 1024
CHUNK = 128
A_WIDTH = D_MODEL // 2
A_GROUPS = 4
A_GROUP_DIM = A_WIDTH // A_GROUPS
B_HEADS = 8
B_HEAD_DIM = 64
B_WIDTH = B_HEADS * B_HEAD_DIM
MOBA_BLOCK = 256
MOBA_TOPK = 3
N_BUCKETS = 32
MAX_DISTANCE = 128
D_FF = 2816
CONV_W = 3
EPS = 1e-6
N_IN = 2 * A_WIDTH + 3 * B_WIDTH + 2 * D_MODEL

_C_U, _C_VA, _C_Q, _C_K, _C_V, _C_GA, _C_GB = 0, 512, 1024, 1536, 2048, 2560, 3584

_NEG = -1e30
_HEADS_PER_STEP = 2
_PAGES_PER_STEP = 8
_VMEM_LIMIT = 48 * 1024 * 1024

_F32 = jnp.float32
_BF16 = jnp.bfloat16


def _gelu(x):
    return 0.5 * x * (1.0 + jnp.tanh(0.7978845608028654 * (x + 0.044715 * (x * x * x))))


def _sigmoid(x):
    return 1.0 / (1.0 + jnp.exp(-x))


def _rms(x, g):
    return x * lax.rsqrt(jnp.mean(x * x, axis=-1, keepdims=True) + EPS) * g


def _const_spec(shape):
    zeros = (0,) * len(shape)
    return pl.BlockSpec(shape, lambda *_: zeros, pipeline_mode=pl.Buffered(1))


def _proj_kernel(x_ref, g1_ref, w_ref, lng_ref, lnb_ref, wmix_ref, bmix_ref,
                 q_ref, k_ref, v_ref, ya_ref, sga_ref, sgb_ref, *va_out, tm):
    xn = _rms(x_ref[...], g1_ref[...]).astype(_BF16)

    def proj(lo, width):
        return jnp.dot(xn, w_ref[:, lo:lo + width], preferred_element_type=_F32)

    q_ref[...] = proj(_C_Q, B_WIDTH)
    k_ref[...] = proj(_C_K, B_WIDTH)
    v_ref[...] = proj(_C_V, B_WIDTH)
    sga_ref[...] = _sigmoid(proj(_C_GA, D_MODEL)).astype(_BF16)
    sgb_ref[...] = _sigmoid(proj(_C_GB, D_MODEL)).astype(_BF16)

    u = _gelu(proj(_C_U, A_WIDTH))
    va = _gelu(proj(_C_VA, A_WIDTH))
    mu = jnp.mean(va, axis=-1, keepdims=True)
    vc = va - mu
    var = jnp.mean(vc * vc, axis=-1, keepdims=True)
    va = vc * lax.rsqrt(var + EPS) * lng_ref[...] + lnb_ref[...]
    if va_out:
        va_out[0][...] = va
    for c in range(tm // CHUNK):
        rows = slice(c * CHUNK, (c + 1) * CHUNK)
        for g in range(A_GROUPS):
            cols = slice(g * A_GROUP_DIM, (g + 1) * A_GROUP_DIM)
            mixed = jnp.dot(wmix_ref[g], va[rows, cols].astype(_BF16),
                            preferred_element_type=_F32) + bmix_ref[g]
            ya_ref[rows, cols] = (u[rows, cols] * mixed).astype(_BF16)


def _proj_call(x2, g1, w_in, lng, lnb, wmix, bmix, *, tm, emit_va):
    n = x2.shape[0]
    row = lambda width: pl.BlockSpec((tm, width), lambda i: (i, 0))
    out_shape = [jax.ShapeDtypeStruct((n, B_WIDTH), _F32)] * 3 + [
        jax.ShapeDtypeStruct((n, A_WIDTH), _BF16),
        jax.ShapeDtypeStruct((n, D_MODEL), _BF16),
        jax.ShapeDtypeStruct((n, D_MODEL), _BF16)]
    out_specs = [row(B_WIDTH)] * 3 + [row(A_WIDTH), row(D_MODEL), row(D_MODEL)]
    if emit_va:
        out_shape.append(jax.ShapeDtypeStruct((n, A_WIDTH), _F32))
        out_specs.append(row(A_WIDTH))
    return pl.pallas_call(
        functools.partial(_proj_kernel, tm=tm),
        grid=(n // tm,),
        in_specs=[row(D_MODEL), _const_spec((1, D_MODEL)), _const_spec((D_MODEL, N_IN)),
                  _const_spec((1, A_WIDTH)), _const_spec((1, A_WIDTH)),
                  _const_spec((A_GROUPS, CHUNK, CHUNK)), _const_spec((A_GROUPS, CHUNK, CHUNK))],
        out_specs=out_specs,
        out_shape=out_shape,
        compiler_params=pltpu.CompilerParams(dimension_semantics=("arbitrary",),
                                             vmem_limit_bytes=_VMEM_LIMIT),
        name="proj_mixer_a",
    )(x2, g1, w_in, lng, lnb, wmix, bmix)


def _topk_block_mask(gate, own, nb, width):
    blk = lax.broadcasted_iota(jnp.int32, (nb, width), 0)
    gm = jnp.where(blk < own, gate, -jnp.inf)
    rank = jnp.zeros((nb, width), _F32)
    for jp in range(nb):
        r = gm[jp:jp + 1, :]
        beats = jnp.where(r > gm, 1.0, jnp.where(r == gm, jnp.where(blk > jp, 1.0, 0.0), 0.0))
        rank = rank + beats
    past_sel = jnp.where(blk < own, jnp.where(rank < MOBA_TOPK, 1.0, 0.0), 0.0)
    return jnp.where(blk == own, 1.0, past_sel)


def _moba_prompt_kernel(q_ref, k_ref, v_ref, btab_ref, o_ref,
                        k_scr, vt_scr, kmean_scr, sel_scr, *, nb):
    i = pl.program_id(2)
    blk = MOBA_BLOCK

    @pl.when(i == 0)
    def _():
        for jb in range(nb):
            kb = k_ref[0, jb * blk:(jb + 1) * blk, :]
            k_scr[jb] = kb.astype(_BF16)
            kmean_scr[jb:jb + 1, :] = jnp.mean(kb, axis=0, keepdims=True)
            vt_scr[jb] = v_ref[0, jb * blk:(jb + 1) * blk, :].T.astype(_BF16)

    qt = q_ref[0].T
    feat = lax.broadcasted_iota(jnp.int32, qt.shape, 0)
    outs = []
    for hh in range(_HEADS_PER_STEP):
        in_head = (feat >= hh * B_HEAD_DIM) & (feat < (hh + 1) * B_HEAD_DIM)
        qh = jnp.where(in_head, qt, 0.0)
        gate = jnp.dot(kmean_scr[...], qh, preferred_element_type=_F32,
                       precision=lax.Precision.HIGHEST)
        sel_scr[hh] = _topk_block_mask(gate, i, nb, blk)
        qz = (qh * (1.0 / math.sqrt(B_HEAD_DIM))).astype(_BF16)

        def body(step, carry, hh=hh, qz=qz):
            m, l, acc = carry
            j = i - step
            s = jnp.dot(k_scr[j], qz, preferred_element_type=_F32)
            s = s + btab_ref[hh, jnp.minimum(step, 2)]
            s = jnp.where(sel_scr[hh, pl.ds(j, 1), :] > 0.5, s, _NEG)
            m_new = jnp.maximum(m, jnp.max(s, axis=0, keepdims=True))
            alpha = jnp.exp(m - m_new)
            p = jnp.exp(s - m_new)
            l = alpha * l + jnp.sum(p, axis=0, keepdims=True)
            pv = jnp.dot(vt_scr[j, hh * B_HEAD_DIM:(hh + 1) * B_HEAD_DIM, :], p.astype(_BF16),
                         preferred_element_type=_F32)
            return m_new, l, alpha * acc + pv

        init = (jnp.full((1, blk), _NEG, _F32), jnp.zeros((1, blk), _F32),
                jnp.zeros((B_HEAD_DIM, blk), _F32))
        _, l, acc = lax.fori_loop(0, i + 1, body, init)
        outs.append(acc / l)
    o_ref[0] = jnp.concatenate(outs, axis=0).T.astype(_BF16)


def _moba_prompt_call(q3, k3, v3, btab):
    b, t, _ = q3.shape
    nb = t // MOBA_BLOCK
    lanes = _HEADS_PER_STEP * B_HEAD_DIM
    return pl.pallas_call(
        functools.partial(_moba_prompt_kernel, nb=nb),
        grid=(b, B_HEADS // _HEADS_PER_STEP, nb),
        in_specs=[pl.BlockSpec((1, MOBA_BLOCK, lanes), lambda bb, hp, i: (bb, i, hp)),
                  pl.BlockSpec((1, t, lanes), lambda bb, hp, i: (bb, 0, hp)),
                  pl.BlockSpec((1, t, lanes), lambda bb, hp, i: (bb, 0, hp)),
                  pl.BlockSpec((_HEADS_PER_STEP, 3, MOBA_BLOCK, MOBA_BLOCK),
                               lambda bb, hp, i: (hp, 0, 0, 0))],
        out_specs=pl.BlockSpec((1, MOBA_BLOCK, lanes), lambda bb, hp, i: (bb, i, hp)),
        out_shape=jax.ShapeDtypeStruct((b, t, B_WIDTH), _BF16),
        scratch_shapes=[pltpu.VMEM((nb, MOBA_BLOCK, lanes), _BF16),
                        pltpu.VMEM((nb, lanes, MOBA_BLOCK), _BF16),
                        pltpu.VMEM((nb, lanes), _F32),
                        pltpu.VMEM((_HEADS_PER_STEP, nb, MOBA_BLOCK), _F32)],
        compiler_params=pltpu.CompilerParams(
            dimension_semantics=("arbitrary", "arbitrary", "arbitrary"),
            vmem_limit_bytes=_VMEM_LIMIT),
        name="moba_prompt",
    )(q3, k3, v3, btab)


def _moba_sample_kernel(pt_ref, qbd_ref, knew_ref, vnew_ref, bnear_ref, bfar_ref, bown_ref,
                        hmask_ref, rsum_ref, *rest, nb_past):
    del pt_ref
    pages = _PAGES_PER_STEP
    k_pages = rest[:pages]
    v_pages = rest[pages:2 * pages]
    o_ref = rest[2 * pages]
    kmean_scr, m_scr, l_scr, acc_scr = rest[2 * pages + 1:]
    c = pl.program_id(1)
    rows = qbd_ref.shape[1]
    blocks_per_step = pages * CHUNK // MOBA_BLOCK
    pages_per_block = MOBA_BLOCK // CHUNK

    qbd = qbd_ref[0]
    qz = (qbd * (1.0 / math.sqrt(B_HEAD_DIM))).astype(_BF16)

    @pl.when(c == 0)
    def _():
        kmean_scr[...] = jnp.zeros_like(kmean_scr)

    for bl in range(blocks_per_step):
        jg = c * blocks_per_step + bl
        kb = jnp.concatenate([k_pages[bl * pages_per_block + p][0] for p in range(pages_per_block)],
                             axis=0)
        vb = jnp.concatenate([v_pages[bl * pages_per_block + p][0] for p in range(pages_per_block)],
                             axis=0)
        kmean_scr[pl.ds(jg, 1), :] = jnp.mean(kb, axis=0, keepdims=True)
        s = lax.dot_general(qz, kb.astype(_BF16), (((1,), (1,)), ((), ())),
                            preferred_element_type=_F32)
        s = s + jnp.where(jg == nb_past - 1, bnear_ref[...], bfar_ref[:, 0:1])
        m = jnp.max(s, axis=-1, keepdims=True)
        p = jnp.exp(s - m)
        m_scr[jg] = jnp.broadcast_to(m, (rows, 128))
        l_scr[jg] = jnp.broadcast_to(jnp.sum(p, axis=-1, keepdims=True), (rows, 128))
        acc_scr[jg] = jnp.dot(p.astype(_BF16), vb.astype(_BF16), preferred_element_type=_F32)

    @pl.when(c == pl.num_programs(1) - 1)
    def _():
        nbp = kmean_scr.shape[0]
        gate = lax.dot_general(qbd, kmean_scr[...], (((1,), (1,)), ((), ())),
                               preferred_element_type=_F32, precision=lax.Precision.HIGHEST)
        lane = lax.broadcasted_iota(jnp.int32, (rows, nbp), 1)
        gm = jnp.where(lane < nb_past, gate, -jnp.inf)
        rank = jnp.zeros((rows, nbp), _F32)
        for jp in range(nb_past):
            r = gm[:, jp:jp + 1]
            rank = rank + jnp.where(r > gm, 1.0,
                                    jnp.where(r == gm, jnp.where(lane > jp, 1.0, 0.0), 0.0))
        sel = jnp.where(lane < nb_past, jnp.where(rank < MOBA_TOPK, 1.0, 0.0), 0.0)

        s_own = lax.dot_general(qbd * (1.0 / math.sqrt(B_HEAD_DIM)), knew_ref[0],
                                (((1,), (1,)), ((), ())),
                                preferred_element_type=_F32) + bown_ref[...]
        m_tot = jnp.max(s_own, axis=-1, keepdims=True)
        sel_cols = [sel[:, j:j + 1] > 0.5 for j in range(nb_past)]
        for j in range(nb_past):
            m_tot = jnp.maximum(m_tot, jnp.where(sel_cols[j], m_scr[j][:, 0:1], _NEG))
        p_own = jnp.exp(s_own - m_tot)
        l_tot = jnp.sum(p_own, axis=-1, keepdims=True)
        acc = jnp.dot(p_own, vnew_ref[0], preferred_element_type=_F32)
        for j in range(nb_past):
            w = jnp.where(sel_cols[j], jnp.exp(m_scr[j][:, 0:1] - m_tot), 0.0)
            l_tot = l_tot + w * l_scr[j][:, 0:1]
            acc = acc + w * acc_scr[j]
        out = acc / l_tot * hmask_ref[...]
        o_ref[0] = jnp.dot(rsum_ref[...], out, preferred_element_type=_F32,
                           precision=lax.Precision.HIGHEST)


def _moba_sample_call(page_flat, qbd, knew, vnew, bnear, bfar, bown, hmask, rsum,
                      cache_k2, cache_v2, *, layer, n_pool, n_pages):
    nbatch, rows, _ = qbd.shape
    tpad = knew.shape[1]
    tdec = rsum.shape[0]
    pages = _PAGES_PER_STEP
    nb_past = n_pages * CHUNK // MOBA_BLOCK
    nbp = -(-nb_past // 128) * 128
    base = layer * n_pool

    def page_spec(p):
        return pl.BlockSpec((1, CHUNK, B_WIDTH),
                            lambda bb, cc, pt, p=p: (base + pt[bb * n_pages + cc * pages + p], 0, 0))

    per_batch = lambda shape: pl.BlockSpec((1,) + shape, lambda bb, cc, pt: (bb, 0, 0))
    const = lambda shape: pl.BlockSpec(shape, lambda bb, cc, pt: (0,) * len(shape))
    grid_spec = pltpu.PrefetchScalarGridSpec(
        num_scalar_prefetch=1,
        grid=(nbatch, n_pages // pages),
        in_specs=[per_batch((rows, B_WIDTH)), per_batch((tpad, B_WIDTH)), per_batch((tpad, B_WIDTH)),
                  const((rows, MOBA_BLOCK)), const((rows, 128)), const((rows, tpad)),
                  const((rows, B_WIDTH)), const((tdec, rows))]
                 + [page_spec(p) for p in range(pages)] * 2,
        out_specs=per_batch((tdec, B_WIDTH)),
        scratch_shapes=[pltpu.VMEM((nbp, B_WIDTH), _F32),
                        pltpu.VMEM((nb_past, rows, 128), _F32),
                        pltpu.VMEM((nb_past, rows, 128), _F32),
                        pltpu.VMEM((nb_past, rows, B_WIDTH), _F32)])
    return pl.pallas_call(
        functools.partial(_moba_sample_kernel, nb_past=nb_past),
        grid_spec=grid_spec,
        out_shape=jax.ShapeDtypeStruct((nbatch, tdec, B_WIDTH), _F32),
        compiler_params=pltpu.CompilerParams(dimension_semantics=("arbitrary", "arbitrary"),
                                             vmem_limit_bytes=_VMEM_LIMIT),
        name="moba_sample",
    )(page_flat, qbd, knew, vnew, bnear, bfar, bown, hmask, rsum,
      *([cache_k2] * pages), *([cache_v2] * pages))


def _post_kernel(x_ref, ya_ref, yb_ref, sga_ref, sgb_ref, wpa_ref, wpb_ref, wo_ref, g2_ref,
                 wup_ref, cw_ref, cb_ref, wdn_ref, gf_ref, *rest, tm, tiles_per_seq, per_row_state,
                 final_norm):
    if per_row_state:
        s1_ref, s2_ref, tpos_ref = rest[:3]
        rest = rest[3:]
    else:
        st_ref = rest[0]
        rest = rest[1:]
    xo_ref, a_ref = rest[:2]
    rest = rest[2:]
    if final_norm:
        y_ref = rest[0]
        rest = rest[1:]
    if not per_row_state:
        carry_scr = rest[0]

    ma = jnp.dot(ya_ref[...], wpa_ref[...], preferred_element_type=_F32)
    mb = jnp.dot(yb_ref[...], wpb_ref[...], preferred_element_type=_F32)
    merged = sga_ref[...].astype(_F32) * ma + sgb_ref[...].astype(_F32) * mb
    x1 = x_ref[...] + jnp.dot(merged.astype(_BF16), wo_ref[...], preferred_element_type=_F32)

    xn = _rms(x1, g2_ref[...]).astype(_BF16)
    a = jnp.dot(xn, wup_ref[:, :D_FF], preferred_element_type=_F32)
    gate = jnp.dot(xn, wup_ref[:, D_FF:], preferred_element_type=_F32)

    prev1 = pltpu.roll(a, 1, 0)
    prev2 = pltpu.roll(a, 2, 0)
    if per_row_state:
        tpos = tpos_ref[...]
        prev1 = jnp.where(tpos >= 1, prev1, s1_ref[...])
        prev2 = jnp.where(tpos >= 2, prev2, s2_ref[...])
        a_ref[...] = a
    else:
        i = pl.program_id(0)

        @pl.when(i % tiles_per_seq == 0)
        def _():
            carry_scr[...] = st_ref[0]

        row = lax.broadcasted_iota(jnp.int32, a.shape, 0)
        c0 = carry_scr[0:1, :]
        c1 = carry_scr[1:2, :]
        prev1 = jnp.where(row == 0, c1, prev1)
        prev2 = jnp.where(row == 0, c0, jnp.where(row == 1, c1, prev2))
        tail = a[tm - (CONV_W - 1):, :]
        carry_scr[...] = tail
        a_ref[0] = tail
    conv = cb_ref[...] + prev2 * cw_ref[0:1, :] + prev1 * cw_ref[1:2, :] + a * cw_ref[2:3, :]
    act = (_gelu(conv) * gate).astype(_BF16)
    x2 = x1 + jnp.dot(act, wdn_ref[...], preferred_element_type=_F32)
    xo_ref[...] = x2
    if final_norm:
        y_ref[...] = _rms(x2, gf_ref[...])


def _post_call(x2, ya, yb, sga, sgb, wpa, wpb, wo, g2, wup, cw, cb, wdn, gf, state_args, *,
               tm, tiles_per_seq, per_row_state, final_norm):
    n = x2.shape[0]
    row = lambda width: pl.BlockSpec((tm, width), lambda i: (i, 0))
    in_specs = [row(D_MODEL), row(A_WIDTH), row(B_WIDTH), row(D_MODEL), row(D_MODEL),
                _const_spec((A_WIDTH, D_MODEL)), _const_spec((B_WIDTH, D_MODEL)),
                _const_spec((D_MODEL, D_MODEL)), _const_spec((1, D_MODEL)),
                _const_spec((D_MODEL, 2 * D_FF)), _const_spec((CONV_W, D_FF)),
                _const_spec((1, D_FF)), _const_spec((D_FF, D_MODEL)), _const_spec((1, D_MODEL))]
    out_shape = [jax.ShapeDtypeStruct((n, D_MODEL), _F32)]
    out_specs = [row(D_MODEL)]
    scratch = []
    if per_row_state:
        in_specs += [row(D_FF), row(D_FF), row(D_FF)]
        out_shape.append(jax.ShapeDtypeStruct((n, D_FF), _F32))
        out_specs.append(row(D_FF))
    else:
        nseq = n // (tm * tiles_per_seq)
        in_specs.append(pl.BlockSpec((1, CONV_W - 1, D_FF), lambda i: (i // tiles_per_seq, 0, 0)))
        out_shape.append(jax.ShapeDtypeStruct((nseq, CONV_W - 1, D_FF), _F32))
        out_specs.append(pl.BlockSpec((1, CONV_W - 1, D_FF), lambda i: (i // tiles_per_seq, 0, 0)))
        scratch.append(pltpu.VMEM((CONV_W - 1, D_FF), _F32))
    if final_norm:
        out_shape.append(jax.ShapeDtypeStruct((n, D_MODEL), _F32))
        out_specs.append(row(D_MODEL))
    return pl.pallas_call(
        functools.partial(_post_kernel, tm=tm, tiles_per_seq=tiles_per_seq,
                          per_row_state=per_row_state, final_norm=final_norm),
        grid=(n // tm,),
        in_specs=in_specs,
        out_specs=out_specs,
        out_shape=out_shape,
        scratch_shapes=scratch,
        compiler_params=pltpu.CompilerParams(dimension_semantics=("arbitrary",),
                                             vmem_limit_bytes=_VMEM_LIMIT),
        name="merge_convffn",
    )(x2, ya, yb, sga, sgb, wpa, wpb, wo, g2, wup, cw, cb, wdn, gf, *state_args)


def _t5_bucket(rel):
    max_exact = N_BUCKETS // 2
    relf = jnp.maximum(rel, 1).astype(_F32)
    large = max_exact + (jnp.log(relf / max_exact) / math.log(MAX_DISTANCE / max_exact)
                         * (N_BUCKETS - max_exact)).astype(jnp.int32)
    large = jnp.minimum(large, N_BUCKETS - 1)
    return jnp.where(rel < max_exact, rel, large)


def _far_bucket_is_last(min_rel):
    max_exact = N_BUCKETS // 2
    v = max_exact + int(math.log(min_rel / max_exact) / math.log(MAX_DISTANCE / max_exact)
                        * (N_BUCKETS - max_exact) - 1e-6)
    return v >= N_BUCKETS - 1


def _prompt_bias_tables(rel_bias):
    assert _far_bucket_is_last(MOBA_BLOCK + 1)
    bt = rel_bias.T.astype(_F32)
    key = jnp.arange(MOBA_BLOCK, dtype=jnp.int32)[:, None]
    qry = jnp.arange(MOBA_BLOCK, dtype=jnp.int32)[None, :]
    rel_own = qry - key
    own = jnp.where(rel_own >= 0, bt[:, _t5_bucket(jnp.maximum(rel_own, 0))], _NEG)
    prev = bt[:, _t5_bucket(rel_own + MOBA_BLOCK)]
    far = jnp.broadcast_to(bt[:, N_BUCKETS - 1][:, None, None], own.shape)
    return jnp.stack([own, prev, far], axis=1)


def _sample_bias_tables(rel_bias, tdec, tpad):
    assert _far_bucket_is_last(MOBA_BLOCK + 1)
    bt = rel_bias.T.astype(_F32)
    tok = jnp.arange(tdec, dtype=jnp.int32)[:, None]
    near = bt[:, _t5_bucket(MOBA_BLOCK + tok - jnp.arange(MOBA_BLOCK, dtype=jnp.int32)[None, :])]
    far = jnp.broadcast_to(bt[:, N_BUCKETS - 1][:, None, None], (B_HEADS, tdec, 128))
    new = jnp.arange(tpad, dtype=jnp.int32)[None, :]
    rel_own = tok - new
    own = jnp.where((rel_own >= 0) & (new < tdec), bt[:, _t5_bucket(jnp.maximum(rel_own, 0))], _NEG)
    flat = lambda a: a.reshape(B_HEADS * tdec, a.shape[-1])
    return flat(near), flat(far), flat(own)


def kernel(x_prompt, x_sample, cache_k, cache_v, state_conv, page_table, w_in, ln_v_g, ln_v_b, w_s,
           b_s, w_pa, w_pb, w_o, norm1_g, norm2_g, w_up, conv_w, conv_b, w_down, rel_bias, norm_f):
    nbp_, seq, _ = x_prompt.shape
    nbs, tdec, _ = x_sample.shape
    depth, n_pool, page_size = cache_k.shape[:3]
    n_pages = page_table.shape[1]
    ns = nbs * tdec
    assert page_size == CHUNK and (n_pages * page_size) % MOBA_BLOCK == 0
    assert n_pages % _PAGES_PER_STEP == 0 and seq % MOBA_BLOCK == 0
    assert tdec <= CHUNK and ns % CHUNK == 0 and CHUNK % tdec == 0
    tpad = -(-tdec // 8) * 8
    tm_p = 256

    xp = x_prompt.reshape(nbp_ * seq, D_MODEL)
    xs = x_sample.reshape(ns, D_MODEL)
    cache_k2 = cache_k.reshape(depth * n_pool, page_size, B_WIDTH)
    cache_v2 = cache_v.reshape(depth * n_pool, page_size, B_WIDTH)
    page_flat = page_table.reshape(-1).astype(jnp.int32)

    btab = _prompt_bias_tables(rel_bias)
    bnear_s, bfar_s, bown_s = _sample_bias_tables(rel_bias, tdec, tpad)
    rows = B_HEADS * tdec
    row_head = jnp.arange(rows, dtype=jnp.int32)[:, None] // tdec
    hmask = (jnp.arange(B_WIDTH, dtype=jnp.int32)[None, :] // B_HEAD_DIM == row_head).astype(_F32)
    rsum = (jnp.arange(rows, dtype=jnp.int32)[None, :] % tdec
            == jnp.arange(tdec, dtype=jnp.int32)[:, None]).astype(_F32)
    tpos = jnp.broadcast_to((jnp.arange(ns, dtype=jnp.int32) % tdec)[:, None], (ns, D_FF))
    tril = jnp.tril(jnp.ones((CHUNK, CHUNK), bool))
    eye_s = jnp.eye(CHUNK // tdec, dtype=_F32)
    zero_state = jnp.zeros((nbp_, CONV_W - 1, D_FF), _F32)

    kp_l, vp_l, ks_l, vs_l, as_l, cp_l, cs_l = [], [], [], [], [], [], []
    yp = ys = None
    for l in range(depth):
        last = l == depth - 1
        w_in_b = w_in[l].astype(_BF16)
        g1 = norm1_g[l][None, :]
        lng, lnb = ln_v_g[l][None, :], ln_v_b[l][None, :]
        wpa, wpb, wo = w_pa[l].astype(_BF16), w_pb[l].astype(_BF16), w_o[l].astype(_BF16)
        g2 = norm2_g[l][None, :]
        wup, wdn = w_up[l].astype(_BF16), w_down[l].astype(_BF16)
        cw, cb = conv_w[l], conv_b[l][None, :]
        gf = norm_f[None, :]
        wmix_p = jnp.where(tril[None], w_s[l], 0.0).astype(_BF16)
        bmix_p = jnp.broadcast_to(b_s[l][:, :, None], (A_GROUPS, CHUNK, CHUNK))
        w_small = jnp.where(tril[None, :tdec, :tdec], w_s[l][:, :tdec, :tdec], 0.0)
        wmix_s = jnp.einsum("ab,gts->gatbs", eye_s, w_small).reshape(A_GROUPS, CHUNK, CHUNK)
        wmix_s = wmix_s.astype(_BF16)
        bmix_s = jnp.broadcast_to(jnp.tile(b_s[l][:, :tdec], (1, CHUNK // tdec))[:, :, None],
                                  (A_GROUPS, CHUNK, CHUNK))

        qp, kp, vp, yap, sgap, sgbp = _proj_call(xp, g1, w_in_b, lng, lnb, wmix_p, bmix_p,
                                                 tm=tm_p, emit_va=False)
        ybp = _moba_prompt_call(qp.reshape(nbp_, seq, B_WIDTH), kp.reshape(nbp_, seq, B_WIDTH),
                                vp.reshape(nbp_, seq, B_WIDTH), btab)
        outs = _post_call(xp, yap, ybp.reshape(nbp_ * seq, B_WIDTH), sgap, sgbp, wpa, wpb, wo, g2,
                          wup, cw, cb, wdn, gf, (zero_state,), tm=tm_p,
                          tiles_per_seq=seq // tm_p, per_row_state=False, final_norm=last)
        xp, cp = outs[0], outs[1]
        if last:
            yp = outs[2]

        qs, ks, vs, yas, sgas, sgbs, vas = _proj_call(xs, g1, w_in_b, lng, lnb, wmix_s, bmix_s,
                                                      tm=CHUNK, emit_va=True)
        qbd = (qs.reshape(nbs, 1, tdec, B_WIDTH) * hmask.reshape(B_HEADS, tdec, B_WIDTH)[None]
               ).reshape(nbs, rows, B_WIDTH)
        pad = ((0, 0), (0, tpad - tdec), (0, 0))
        knew = jnp.pad(ks.reshape(nbs, tdec, B_WIDTH), pad)
        vnew = jnp.pad(vs.reshape(nbs, tdec, B_WIDTH), pad)
        ybs = _moba_sample_call(page_flat, qbd, knew, vnew, bnear_s, bfar_s, bown_s, hmask, rsum,
                                cache_k2, cache_v2, layer=l, n_pool=n_pool, n_pages=n_pages)
        st = state_conv[l]
        s1 = jnp.broadcast_to(st[:, 1:2, :], (nbs, tdec, D_FF)).reshape(ns, D_FF)
        s2 = jnp.concatenate([st, jnp.zeros((nbs, tdec - (CONV_W - 1), D_FF), _F32)],
                             axis=1).reshape(ns, D_FF)
        outs = _post_call(xs, yas, ybs.reshape(ns, B_WIDTH).astype(_BF16), sgas, sgbs, wpa, wpb, wo,
                          g2, wup, cw, cb, wdn, gf, (s1, s2, tpos), tm=CHUNK, tiles_per_seq=1,
                          per_row_state=True, final_norm=last)
        xs, a_s = outs[0], outs[1]
        if last:
            ys = outs[2]

        kp_l.append(kp.reshape(nbp_, seq, B_HEADS, B_HEAD_DIM))
        vp_l.append(vp.reshape(nbp_, seq, B_HEADS, B_HEAD_DIM))
        ks_l.append(ks.reshape(nbs, tdec, B_HEADS, B_HEAD_DIM))
        vs_l.append(vs.reshape(nbs, tdec, B_HEADS, B_HEAD_DIM))
        as_l.append(vas.reshape(nbs, tdec, A_WIDTH))
        cp_l.append(cp)
        a_ext = jnp.concatenate([st, a_s.reshape(nbs, tdec, D_FF)], axis=1)
        cs_l.append(a_ext[:, -(CONV_W - 1):])

    return (yp.reshape(nbp_, seq, D_MODEL), ys.reshape(nbs, tdec, D_MODEL),
            jnp.stack(kp_l, 0), jnp.stack(vp_l, 0), jnp.stack(ks_l, 0), jnp.stack(vs_l, 0),
            jnp.stack(as_l, 0), jnp.stack(cp_l, 0), jnp.stack(cs_l, 0))
```

```python
import functools
import math

import jax
import jax.numpy as jnp
from jax import lax
from jax.experimental import pallas as pl
from jax.experimental.pallas import tpu as pltpu

D_MODEL = 1024
CHUNK = 128
A_WIDTH = D_MODEL // 2
A_GROUPS = 4
A_GROUP_DIM = A_WIDTH // A_GROUPS
B_HEADS = 8
B_HEAD_DIM = 64
B_WIDTH = B_HEADS * B_HEAD_DIM
MOBA_BLOCK = 256
MOBA_TOPK = 3
N_BUCKETS = 32
MAX_DISTANCE = 128
D_FF = 2816
CONV_W = 3
EPS = 1e-6
N_IN = 2 * A_WIDTH + 3 * B_WIDTH + 2 * D_MODEL

_C_U, _C_VA, _C_Q, _C_K, _C_V, _C_GA, _C_GB = 0, 512, 1024, 1536, 2048, 2560, 3584

_NEG = -1e30
_HEADS_PER_STEP = 2
_BLOCKS_PER_ITER = 2
_PAGES_PER_STEP = 8
_PAGES_PER_BLOCK = MOBA_BLOCK // CHUNK
_VMEM_LIMIT = 48 * 1024 * 1024
_SCALE = 1.0 / math.sqrt(B_HEAD_DIM)

_F32 = jnp.float32
_BF16 = jnp.bfloat16


def _gelu(x):
    return 0.5 * x * (1.0 + jnp.tanh(0.7978845608028654 * (x + 0.044715 * (x * x * x))))


def _sigmoid(x):
    return 1.0 / (1.0 + jnp.exp(-x))


def _rms(x, g):
    return x * lax.rsqrt(jnp.mean(x * x, axis=-1, keepdims=True) + EPS) * g


def _const_spec(shape):
    zeros = (0,) * len(shape)
    return pl.BlockSpec(shape, lambda *_: zeros, pipeline_mode=pl.Buffered(1))


def _proj_kernel(x_ref, g1_ref, w_ref, lng_ref, lnb_ref, wmix_ref, bmix_ref, *rest, tm, prompt):
    if prompt:
        (q_ref, kb_ref, vt_ref, kmean_ref, knat_ref, vnat_ref, ya_ref, sga_ref, sgb_ref) = rest[-9:]
    else:
        (q_ref, k_ref, v_ref, ya_ref, sga_ref, sgb_ref, va_ref) = rest
    xn = _rms(x_ref[...], g1_ref[...]).astype(_BF16)

    def proj(lo, width):
        return jnp.dot(xn, w_ref[:, lo:lo + width], preferred_element_type=_F32)

    q_ref[...] = proj(_C_Q, B_WIDTH)
    k = proj(_C_K, B_WIDTH)
    v = proj(_C_V, B_WIDTH)
    if prompt:
        vt = v.T
        kb_ref[...] = k.astype(_BF16)
        vt_ref[0, 0] = vt.astype(_BF16)
        kmean_ref[0] = jnp.mean(k, axis=0, keepdims=True)
        knat_ref[0] = k.T
        vnat_ref[0] = vt
    else:
        k_ref[...] = k
        v_ref[...] = v
    sga_ref[...] = _sigmoid(proj(_C_GA, D_MODEL)).astype(_BF16)
    sgb_ref[...] = _sigmoid(proj(_C_GB, D_MODEL)).astype(_BF16)

    u = _gelu(proj(_C_U, A_WIDTH))
    va = _gelu(proj(_C_VA, A_WIDTH))
    mu = jnp.mean(va, axis=-1, keepdims=True)
    vc = va - mu
    var = jnp.mean(vc * vc, axis=-1, keepdims=True)
    va = vc * lax.rsqrt(var + EPS) * lng_ref[...] + lnb_ref[...]
    if not prompt:
        va_ref[...] = va
    for c in range(tm // CHUNK):
        rows = slice(c * CHUNK, (c + 1) * CHUNK)
        for g in range(A_GROUPS):
            cols = slice(g * A_GROUP_DIM, (g + 1) * A_GROUP_DIM)
            mixed = jnp.dot(wmix_ref[g], va[rows, cols].astype(_BF16),
                            preferred_element_type=_F32) + bmix_ref[g]
            ya_ref[rows, cols] = (u[rows, cols] * mixed).astype(_BF16)


def _proj_call(x2, g1, w_in, lng, lnb, wmix, bmix, *, tm, prompt, layer=0, depth=1, seq=None,
               kv_prev=None):
    n = x2.shape[0]
    nt = n // tm
    row = lambda width: pl.BlockSpec((tm, width), lambda i: (i, 0))
    in_specs = [row(D_MODEL), _const_spec((1, D_MODEL)), _const_spec((D_MODEL, N_IN)),
                _const_spec((1, A_WIDTH)), _const_spec((1, A_WIDTH)),
                _const_spec((A_GROUPS, CHUNK, CHUNK)), _const_spec((A_GROUPS, CHUNK, CHUNK))]
    args = [x2, g1, w_in, lng, lnb, wmix, bmix]
    aliases = {}
    tail_shape = [jax.ShapeDtypeStruct((n, A_WIDTH), _BF16),
                  jax.ShapeDtypeStruct((n, D_MODEL), _BF16),
                  jax.ShapeDtypeStruct((n, D_MODEL), _BF16)]
    tail_specs = [row(A_WIDTH), row(D_MODEL), row(D_MODEL)]
    if prompt:
        assert tm == MOBA_BLOCK
        nb = seq // MOBA_BLOCK
        nseq = n // seq
        nat = jax.ShapeDtypeStruct((depth * nseq, B_WIDTH, seq), _F32)
        nat_spec = pl.BlockSpec((1, B_WIDTH, tm), lambda i: (layer * nseq + i // nb, 0, i % nb))
        out_shape = [jax.ShapeDtypeStruct((n, B_WIDTH), _F32),
                     jax.ShapeDtypeStruct((n, B_WIDTH), _BF16),
                     jax.ShapeDtypeStruct((n // seq, nb, B_WIDTH, MOBA_BLOCK), _BF16),
                     jax.ShapeDtypeStruct((nt, 1, B_WIDTH), _F32), nat, nat] + tail_shape
        out_specs = [row(B_WIDTH), row(B_WIDTH),
                     pl.BlockSpec((1, 1, B_WIDTH, MOBA_BLOCK), lambda i: (i // nb, i % nb, 0, 0)),
                     pl.BlockSpec((1, 1, B_WIDTH), lambda i: (i, 0, 0)), nat_spec, nat_spec
                     ] + tail_specs
        if kv_prev is not None:
            in_specs += [pl.BlockSpec(memory_space=pl.ANY)] * 2
            args += list(kv_prev)
            aliases = {7: 4, 8: 5}
    else:
        out_shape = [jax.ShapeDtypeStruct((n, B_WIDTH), _F32)] * 3 + tail_shape + [
            jax.ShapeDtypeStruct((n, A_WIDTH), _F32)]
        out_specs = [row(B_WIDTH)] * 3 + tail_specs + [row(A_WIDTH)]
    return pl.pallas_call(
        functools.partial(_proj_kernel, tm=tm, prompt=prompt),
        grid=(nt,),
        in_specs=in_specs,
        out_specs=out_specs,
        out_shape=out_shape,
        input_output_aliases=aliases,
        compiler_params=pltpu.CompilerParams(dimension_semantics=("arbitrary",),
                                             vmem_limit_bytes=_VMEM_LIMIT),
        name="proj_mixer_a",
    )(*args)


def _topk_block_mask(gate, own, nb, width):
    blk = lax.broadcasted_iota(jnp.int32, (nb, width), 0)
    gm = jnp.where(blk < own, gate, -jnp.inf)
    rank = jnp.zeros((nb, width), _F32)
    for jp in range(nb):
        r = gm[jp:jp + 1, :]
        beats = jnp.where(r > gm, 1.0, jnp.where(r == gm, jnp.where(blk > jp, 1.0, 0.0), 0.0))
        rank = rank + beats
    past_sel = jnp.where(blk < own, jnp.where(rank < MOBA_TOPK, 1.0, 0.0), 0.0)
    return jnp.where(blk == own, 1.0, past_sel)


def _moba_prompt_kernel(q_ref, k_ref, vt_ref, kmean_ref, btab_ref, o_ref, sel_scr, *, nb):
    i = pl.program_id(2)
    blk = MOBA_BLOCK
    heads = range(_HEADS_PER_STEP)

    qt = q_ref[0].T
    feat = lax.broadcasted_iota(jnp.int32, qt.shape, 0)
    qz = []
    for hh in heads:
        in_head = (feat >= hh * B_HEAD_DIM) & (feat < (hh + 1) * B_HEAD_DIM)
        qh = jnp.where(in_head, qt, 0.0)
        gate = jnp.dot(kmean_ref[0], qh, preferred_element_type=_F32,
                       precision=lax.Precision.HIGHEST)
        sel_scr[hh] = _topk_block_mask(gate, i, nb, blk)
        qz.append((qh * _SCALE).astype(_BF16))

    def body(it, carry):
        steps = [it * _BLOCKS_PER_ITER + u for u in range(_BLOCKS_PER_ITER)]
        js = [jnp.maximum(i - st, 0) for st in steps]
        live = [jnp.where(st <= i, 1.0, 0.0) for st in steps]
        kjs = [k_ref[0, pl.ds(pl.multiple_of(j * blk, blk), blk), :] for j in js]
        out = []
        for hh in heads:
            m, l, acc = carry[hh]
            ss = []
            for st, j, lv, kj in zip(steps, js, live, kjs):
                s = jnp.dot(kj, qz[hh], preferred_element_type=_F32)
                s = s + btab_ref[hh, jnp.minimum(st, 2)]
                ss.append(jnp.where(sel_scr[hh, pl.ds(j, 1), :] * lv > 0.5, s, _NEG))
            m_new = m
            for s in ss:
                m_new = jnp.maximum(m_new, jnp.max(s, axis=0, keepdims=True))
            alpha = jnp.exp(m - m_new)
            l = alpha * l
            acc = alpha * acc
            for s, j in zip(ss, js):
                p = jnp.exp(s - m_new)
                l = l + jnp.sum(p, axis=0, keepdims=True)
                acc = acc + jnp.dot(vt_ref[0, j, hh * B_HEAD_DIM:(hh + 1) * B_HEAD_DIM, :],
                                    p.astype(_BF16), preferred_element_type=_F32)
            out.append((m_new, l, acc))
        return tuple(out)

    init = tuple((jnp.full((1, blk), _NEG, _F32), jnp.zeros((1, blk), _F32),
                  jnp.zeros((B_HEAD_DIM, blk), _F32)) for _ in heads)
    n_iter = (i + _BLOCKS_PER_ITER) // _BLOCKS_PER_ITER
    fin = lax.fori_loop(0, n_iter, body, init)
    o_ref[0] = jnp.concatenate([acc / l for _, l, acc in fin], axis=0).T.astype(_BF16)


def _moba_prompt_call(q3, kb3, vt4, kmean3, btab):
    b, t, _ = q3.shape
    nb = t // MOBA_BLOCK
    lanes = _HEADS_PER_STEP * B_HEAD_DIM
    return pl.pallas_call(
        functools.partial(_moba_prompt_kernel, nb=nb),
        grid=(b, B_HEADS // _HEADS_PER_STEP, nb),
        in_specs=[pl.BlockSpec((1, MOBA_BLOCK, lanes), lambda bb, hp, i: (bb, i, hp)),
                  pl.BlockSpec((1, t, lanes), lambda bb, hp, i: (bb, 0, hp)),
                  pl.BlockSpec((1, nb, lanes, MOBA_BLOCK), lambda bb, hp, i: (bb, 0, hp, 0)),
                  pl.BlockSpec((1, nb, lanes), lambda bb, hp, i: (bb, 0, hp)),
                  pl.BlockSpec((_HEADS_PER_STEP, 3, MOBA_BLOCK, MOBA_BLOCK),
                               lambda bb, hp, i: (hp, 0, 0, 0))],
        out_specs=pl.BlockSpec((1, MOBA_BLOCK, lanes), lambda bb, hp, i: (bb, i, hp)),
        out_shape=jax.ShapeDtypeStruct((b, t, B_WIDTH), _BF16),
        scratch_shapes=[pltpu.VMEM((_HEADS_PER_STEP, nb, MOBA_BLOCK), _F32)],
        compiler_params=pltpu.CompilerParams(
            dimension_semantics=("arbitrary", "arbitrary", "arbitrary"),
            vmem_limit_bytes=_VMEM_LIMIT),
        name="moba_prompt",
    )(q3, kb3, vt4, kmean3, btab)


def _moba_sample_kernel(pt_ref, qbd_ref, qbdt_ref, knew_ref, vnew_ref, bnear_ref, bfar_ref,
                        bown_ref, hmask_ref, rsum_ref, *rest, nb_past):
    del pt_ref
    pages = _PAGES_PER_STEP
    k_pages = rest[:pages]
    v_pages = rest[pages:2 * pages]
    o_ref = rest[2 * pages]
    gate_scr, m_scr, l_scr, acc_scr = rest[2 * pages + 1:]
    c = pl.program_id(1)
    rows = qbd_ref.shape[1]
    blocks_per_step = pages // _PAGES_PER_BLOCK
    nt_dims = (((1,), (1,)), ((), ()))

    qbd = qbd_ref[0]
    qz = (qbd * _SCALE).astype(_BF16)
    qbdt = qbdt_ref[0]

    for bl in range(blocks_per_step):
        jg = c * blocks_per_step + bl
        kts = [k_pages[bl * _PAGES_PER_BLOCK + p][0] for p in range(_PAGES_PER_BLOCK)]
        kt = jnp.concatenate(kts, axis=1)
        vt = jnp.concatenate([v_pages[bl * _PAGES_PER_BLOCK + p][0]
                              for p in range(_PAGES_PER_BLOCK)], axis=1)
        kmean = jnp.sum(functools.reduce(lambda a, b: a + b, kts), axis=-1,
                        keepdims=True) * (1.0 / MOBA_BLOCK)
        gate_scr[pl.ds(jg, 1), :] = jnp.sum(qbdt * kmean, axis=0, keepdims=True)
        s = jnp.dot(qz, kt.astype(_BF16), preferred_element_type=_F32)
        s = s + jnp.where(jg == nb_past - 1, bnear_ref[...], bfar_ref[:, 0:1])
        m = jnp.max(s, axis=-1, keepdims=True)
        p = jnp.exp(s - m)
        m_scr[jg] = jnp.broadcast_to(m, (rows, 128))
        l_scr[jg] = jnp.broadcast_to(jnp.sum(p, axis=-1, keepdims=True), (rows, 128))
        acc_scr[jg] = lax.dot_general(p.astype(_BF16), vt.astype(_BF16), nt_dims,
                                      preferred_element_type=_F32)

    @pl.when(c == pl.num_programs(1) - 1)
    def _():
        sel = _topk_block_mask(gate_scr[...], nb_past, nb_past, 128)
        selt = jnp.concatenate([sel, jnp.zeros((128 - nb_past, 128), _F32)], axis=0).T
        sel_cols = [selt[:rows, j:j + 1] > 0.5 for j in range(nb_past)]

        s_own = lax.dot_general(qbd * _SCALE, knew_ref[0], nt_dims,
                                preferred_element_type=_F32) + bown_ref[...]
        m_tot = jnp.max(s_own, axis=-1, keepdims=True)
        for j in range(nb_past):
            m_tot = jnp.maximum(m_tot, jnp.where(sel_cols[j], m_scr[j][:, 0:1], _NEG))
        p_own = jnp.exp(s_own - m_tot)
        l_tot = jnp.sum(p_own, axis=-1, keepdims=True)
        acc = jnp.dot(p_own, vnew_ref[0], preferred_element_type=_F32)
        for j in range(nb_past):
            w = jnp.where(sel_cols[j], jnp.exp(m_scr[j][:, 0:1] - m_tot), 0.0)
            l_tot = l_tot + w * l_scr[j][:, 0:1]
            acc = acc + w * acc_scr[j]
        out = acc / l_tot * hmask_ref[...]
        o_ref[0] = jnp.dot(rsum_ref[...], out, preferred_element_type=_F32,
                           precision=lax.Precision.HIGHEST)


def _moba_sample_call(page_flat, qbd, qbdt, knew, vnew, bnear, bfar, bown, hmask, rsum,
                      cache_kt, cache_vt, *, layer, n_pool, n_pages):
    nbatch, rows, _ = qbd.shape
    tdec = rsum.shape[0]
    pages = _PAGES_PER_STEP
    nb_past = n_pages // _PAGES_PER_BLOCK
    assert nb_past % 8 == 0 and nb_past <= 128 and rows <= 128
    base = layer * n_pool

    def page_spec(p):
        return pl.BlockSpec((1, B_WIDTH, CHUNK),
                            lambda bb, cc, pt, p=p: (base + pt[bb * n_pages + cc * pages + p], 0, 0))

    per_batch = lambda arr: pl.BlockSpec((1,) + arr.shape[1:], lambda bb, cc, pt: (bb, 0, 0))
    const = lambda arr: pl.BlockSpec(arr.shape, lambda bb, cc, pt: (0,) * arr.ndim)
    grid_spec = pltpu.PrefetchScalarGridSpec(
        num_scalar_prefetch=1,
        grid=(nbatch, n_pages // pages),
        in_specs=[per_batch(qbd), per_batch(qbdt), per_batch(knew), per_batch(vnew),
                  const(bnear), const(bfar), const(bown), const(hmask), const(rsum)]
                 + [page_spec(p) for p in range(pages)] * 2,
        out_specs=pl.BlockSpec((1, tdec, B_WIDTH), lambda bb, cc, pt: (bb, 0, 0)),
        scratch_shapes=[pltpu.VMEM((nb_past, 128), _F32),
                        pltpu.VMEM((nb_past, rows, 128), _F32),
                        pltpu.VMEM((nb_past, rows, 128), _F32),
                        pltpu.VMEM((nb_past, rows, B_WIDTH), _F32)])
    return pl.pallas_call(
        functools.partial(_moba_sample_kernel, nb_past=nb_past),
        grid_spec=grid_spec,
        out_shape=jax.ShapeDtypeStruct((nbatch, tdec, B_WIDTH), _F32),
        compiler_params=pltpu.CompilerParams(dimension_semantics=("arbitrary", "arbitrary"),
                                             vmem_limit_bytes=_VMEM_LIMIT),
        name="moba_sample",
    )(page_flat, qbd, qbdt, knew, vnew, bnear, bfar, bown, hmask, rsum,
      *([cache_kt] * pages), *([cache_vt] * pages))


def _post_kernel(x_ref, ya_ref, yb_ref, sga_ref, sgb_ref, wpa_ref, wpb_ref, wo_ref, g2_ref,
                 wup_ref, cw_ref, cb_ref, wdn_ref, gf_ref, *rest, tm, tiles_per_seq, per_row_state,
                 final_norm):
    if per_row_state:
        s1_ref, s2_ref, tpos_ref = rest[:3]
        rest = rest[3:]
    else:
        st_ref = rest[0]
        rest = rest[1:]
    xo_ref, a_ref = rest[:2]
    rest = rest[2:]
    if final_norm:
        y_ref = rest[0]
        rest = rest[1:]
    if not per_row_state:
        carry_scr = rest[0]

    ma = jnp.dot(ya_ref[...], wpa_ref[...], preferred_element_type=_F32)
    mb = jnp.dot(yb_ref[...], wpb_ref[...], preferred_element_type=_F32)
    merged = sga_ref[...].astype(_F32) * ma + sgb_ref[...].astype(_F32) * mb
    x1 = x_ref[...] + jnp.dot(merged.astype(_BF16), wo_ref[...], preferred_element_type=_F32)

    xn = _rms(x1, g2_ref[...]).astype(_BF16)
    a = jnp.dot(xn, wup_ref[:, :D_FF], preferred_element_type=_F32)
    gate = jnp.dot(xn, wup_ref[:, D_FF:], preferred_element_type=_F32)

    prev1 = pltpu.roll(a, 1, 0)
    prev2 = pltpu.roll(a, 2, 0)
    if per_row_state:
        tpos = tpos_ref[...]
        prev1 = jnp.where(tpos >= 1, prev1, s1_ref[...])
        prev2 = jnp.where(tpos >= 2, prev2, s2_ref[...])
        a_ref[...] = a
    else:
        i = pl.program_id(0)

        @pl.when(i % tiles_per_seq == 0)
        def _():
            carry_scr[...] = st_ref[0]

        row = lax.broadcasted_iota(jnp.int32, a.shape, 0)
        c0 = carry_scr[0:1, :]
        c1 = carry_scr[1:2, :]
        prev1 = jnp.where(row == 0, c1, prev1)
        prev2 = jnp.where(row == 0, c0, jnp.where(row == 1, c1, prev2))
        tail = a[tm - (CONV_W - 1):, :]
        carry_scr[...] = tail
        a_ref[0] = tail
    conv = cb_ref[...] + prev2 * cw_ref[0:1, :] + prev1 * cw_ref[1:2, :] + a * cw_ref[2:3, :]
    act = (_gelu(conv) * gate).astype(_BF16)
    x2 = x1 + jnp.dot(act, wdn_ref[...], preferred_element_type=_F32)
    xo_ref[...] = x2
    if final_norm:
        y_ref[...] = _rms(x2, gf_ref[...])


def _post_call(x2, ya, yb, sga, sgb, wpa, wpb, wo, g2, wup, cw, cb, wdn, gf, state_args, *,
               tm, tiles_per_seq, per_row_state, final_norm):
    n = x2.shape[0]
    row = lambda width: pl.BlockSpec((tm, width), lambda i: (i, 0))
    in_specs = [row(D_MODEL), row(A_WIDTH), row(B_WIDTH), row(D_MODEL), row(D_MODEL),
                _const_spec((A_WIDTH, D_MODEL)), _const_spec((B_WIDTH, D_MODEL)),
                _const_spec((D_MODEL, D_MODEL)), _const_spec((1, D_MODEL)),
                _const_spec((D_MODEL, 2 * D_FF)), _const_spec((CONV_W, D_FF)),
                _const_spec((1, D_FF)), _const_spec((D_FF, D_MODEL)), _const_spec((1, D_MODEL))]
    out_shape = [jax.ShapeDtypeStruct((n, D_MODEL), _F32)]
    out_specs = [row(D_MODEL)]
    scratch = []
    if per_row_state:
        in_specs += [row(D_FF), row(D_FF), row(D_FF)]
        out_shape.append(jax.ShapeDtypeStruct((n, D_FF), _F32))
        out_specs.append(row(D_FF))
    else:
        nseq = n // (tm * tiles_per_seq)
        in_specs.append(pl.BlockSpec((1, CONV_W - 1, D_FF), lambda i: (i // tiles_per_seq, 0, 0)))
        out_shape.append(jax.ShapeDtypeStruct((nseq, CONV_W - 1, D_FF), _F32))
        out_specs.append(pl.BlockSpec((1, CONV_W - 1, D_FF), lambda i: (i // tiles_per_seq, 0, 0)))
        scratch.append(pltpu.VMEM((CONV_W - 1, D_FF), _F32))
    if final_norm:
        out_shape.append(jax.ShapeDtypeStruct((n, D_MODEL), _F32))
        out_specs.append(row(D_MODEL))
    return pl.pallas_call(
        functools.partial(_post_kernel, tm=tm, tiles_per_seq=tiles_per_seq,
                          per_row_state=per_row_state, final_norm=final_norm),
        grid=(n // tm,),
        in_specs=in_specs,
        out_specs=out_specs,
        out_shape=out_shape,
        scratch_shapes=scratch,
        compiler_params=pltpu.CompilerParams(dimension_semantics=("arbitrary",),
                                             vmem_limit_bytes=_VMEM_LIMIT),
        name="merge_convffn",
    )(x2, ya, yb, sga, sgb, wpa, wpb, wo, g2, wup, cw, cb, wdn, gf, *state_args)


def _t5_bucket(rel):
    max_exact = N_BUCKETS // 2
    relf = jnp.maximum(rel, 1).astype(_F32)
    large = max_exact + (jnp.log(relf / max_exact) / math.log(MAX_DISTANCE / max_exact)
                         * (N_BUCKETS - max_exact)).astype(jnp.int32)
    large = jnp.minimum(large, N_BUCKETS - 1)
    return jnp.where(rel < max_exact, rel, large)


def _far_bucket_is_last(min_rel):
    max_exact = N_BUCKETS // 2
    v = max_exact + int(math.log(min_rel / max_exact) / math.log(MAX_DISTANCE / max_exact)
                        * (N_BUCKETS - max_exact) - 1e-6)
    return v >= N_BUCKETS - 1


def _toeplitz(diag, n):
    width = 2 * n
    flat = jnp.tile(diag, (1,) * (diag.ndim - 1) + (n,))
    skew = flat[..., :n * (width - 1)].reshape(diag.shape[:-1] + (n, width - 1))
    return skew[..., n - 1:2 * n - 1]


def _prompt_bias_tables(rel_bias):
    assert _far_bucket_is_last(MOBA_BLOCK + 1)
    bt = rel_bias.T.astype(_F32)
    rel = jnp.arange(2 * MOBA_BLOCK, dtype=jnp.int32) - (MOBA_BLOCK - 1)
    own = jnp.where(rel >= 0, bt[:, _t5_bucket(jnp.maximum(rel, 0))], _NEG)
    prev = bt[:, _t5_bucket(rel + MOBA_BLOCK)]
    own, prev = _toeplitz(own, MOBA_BLOCK), _toeplitz(prev, MOBA_BLOCK)
    far = jnp.broadcast_to(bt[:, N_BUCKETS - 1][:, None, None], own.shape)
    return jnp.stack([own, prev, far], axis=1)


def _sample_bias_tables(rel_bias, tdec, tpad):
    assert _far_bucket_is_last(MOBA_BLOCK + 1)
    bt = rel_bias.T.astype(_F32)
    tok = jnp.arange(tdec, dtype=jnp.int32)[:, None]
    key = jnp.arange(MOBA_BLOCK, dtype=jnp.int32)[None, :]
    near = bt[:, _t5_bucket(MOBA_BLOCK + tok - key)]
    far = jnp.broadcast_to(bt[:, N_BUCKETS - 1][:, None, None], (B_HEADS, tdec, 128))
    new = jnp.arange(tpad, dtype=jnp.int32)[None, :]
    own = jnp.where((tok - new >= 0) & (new < tdec), bt[:, _t5_bucket(jnp.maximum(tok - new, 0))],
                    _NEG)
    flat = lambda a: a.reshape(B_HEADS * tdec, a.shape[-1])
    return flat(near), flat(far), flat(own)


def kernel(x_prompt, x_sample, cache_k, cache_v, state_conv, page_table, w_in, ln_v_g, ln_v_b, w_s,
           b_s, w_pa, w_pb, w_o, norm1_g, norm2_g, w_up, conv_w, conv_b, w_down, rel_bias, norm_f):
    nbp_, seq, _ = x_prompt.shape
    nbs, tdec, _ = x_sample.shape
    depth, n_pool, page_size = cache_k.shape[:3]
    n_pages = page_table.shape[1]
    ns = nbs * tdec
    np_ = nbp_ * seq
    assert page_size == CHUNK and n_pages % _PAGES_PER_STEP == 0 and seq % MOBA_BLOCK == 0
    assert _PAGES_PER_STEP % _PAGES_PER_BLOCK == 0
    assert CONV_W - 1 <= tdec <= CHUNK and ns % CHUNK == 0 and CHUNK % tdec == 0
    tpad = -(-tdec // 8) * 8
    tm_p = MOBA_BLOCK
    nb = seq // MOBA_BLOCK
    nb_past = n_pages // _PAGES_PER_BLOCK
    rows = B_HEADS * tdec

    xp = x_prompt.reshape(np_, D_MODEL)
    xs = x_sample.reshape(ns, D_MODEL)
    cache_kt = cache_k.transpose(0, 1, 3, 4, 2).reshape(depth * n_pool, B_WIDTH, page_size)
    cache_vt = cache_v.transpose(0, 1, 3, 4, 2).reshape(depth * n_pool, B_WIDTH, page_size)
    page_flat = page_table.reshape(-1).astype(jnp.int32)

    btab = _prompt_bias_tables(rel_bias)
    bnear_s, bfar_s, bown_s = _sample_bias_tables(rel_bias, tdec, tpad)
    row_head = jnp.arange(rows, dtype=jnp.int32)[:, None] // tdec
    hmask = (jnp.arange(B_WIDTH, dtype=jnp.int32)[None, :] // B_HEAD_DIM == row_head).astype(_F32)
    rsum = (jnp.arange(rows, dtype=jnp.int32)[None, :] % tdec
            == jnp.arange(tdec, dtype=jnp.int32)[:, None]).astype(_F32)
    tpos = jnp.broadcast_to((jnp.arange(ns, dtype=jnp.int32) % tdec)[:, None], (ns, D_FF))
    tril = jnp.tril(jnp.ones((CHUNK, CHUNK), bool))
    eye_s = jnp.eye(CHUNK // tdec, dtype=_F32)
    zero_state = jnp.zeros((nbp_, CONV_W - 1, D_FF), _F32)

    ks_l, vs_l, as_l, cp_l, cs_l = [], [], [], [], []
    yp = ys = kv_nat = None
    for l in range(depth):
        last = l == depth - 1
        w_in_b = w_in[l].astype(_BF16)
        g1 = norm1_g[l][None, :]
        lng, lnb = ln_v_g[l][None, :], ln_v_b[l][None, :]
        wpa, wpb, wo = w_pa[l].astype(_BF16), w_pb[l].astype(_BF16), w_o[l].astype(_BF16)
        g2 = norm2_g[l][None, :]
        wup, wdn = w_up[l].astype(_BF16), w_down[l].astype(_BF16)
        cw, cb = conv_w[l], conv_b[l][None, :]
        gf = norm_f[None, :]
        wmix_p = jnp.where(tril[None], w_s[l], 0.0).astype(_BF16)
        bmix_p = jnp.broadcast_to(b_s[l][:, :, None], (A_GROUPS, CHUNK, CHUNK))
        w_small = jnp.where(tril[None, :tdec, :tdec], w_s[l][:, :tdec, :tdec], 0.0)
        wmix_s = jnp.einsum("ab,gts->gatbs", eye_s, w_small).reshape(A_GROUPS, CHUNK, CHUNK)
        wmix_s = wmix_s.astype(_BF16)
        bmix_s = jnp.broadcast_to(jnp.tile(b_s[l][:, :tdec], (1, CHUNK // tdec))[:, :, None],
                                  (A_GROUPS, CHUNK, CHUNK))

        (qp, kbp, vtp, kmeanp, knat, vnat, yap, sgap, sgbp) = _proj_call(
            xp, g1, w_in_b, lng, lnb, wmix_p, bmix_p, tm=tm_p, prompt=True, layer=l, depth=depth,
            seq=seq, kv_prev=kv_nat)
        kv_nat = (knat, vnat)
        ybp = _moba_prompt_call(qp.reshape(nbp_, seq, B_WIDTH), kbp.reshape(nbp_, seq, B_WIDTH),
                                vtp, kmeanp.reshape(nbp_, nb, B_WIDTH), btab)
        outs = _post_call(xp, yap, ybp.reshape(np_, B_WIDTH), sgap, sgbp, wpa, wpb, wo, g2,
                          wup, cw, cb, wdn, gf, (zero_state,), tm=tm_p,
                          tiles_per_seq=seq // tm_p, per_row_state=False, final_norm=last)
        xp, cp = outs[0], outs[1]
        if last:
            yp = outs[2]

        qs, ks, vs, yas, sgas, sgbs, vas = _proj_call(xs, g1, w_in_b, lng, lnb, wmix_s, bmix_s,
                                                      tm=CHUNK, prompt=False)
        ks4 = ks.reshape(nbs, tdec, B_HEADS, B_HEAD_DIM)
        vs4 = vs.reshape(nbs, tdec, B_HEADS, B_HEAD_DIM)
        qbd = (qs.reshape(nbs, 1, tdec, B_WIDTH) * hmask.reshape(B_HEADS, tdec, B_WIDTH)[None]
               ).reshape(nbs, rows, B_WIDTH)
        qbdt = jnp.pad(qbd.transpose(0, 2, 1), ((0, 0), (0, 0), (0, 128 - rows)))
        pad = ((0, 0), (0, tpad - tdec), (0, 0))
        knew = jnp.pad(ks.reshape(nbs, tdec, B_WIDTH), pad)
        vnew = jnp.pad(vs.reshape(nbs, tdec, B_WIDTH), pad)
        ybs = _moba_sample_call(page_flat, qbd, qbdt, knew, vnew, bnear_s, bfar_s, bown_s, hmask,
                                rsum, cache_kt, cache_vt, layer=l, n_pool=n_pool, n_pages=n_pages)
        ybs = ybs.reshape(ns, B_WIDTH)
        st = state_conv[l]
        s1 = jnp.broadcast_to(st[:, 1:2, :], (nbs, tdec, D_FF)).reshape(ns, D_FF)
        s2 = jnp.concatenate([st, jnp.zeros((nbs, tdec - (CONV_W - 1), D_FF), _F32)],
                             axis=1).reshape(ns, D_FF)
        outs = _post_call(xs, yas, ybs.astype(_BF16), sgas, sgbs, wpa, wpb, wo,
                          g2, wup, cw, cb, wdn, gf, (s1, s2, tpos), tm=CHUNK, tiles_per_seq=1,
                          per_row_state=True, final_norm=last)
        xs, a_s = outs[0], outs[1]
        if last:
            ys = outs[2]

        ks_l.append(ks4)
        vs_l.append(vs4)
        as_l.append(vas.reshape(nbs, tdec, A_WIDTH))
        cp_l.append(cp)
        a_ext = jnp.concatenate([st, a_s.reshape(nbs, tdec, D_FF)], axis=1)
        cs_l.append(a_ext[:, -(CONV_W - 1):])

    def kv_out(t):
        return t.reshape(depth, nbp_, B_HEADS, B_HEAD_DIM, seq).transpose(0, 1, 4, 2, 3)

    return (yp.reshape(nbp_, seq, D_MODEL), ys.reshape(nbs, tdec, D_MODEL),
            kv_out(kv_nat[0]), kv_out(kv_nat[1]),
            jnp.stack(ks_l, 0), jnp.stack(vs_l, 0),
            jnp.stack(as_l, 0), jnp.stack(cp_l, 0), jnp.stack(cs_l, 0))
```

```python
import functools
import math

import jax
import jax.numpy as jnp
from jax import lax
from jax.experimental import pallas as pl
from jax.experimental.pallas import tpu as pltpu

D_MODEL = 1024
CHUNK = 128
A_WIDTH = D_MODEL // 2
A_GROUPS = 4
A_GROUP_DIM = A_WIDTH // A_GROUPS
B_HEADS = 8
B_HEAD_DIM = 64
B_WIDTH = B_HEADS * B_HEAD_DIM
MOBA_BLOCK = 256
MOBA_TOPK = 3
N_BUCKETS = 32
MAX_DISTANCE = 128
D_FF = 2816
CONV_W = 3
EPS = 1e-6
N_IN = 2 * A_WIDTH + 3 * B_WIDTH + 2 * D_MODEL

_C_U, _C_VA, _C_Q, _C_K, _C_V, _C_GA, _C_GB = 0, 512, 1024, 1536, 2048, 2560, 3584

_NEG = -1e30
_HEADS_PER_PAIR = 2
_PAIRS_PER_STEP = 2
_BLOCKS_PER_ITER = 2
_TAB_OWN, _TAB_PREV, _TAB_FAR, _TAB_DEAD, _N_TABLES = 0, 1, 2, 3, 4
_PAIR_LANES = _HEADS_PER_PAIR * B_HEAD_DIM
_K_LANES = 2 * _PAIR_LANES
_V_ROWS = B_HEAD_DIM + 16
_LOG2E = math.log2(math.e)
_PAGES_PER_STEP = 8
_PAGES_PER_BLOCK = MOBA_BLOCK // CHUNK
_VMEM_LIMIT = 48 * 1024 * 1024
_SCALE = 1.0 / math.sqrt(B_HEAD_DIM)

_F32 = jnp.float32
_BF16 = jnp.bfloat16


def _gelu(x):
    return 0.5 * x * (1.0 + jnp.tanh(0.7978845608028654 * (x + 0.044715 * (x * x * x))))


def _sigmoid(x):
    return 1.0 / (1.0 + jnp.exp(-x))


def _rms(x, g):
    return x * lax.rsqrt(jnp.mean(x * x, axis=-1, keepdims=True) + EPS) * g


def _const_spec(shape):
    zeros = (0,) * len(shape)
    return pl.BlockSpec(shape, lambda *_: zeros, pipeline_mode=pl.Buffered(1))


def _proj_kernel(x_ref, g1_ref, w_ref, lng_ref, lnb_ref, wmix_ref, bmix_ref, *rest, tm, prompt,
                 nb=None):
    if prompt:
        (q_ref, kb_ref, vt_ref, kmean_ref, knat_ref, vnat_ref, ya_ref, sga_ref, sgb_ref) = rest[-9:]
    else:
        (q_ref, k_ref, v_ref, ya_ref, sga_ref, sgb_ref, va_ref) = rest
    xn = _rms(x_ref[...], g1_ref[...]).astype(_BF16)

    def proj(lo, width):
        return jnp.dot(xn, w_ref[:, lo:lo + width], preferred_element_type=_F32)

    q_ref[...] = proj(_C_Q, B_WIDTH)
    k = proj(_C_K, B_WIDTH)
    v = proj(_C_V, B_WIDTH)
    if prompt:
        vt = v.T
        kmean_ref[0] = jnp.mean(k, axis=0, keepdims=True)
        knat_ref[0] = k.T
        vnat_ref[0] = vt
        kb = k.astype(_BF16)
        blk_id = pl.program_id(0) % nb
        lane = lax.broadcasted_iota(jnp.int32, (tm, _K_LANES - _PAIR_LANES), 1)
        onehot = jnp.where(lane == blk_id, 1.0, 0.0).astype(_BF16)
        for hp in range(B_HEADS // _HEADS_PER_PAIR):
            kb_ref[:, hp * _K_LANES:hp * _K_LANES + _PAIR_LANES] = (
                kb[:, hp * _PAIR_LANES:(hp + 1) * _PAIR_LANES])
            kb_ref[:, hp * _K_LANES + _PAIR_LANES:(hp + 1) * _K_LANES] = onehot
        vtb = vt.astype(_BF16)
        row = lax.broadcasted_iota(jnp.int32, (_V_ROWS - B_HEAD_DIM, tm), 0)
        ones_row = jnp.where(row == 0, 1.0, 0.0).astype(_BF16)
        for h in range(B_HEADS):
            vt_ref[0, 0, h * _V_ROWS:h * _V_ROWS + B_HEAD_DIM, :] = (
                vtb[h * B_HEAD_DIM:(h + 1) * B_HEAD_DIM, :])
            vt_ref[0, 0, h * _V_ROWS + B_HEAD_DIM:(h + 1) * _V_ROWS, :] = ones_row
    else:
        k_ref[...] = k
        v_ref[...] = v
    sga_ref[...] = _sigmoid(proj(_C_GA, D_MODEL)).astype(_BF16)
    sgb_ref[...] = _sigmoid(proj(_C_GB, D_MODEL)).astype(_BF16)

    u = _gelu(proj(_C_U, A_WIDTH))
    va = _gelu(proj(_C_VA, A_WIDTH))
    mu = jnp.mean(va, axis=-1, keepdims=True)
    vc = va - mu
    var = jnp.mean(vc * vc, axis=-1, keepdims=True)
    va = vc * lax.rsqrt(var + EPS) * lng_ref[...] + lnb_ref[...]
    if not prompt:
        va_ref[...] = va
    for c in range(tm // CHUNK):
        rows = slice(c * CHUNK, (c + 1) * CHUNK)
        for g in range(A_GROUPS):
            cols = slice(g * A_GROUP_DIM, (g + 1) * A_GROUP_DIM)
            mixed = jnp.dot(wmix_ref[g], va[rows, cols].astype(_BF16),
                            preferred_element_type=_F32) + bmix_ref[g]
            ya_ref[rows, cols] = (u[rows, cols] * mixed).astype(_BF16)


def _proj_call(x2, g1, w_in, lng, lnb, wmix, bmix, *, tm, prompt, layer=0, depth=1, seq=None,
               kv_prev=None):
    n = x2.shape[0]
    nt = n // tm
    row = lambda width: pl.BlockSpec((tm, width), lambda i: (i, 0))
    in_specs = [row(D_MODEL), _const_spec((1, D_MODEL)), _const_spec((D_MODEL, N_IN)),
                _const_spec((1, A_WIDTH)), _const_spec((1, A_WIDTH)),
                _const_spec((A_GROUPS, CHUNK, CHUNK)), _const_spec((A_GROUPS, CHUNK, CHUNK))]
    args = [x2, g1, w_in, lng, lnb, wmix, bmix]
    aliases = {}
    tail_shape = [jax.ShapeDtypeStruct((n, A_WIDTH), _BF16),
                  jax.ShapeDtypeStruct((n, D_MODEL), _BF16),
                  jax.ShapeDtypeStruct((n, D_MODEL), _BF16)]
    tail_specs = [row(A_WIDTH), row(D_MODEL), row(D_MODEL)]
    nb = None
    if prompt:
        assert tm == MOBA_BLOCK
        nb = seq // MOBA_BLOCK
        assert nb <= _K_LANES - _PAIR_LANES
        nseq = n // seq
        k_cols = B_HEADS // _HEADS_PER_PAIR * _K_LANES
        nat = jax.ShapeDtypeStruct((depth * nseq, B_WIDTH, seq), _F32)
        nat_spec = pl.BlockSpec((1, B_WIDTH, tm), lambda i: (layer * nseq + i // nb, 0, i % nb))
        out_shape = [jax.ShapeDtypeStruct((n, B_WIDTH), _F32),
                     jax.ShapeDtypeStruct((n, k_cols), _BF16),
                     jax.ShapeDtypeStruct((nseq, nb, B_HEADS * _V_ROWS, MOBA_BLOCK), _BF16),
                     jax.ShapeDtypeStruct((nt, 1, B_WIDTH), _F32), nat, nat] + tail_shape
        out_specs = [row(B_WIDTH), row(k_cols),
                     pl.BlockSpec((1, 1, B_HEADS * _V_ROWS, MOBA_BLOCK),
                                  lambda i: (i // nb, i % nb, 0, 0)),
                     pl.BlockSpec((1, 1, B_WIDTH), lambda i: (i, 0, 0)), nat_spec, nat_spec
                     ] + tail_specs
        if kv_prev is not None:
            in_specs += [pl.BlockSpec(memory_space=pl.ANY)] * 2
            args += list(kv_prev)
            aliases = {7: 4, 8: 5}
    else:
        out_shape = [jax.ShapeDtypeStruct((n, B_WIDTH), _F32)] * 3 + tail_shape + [
            jax.ShapeDtypeStruct((n, A_WIDTH), _F32)]
        out_specs = [row(B_WIDTH)] * 3 + tail_specs + [row(A_WIDTH)]
    return pl.pallas_call(
        functools.partial(_proj_kernel, tm=tm, prompt=prompt, nb=nb),
        grid=(nt,),
        in_specs=in_specs,
        out_specs=out_specs,
        out_shape=out_shape,
        input_output_aliases=aliases,
        compiler_params=pltpu.CompilerParams(dimension_semantics=("arbitrary",),
                                             vmem_limit_bytes=_VMEM_LIMIT),
        name="proj_mixer_a",
    )(*args)


def _topk_block_mask(gate, own, nb, width):
    blk = lax.broadcasted_iota(jnp.int32, (nb, width), 0)
    gm = jnp.where(blk < own, gate, -jnp.inf)
    rank = jnp.zeros((nb, width), _F32)
    for jp in range(nb):
        r = gm[jp:jp + 1, :]
        beats = jnp.where(r > gm, 1.0, jnp.where(r == gm, jnp.where(blk > jp, 1.0, 0.0), 0.0))
        rank = rank + beats
    past_sel = jnp.where(blk < own, jnp.where(rank < MOBA_TOPK, 1.0, 0.0), 0.0)
    return jnp.where(blk == own, 1.0, past_sel)


def _moba_prompt_kernel(q_ref, k_ref, vt_ref, kmean_ref, btab_ref, o_ref, s_scr, *, nb):
    i = pl.program_id(2)
    blk = MOBA_BLOCK
    heads = range(_PAIRS_PER_STEP * _HEADS_PER_PAIR)

    qt = q_ref[0].T
    feat = lax.broadcasted_iota(jnp.int32, (_PAIR_LANES, blk), 0)
    qz = []
    for hh in heads:
        pair, sub = divmod(hh, _HEADS_PER_PAIR)
        in_head = (feat >= sub * B_HEAD_DIM) & (feat < (sub + 1) * B_HEAD_DIM)
        qh = jnp.where(in_head, qt[pair * _PAIR_LANES:(pair + 1) * _PAIR_LANES], 0.0)
        gate = jnp.dot(kmean_ref[0, :, pair * _PAIR_LANES:(pair + 1) * _PAIR_LANES], qh,
                       preferred_element_type=_F32, precision=lax.Precision.HIGHEST)
        mask_rows = jnp.where(_topk_block_mask(gate, i, nb, blk) > 0.5, 0.0, _NEG)
        pad = jnp.zeros((_K_LANES - _PAIR_LANES - nb, blk), _F32)
        qz.append(jnp.concatenate([qh * (_SCALE * _LOG2E), mask_rows, pad], axis=0).astype(_BF16))

    n_iter = lax.shift_right_logical(i + _BLOCKS_PER_ITER, 1)

    def slots(it):
        js = [it * _BLOCKS_PER_ITER + u for u in range(_BLOCKS_PER_ITER)]
        tabs = [jnp.where(j == i, _TAB_OWN, jnp.where(j == i - 1, _TAB_PREV,
                                                      jnp.where(j > i, _TAB_DEAD, _TAB_FAR)))
                for j in js]
        return js, [jnp.minimum(j, i) for j in js], tabs

    def logits_pass(it, ms):
        js, jcs, tabs = slots(it)
        out = []
        for hh in heads:
            pair = hh // _HEADS_PER_PAIR
            m = ms[hh]
            for j, jc, tab in zip(js, jcs, tabs):
                kj = k_ref[0, pl.ds(pl.multiple_of(jc * blk, blk), blk),
                           pair * _K_LANES:(pair + 1) * _K_LANES]
                s = jnp.dot(kj, qz[hh], preferred_element_type=_F32) + btab_ref[hh, tab]
                s_scr[hh, j] = s
                m = jnp.maximum(m, jnp.max(s, axis=0, keepdims=True))
            out.append(m)
        return tuple(out)

    ms = lax.fori_loop(0, n_iter, logits_pass,
                       tuple(jnp.full((1, blk), _NEG, _F32) for _ in heads))

    def value_pass(it, accs):
        js, jcs, _ = slots(it)
        out = []
        for hh in heads:
            acc = accs[hh]
            for j, jc in zip(js, jcs):
                p = jnp.exp2(s_scr[hh, j] - ms[hh]).astype(_BF16)
                acc = acc + jnp.dot(vt_ref[0, jc, hh * _V_ROWS:(hh + 1) * _V_ROWS, :], p,
                                    preferred_element_type=_F32)
            out.append(acc)
        return tuple(out)

    accs = lax.fori_loop(0, n_iter, value_pass,
                         tuple(jnp.zeros((_V_ROWS, blk), _F32) for _ in heads))
    outs = [acc[:B_HEAD_DIM] / acc[B_HEAD_DIM:B_HEAD_DIM + 1] for acc in accs]
    o_ref[0] = jnp.concatenate(outs, axis=0).T.astype(_BF16)


def _moba_prompt_call(q3, kb3, vt4, kmean3, btab):
    b, t, _ = q3.shape
    nb = t // MOBA_BLOCK
    g = _PAIRS_PER_STEP
    heads = g * _HEADS_PER_PAIR
    return pl.pallas_call(
        functools.partial(_moba_prompt_kernel, nb=nb),
        grid=(b, B_HEADS // heads, nb),
        in_specs=[pl.BlockSpec((1, MOBA_BLOCK, g * _PAIR_LANES), lambda bb, hp, i: (bb, i, hp)),
                  pl.BlockSpec((1, t, g * _K_LANES), lambda bb, hp, i: (bb, 0, hp)),
                  pl.BlockSpec((1, nb, heads * _V_ROWS, MOBA_BLOCK),
                               lambda bb, hp, i: (bb, 0, hp, 0)),
                  pl.BlockSpec((1, nb, g * _PAIR_LANES), lambda bb, hp, i: (bb, 0, hp)),
                  pl.BlockSpec((heads, _N_TABLES, MOBA_BLOCK, MOBA_BLOCK),
                               lambda bb, hp, i: (hp, 0, 0, 0), pipeline_mode=pl.Buffered(1))],
        out_specs=pl.BlockSpec((1, MOBA_BLOCK, g * _PAIR_LANES), lambda bb, hp, i: (bb, i, hp)),
        out_shape=jax.ShapeDtypeStruct((b, t, B_WIDTH), _BF16),
        scratch_shapes=[pltpu.VMEM((heads, nb + _BLOCKS_PER_ITER - 1, MOBA_BLOCK, MOBA_BLOCK), _F32)],
        compiler_params=pltpu.CompilerParams(
            dimension_semantics=("arbitrary", "arbitrary", "arbitrary"),
            vmem_limit_bytes=_VMEM_LIMIT),
        name="moba_prompt",
    )(q3, kb3, vt4, kmean3, btab)


def _moba_sample_kernel(pt_ref, qbd_ref, qbdt_ref, knew_ref, vnewt_ref, bnear_ref, bfar_ref,
                        bownt_ref, hmaskt_ref, rsum_ref, *rest, nb_past):
    del pt_ref
    pages = _PAGES_PER_STEP
    k_pages = rest[:pages]
    v_pages = rest[pages:2 * pages]
    o_ref = rest[2 * pages]
    gate_scr, m_scr, l_scr, acc_scr = rest[2 * pages + 1:]
    c = pl.program_id(1)
    rows = qbd_ref.shape[1]
    tdec = o_ref.shape[1]
    blocks_per_step = pages // _PAGES_PER_BLOCK

    qz = (qbd_ref[0] * _SCALE).astype(_BF16)
    qbdt = qbdt_ref[0]
    row_pad = jnp.zeros((128 - rows, MOBA_BLOCK), _F32)

    for bl in range(blocks_per_step):
        jg = c * blocks_per_step + bl
        kts = [k_pages[bl * _PAGES_PER_BLOCK + p][0] for p in range(_PAGES_PER_BLOCK)]
        kt = jnp.concatenate(kts, axis=1)
        vt = jnp.concatenate([v_pages[bl * _PAGES_PER_BLOCK + p][0]
                              for p in range(_PAGES_PER_BLOCK)], axis=1)
        kmean = jnp.sum(functools.reduce(lambda a, b: a + b, kts), axis=-1,
                        keepdims=True) * (1.0 / MOBA_BLOCK)
        gate_scr[pl.ds(jg, 1), :] = jnp.sum(qbdt * kmean, axis=0, keepdims=True)
        s = jnp.dot(qz, kt.astype(_BF16), preferred_element_type=_F32)
        s = s + jnp.where(jg == nb_past - 1, bnear_ref[...], bfar_ref[:, 0:1])
        st = jnp.concatenate([s, row_pad], axis=0).T
        m = jnp.max(st, axis=0, keepdims=True)
        p = jnp.exp(st - m)
        m_scr[pl.ds(jg, 1), :] = m
        l_scr[pl.ds(jg, 1), :] = jnp.sum(p, axis=0, keepdims=True)
        acc_scr[jg] = jnp.dot(vt.astype(_BF16), p.astype(_BF16), preferred_element_type=_F32)

    @pl.when(c == pl.num_programs(1) - 1)
    def _():
        sel = _topk_block_mask(gate_scr[...], nb_past, nb_past, 128) > 0.5
        m_blk = m_scr[...]
        s_own = jnp.dot(knew_ref[0], qbdt * _SCALE, preferred_element_type=_F32) + bownt_ref[...]
        m_tot = jnp.maximum(jnp.max(s_own, axis=0, keepdims=True),
                            jnp.max(jnp.where(sel, m_blk, _NEG), axis=0, keepdims=True))
        w = jnp.where(sel, jnp.exp(m_blk - m_tot), 0.0)
        p_own = jnp.exp(s_own - m_tot)
        l_tot = jnp.sum(p_own, axis=0, keepdims=True) + jnp.sum(w * l_scr[...], axis=0, keepdims=True)
        accs = [jnp.dot(vnewt_ref[0], p_own, preferred_element_type=_F32)]
        accs += [jnp.zeros_like(accs[0]) for _ in range(3)]
        for j in range(nb_past):
            accs[j % 4] = accs[j % 4] + w[j:j + 1, :] * acc_scr[j]
        out_t = ((accs[0] + accs[1]) + (accs[2] + accs[3])) / l_tot
        folded = lax.dot_general(rsum_ref[...], out_t * hmaskt_ref[...], (((1,), (1,)), ((), ())),
                                 preferred_element_type=_F32, precision=lax.Precision.HIGHEST)
        o_ref[0] = folded[:tdec]


def _moba_sample_call(page_flat, qbd, qbdt, knew, vnewt, bnear, bfar, bownt, hmaskt, rsum,
                      cache_kt, cache_vt, *, layer, n_pool, n_pages, tdec):
    nbatch, rows, _ = qbd.shape
    pages = _PAGES_PER_STEP
    nb_past = n_pages // _PAGES_PER_BLOCK
    assert nb_past % 8 == 0 and rows <= 128
    base = layer * n_pool

    def page_spec(p):
        return pl.BlockSpec((1, B_WIDTH, CHUNK),
                            lambda bb, cc, pt, p=p: (base + pt[bb * n_pages + cc * pages + p], 0, 0))

    per_batch = lambda arr: pl.BlockSpec((1,) + arr.shape[1:], lambda bb, cc, pt: (bb, 0, 0))
    const = lambda arr: pl.BlockSpec(arr.shape, lambda bb, cc, pt: (0,) * arr.ndim)
    grid_spec = pltpu.PrefetchScalarGridSpec(
        num_scalar_prefetch=1,
        grid=(nbatch, n_pages // pages),
        in_specs=[per_batch(qbd), per_batch(qbdt), per_batch(knew), per_batch(vnewt),
                  const(bnear), const(bfar), const(bownt), const(hmaskt), const(rsum)]
                 + [page_spec(p) for p in range(pages)] * 2,
        out_specs=pl.BlockSpec((1, tdec, B_WIDTH), lambda bb, cc, pt: (bb, 0, 0)),
        scratch_shapes=[pltpu.VMEM((nb_past, 128), _F32),
                        pltpu.VMEM((nb_past, 128), _F32),
                        pltpu.VMEM((nb_past, 128), _F32),
                        pltpu.VMEM((nb_past, B_WIDTH, 128), _F32)])
    return pl.pallas_call(
        functools.partial(_moba_sample_kernel, nb_past=nb_past),
        grid_spec=grid_spec,
        out_shape=jax.ShapeDtypeStruct((nbatch, tdec, B_WIDTH), _F32),
        compiler_params=pltpu.CompilerParams(dimension_semantics=("arbitrary", "arbitrary"),
                                             vmem_limit_bytes=_VMEM_LIMIT),
        name="moba_sample",
    )(page_flat, qbd, qbdt, knew, vnewt, bnear, bfar, bownt, hmaskt, rsum,
      *([cache_kt] * pages), *([cache_vt] * pages))


def _post_kernel(x_ref, ya_ref, yb_ref, sga_ref, sgb_ref, wpa_ref, wpb_ref, wo_ref, g2_ref,
                 wup_ref, cw_ref, cb_ref, wdn_ref, gf_ref, *rest, tm, tiles_per_seq, per_row_state,
                 final_norm):
    if per_row_state:
        s1_ref, s2_ref, tpos_ref = rest[:3]
        rest = rest[3:]
    else:
        st_ref = rest[0]
        rest = rest[1:]
    xo_ref, a_ref = rest[:2]
    rest = rest[2:]
    if final_norm:
        y_ref = rest[0]
        rest = rest[1:]
    if not per_row_state:
        carry_scr = rest[0]

    ma = jnp.dot(ya_ref[...], wpa_ref[...], preferred_element_type=_F32)
    mb = jnp.dot(yb_ref[...], wpb_ref[...], preferred_element_type=_F32)
    merged = sga_ref[...].astype(_F32) * ma + sgb_ref[...].astype(_F32) * mb
    x1 = x_ref[...] + jnp.dot(merged.astype(_BF16), wo_ref[...], preferred_element_type=_F32)

    xn = _rms(x1, g2_ref[...]).astype(_BF16)
    a = jnp.dot(xn, wup_ref[:, :D_FF], preferred_element_type=_F32)
    gate = jnp.dot(xn, wup_ref[:, D_FF:], preferred_element_type=_F32)

    prev1 = pltpu.roll(a, 1, 0)
    prev2 = pltpu.roll(a, 2, 0)
    if per_row_state:
        tpos = tpos_ref[...]
        prev1 = jnp.where(tpos >= 1, prev1, s1_ref[...])
        prev2 = jnp.where(tpos >= 2, prev2, s2_ref[...])
        a_ref[...] = a
    else:
        i = pl.program_id(0)

        @pl.when(i % tiles_per_seq == 0)
        def _():
            carry_scr[...] = st_ref[0]

        row = lax.broadcasted_iota(jnp.int32, a.shape, 0)
        c0 = carry_scr[0:1, :]
        c1 = carry_scr[1:2, :]
        prev1 = jnp.where(row == 0, c1, prev1)
        prev2 = jnp.where(row == 0, c0, jnp.where(row == 1, c1, prev2))
        tail = a[tm - (CONV_W - 1):, :]
        carry_scr[...] = tail
        a_ref[0] = tail
    conv = cb_ref[...] + prev2 * cw_ref[0:1, :] + prev1 * cw_ref[1:2, :] + a * cw_ref[2:3, :]
    act = (_gelu(conv) * gate).astype(_BF16)
    x2 = x1 + jnp.dot(act, wdn_ref[...], preferred_element_type=_F32)
    xo_ref[...] = x2
    if final_norm:
        y_ref[...] = _rms(x2, gf_ref[...])


def _post_call(x2, ya, yb, sga, sgb, wpa, wpb, wo, g2, wup, cw, cb, wdn, gf, state_args, *,
               tm, tiles_per_seq, per_row_state, final_norm):
    n = x2.shape[0]
    row = lambda width: pl.BlockSpec((tm, width), lambda i: (i, 0))
    in_specs = [row(D_MODEL), row(A_WIDTH), row(B_WIDTH), row(D_MODEL), row(D_MODEL),
                _const_spec((A_WIDTH, D_MODEL)), _const_spec((B_WIDTH, D_MODEL)),
                _const_spec((D_MODEL, D_MODEL)), _const_spec((1, D_MODEL)),
                _const_spec((D_MODEL, 2 * D_FF)), _const_spec((CONV_W, D_FF)),
                _const_spec((1, D_FF)), _const_spec((D_FF, D_MODEL)), _const_spec((1, D_MODEL))]
    out_shape = [jax.ShapeDtypeStruct((n, D_MODEL), _F32)]
    out_specs = [row(D_MODEL)]
    scratch = []
    if per_row_state:
        in_specs += [row(D_FF), row(D_FF), row(D_FF)]
        out_shape.append(jax.ShapeDtypeStruct((n, D_FF), _F32))
        out_specs.append(row(D_FF))
    else:
        nseq = n // (tm * tiles_per_seq)
        in_specs.append(pl.BlockSpec((1, CONV_W - 1, D_FF), lambda i: (i // tiles_per_seq, 0, 0)))
        out_shape.append(jax.ShapeDtypeStruct((nseq, CONV_W - 1, D_FF), _F32))
        out_specs.append(pl.BlockSpec((1, CONV_W - 1, D_FF), lambda i: (i // tiles_per_seq, 0, 0)))
        scratch.append(pltpu.VMEM((CONV_W - 1, D_FF), _F32))
    if final_norm:
        out_shape.append(jax.ShapeDtypeStruct((n, D_MODEL), _F32))
        out_specs.append(row(D_MODEL))
    return pl.pallas_call(
        functools.partial(_post_kernel, tm=tm, tiles_per_seq=tiles_per_seq,
                          per_row_state=per_row_state, final_norm=final_norm),
        grid=(n // tm,),
        in_specs=in_specs,
        out_specs=out_specs,
        out_shape=out_shape,
        scratch_shapes=scratch,
        compiler_params=pltpu.CompilerParams(dimension_semantics=("arbitrary",),
                                             vmem_limit_bytes=_VMEM_LIMIT),
        name="merge_convffn",
    )(x2, ya, yb, sga, sgb, wpa, wpb, wo, g2, wup, cw, cb, wdn, gf, *state_args)


def _t5_bucket(rel):
    max_exact = N_BUCKETS // 2
    relf = jnp.maximum(rel, 1).astype(_F32)
    large = max_exact + (jnp.log(relf / max_exact) / math.log(MAX_DISTANCE / max_exact)
                         * (N_BUCKETS - max_exact)).astype(jnp.int32)
    large = jnp.minimum(large, N_BUCKETS - 1)
    return jnp.where(rel < max_exact, rel, large)


def _far_bucket_is_last(min_rel):
    max_exact = N_BUCKETS // 2
    v = max_exact + int(math.log(min_rel / max_exact) / math.log(MAX_DISTANCE / max_exact)
                        * (N_BUCKETS - max_exact) - 1e-6)
    return v >= N_BUCKETS - 1


def _toeplitz(diag, n):
    width = 2 * n
    flat = jnp.tile(diag, (1,) * (diag.ndim - 1) + (n,))
    skew = flat[..., :n * (width - 1)].reshape(diag.shape[:-1] + (n, width - 1))
    return skew[..., n - 1:2 * n - 1]


def _prompt_bias_tables(rel_bias):
    assert _far_bucket_is_last(MOBA_BLOCK + 1)
    bt = rel_bias.T.astype(_F32)
    bt = (bt - bt[:, N_BUCKETS - 1:]) * _LOG2E
    rel = jnp.arange(2 * MOBA_BLOCK, dtype=jnp.int32) - (MOBA_BLOCK - 1)
    own = jnp.where(rel >= 0, bt[:, _t5_bucket(jnp.maximum(rel, 0))], _NEG)
    prev = bt[:, _t5_bucket(rel + MOBA_BLOCK)]
    own, prev = _toeplitz(own, MOBA_BLOCK), _toeplitz(prev, MOBA_BLOCK)
    tables = [None] * _N_TABLES
    tables[_TAB_OWN], tables[_TAB_PREV] = own, prev
    tables[_TAB_FAR], tables[_TAB_DEAD] = jnp.zeros_like(own), jnp.full_like(own, _NEG)
    return jnp.stack(tables, axis=1)


def _sample_bias_tables(rel_bias, tdec, tpad):
    assert _far_bucket_is_last(MOBA_BLOCK + 1)
    bt = rel_bias.T.astype(_F32)
    tok = jnp.arange(tdec, dtype=jnp.int32)[:, None]
    key = jnp.arange(MOBA_BLOCK, dtype=jnp.int32)[None, :]
    near = bt[:, _t5_bucket(MOBA_BLOCK + tok - key)]
    far = jnp.broadcast_to(bt[:, N_BUCKETS - 1][:, None, None], (B_HEADS, tdec, 128))
    new = jnp.arange(tpad, dtype=jnp.int32)[None, :]
    own = jnp.where((tok - new >= 0) & (new < tdec), bt[:, _t5_bucket(jnp.maximum(tok - new, 0))],
                    _NEG)
    flat = lambda a: a.reshape(B_HEADS * tdec, a.shape[-1])
    return flat(near), flat(far), flat(own)


def kernel(x_prompt, x_sample, cache_k, cache_v, state_conv, page_table, w_in, ln_v_g, ln_v_b, w_s,
           b_s, w_pa, w_pb, w_o, norm1_g, norm2_g, w_up, conv_w, conv_b, w_down, rel_bias, norm_f):
    nbp_, seq, _ = x_prompt.shape
    nbs, tdec, _ = x_sample.shape
    depth, n_pool, page_size = cache_k.shape[:3]
    n_pages = page_table.shape[1]
    ns = nbs * tdec
    np_ = nbp_ * seq
    assert page_size == CHUNK and n_pages % _PAGES_PER_STEP == 0 and seq % MOBA_BLOCK == 0
    assert _PAGES_PER_STEP % _PAGES_PER_BLOCK == 0
    assert CONV_W - 1 <= tdec <= CHUNK and ns % CHUNK == 0 and CHUNK % tdec == 0
    tpad = -(-tdec // 8) * 8
    tm_p = MOBA_BLOCK
    nb = seq // MOBA_BLOCK
    nb_past = n_pages // _PAGES_PER_BLOCK
    rows = B_HEADS * tdec

    xp = x_prompt.reshape(np_, D_MODEL)
    xs = x_sample.reshape(ns, D_MODEL)
    cache_kt = cache_k.transpose(0, 1, 3, 4, 2).reshape(depth * n_pool, B_WIDTH, page_size)
    cache_vt = cache_v.transpose(0, 1, 3, 4, 2).reshape(depth * n_pool, B_WIDTH, page_size)
    page_flat = page_table.reshape(-1).astype(jnp.int32)

    btab = _prompt_bias_tables(rel_bias)
    bnear_s, bfar_s, bown_s = _sample_bias_tables(rel_bias, tdec, tpad)
    row_head = jnp.arange(rows, dtype=jnp.int32)[:, None] // tdec
    hmask = (jnp.arange(B_WIDTH, dtype=jnp.int32)[None, :] // B_HEAD_DIM == row_head).astype(_F32)
    rsum = (jnp.arange(rows, dtype=jnp.int32)[None, :] % tdec
            == jnp.arange(tdec, dtype=jnp.int32)[:, None]).astype(_F32)
    lane_pad = lambda a: jnp.pad(a, ((0, 0), (0, 128 - rows)))
    bownt_s, hmaskt = lane_pad(bown_s.T), lane_pad(hmask.T)
    rsum_p = lane_pad(jnp.pad(rsum, ((0, tpad - tdec), (0, 0))))
    tpos = jnp.broadcast_to((jnp.arange(ns, dtype=jnp.int32) % tdec)[:, None], (ns, D_FF))
    tril = jnp.tril(jnp.ones((CHUNK, CHUNK), bool))
    eye_s = jnp.eye(CHUNK // tdec, dtype=_F32)
    zero_state = jnp.zeros((nbp_, CONV_W - 1, D_FF), _F32)

    ks_l, vs_l, as_l, cp_l, cs_l = [], [], [], [], []
    yp = ys = kv_nat = None
    for l in range(depth):
        last = l == depth - 1
        w_in_b = w_in[l].astype(_BF16)
        g1 = norm1_g[l][None, :]
        lng, lnb = ln_v_g[l][None, :], ln_v_b[l][None, :]
        wpa, wpb, wo = w_pa[l].astype(_BF16), w_pb[l].astype(_BF16), w_o[l].astype(_BF16)
        g2 = norm2_g[l][None, :]
        wup, wdn = w_up[l].astype(_BF16), w_down[l].astype(_BF16)
        cw, cb = conv_w[l], conv_b[l][None, :]
        gf = norm_f[None, :]
        wmix_p = jnp.where(tril[None], w_s[l], 0.0).astype(_BF16)
        bmix_p = jnp.broadcast_to(b_s[l][:, :, None], (A_GROUPS, CHUNK, CHUNK))
        w_small = jnp.where(tril[None, :tdec, :tdec], w_s[l][:, :tdec, :tdec], 0.0)
        wmix_s = jnp.einsum("ab,gts->gatbs", eye_s, w_small).reshape(A_GROUPS, CHUNK, CHUNK)
        wmix_s = wmix_s.astype(_BF16)
        bmix_s = jnp.broadcast_to(jnp.tile(b_s[l][:, :tdec], (1, CHUNK // tdec))[:, :, None],
                                  (A_GROUPS, CHUNK, CHUNK))

        (qp, kbp, vtp, kmeanp, knat, vnat, yap, sgap, sgbp) = _proj_call(
            xp, g1, w_in_b, lng, lnb, wmix_p, bmix_p, tm=tm_p, prompt=True, layer=l, depth=depth,
            seq=seq, kv_prev=kv_nat)
        kv_nat = (knat, vnat)
        ybp = _moba_prompt_call(qp.reshape(nbp_, seq, B_WIDTH), kbp.reshape(nbp_, seq, -1),
                                vtp, kmeanp.reshape(nbp_, nb, B_WIDTH), btab)
        outs = _post_call(xp, yap, ybp.reshape(np_, B_WIDTH), sgap, sgbp, wpa, wpb, wo, g2,
                          wup, cw, cb, wdn, gf, (zero_state,), tm=tm_p,
                          tiles_per_seq=seq // tm_p, per_row_state=False, final_norm=last)
        xp, cp = outs[0], outs[1]
        if last:
            yp = outs[2]

        qs, ks, vs, yas, sgas, sgbs, vas = _proj_call(xs, g1, w_in_b, lng, lnb, wmix_s, bmix_s,
                                                      tm=CHUNK, prompt=False)
        ks4 = ks.reshape(nbs, tdec, B_HEADS, B_HEAD_DIM)
        vs4 = vs.reshape(nbs, tdec, B_HEADS, B_HEAD_DIM)
        qbd = (qs.reshape(nbs, 1, tdec, B_WIDTH) * hmask.reshape(B_HEADS, tdec, B_WIDTH)[None]
               ).reshape(nbs, rows, B_WIDTH)
        qbdt = jnp.pad(qbd.transpose(0, 2, 1), ((0, 0), (0, 0), (0, 128 - rows)))
        pad = ((0, 0), (0, tpad - tdec), (0, 0))
        knew = jnp.pad(ks.reshape(nbs, tdec, B_WIDTH), pad)
        vnewt = jnp.pad(vs.reshape(nbs, tdec, B_WIDTH), pad).transpose(0, 2, 1)
        ybs = _moba_sample_call(page_flat, qbd, qbdt, knew, vnewt, bnear_s, bfar_s, bownt_s, hmaskt,
                                rsum_p, cache_kt, cache_vt, layer=l, n_pool=n_pool, n_pages=n_pages,
                                tdec=tdec)
        ybs = ybs.reshape(ns, B_WIDTH)
        st = state_conv[l]
        s1 = jnp.broadcast_to(st[:, 1:2, :], (nbs, tdec, D_FF)).reshape(ns, D_FF)
        s2 = jnp.concatenate([st, jnp.zeros((nbs, tdec - (CONV_W - 1), D_FF), _F32)],
                             axis=1).reshape(ns, D_FF)
        outs = _post_call(xs, yas, ybs.astype(_BF16), sgas, sgbs, wpa, wpb, wo,
                          g2, wup, cw, cb, wdn, gf, (s1, s2, tpos), tm=CHUNK, tiles_per_seq=1,
                          per_row_state=True, final_norm=last)
        xs, a_s = outs[0], outs[1]
        if last:
            ys = outs[2]

        ks_l.append(ks4)
        vs_l.append(vs4)
        as_l.append(vas.reshape(nbs, tdec, A_WIDTH))
        cp_l.append(cp)
        a_ext = jnp.concatenate([st, a_s.reshape(nbs, tdec, D_FF)], axis=1)
        cs_l.append(a_ext[:, -(CONV_W - 1):])

    def kv_out(t):
        return t.reshape(depth, nbp_, B_HEADS, B_HEAD_DIM, seq).transpose(0, 1, 4, 2, 3)

    return (yp.reshape(nbp_, seq, D_MODEL), ys.reshape(nbs, tdec, D_MODEL),
            kv_out(kv_nat[0]), kv_out(kv_nat[1]),
            jnp.stack(ks_l, 0), jnp.stack(vs_l, 0),
            jnp.stack(as_l, 0), jnp.stack(cp_l, 0), jnp.stack(cs_l, 0))
```

```python
import functools
import math

import jax
import jax.numpy as jnp
from jax import lax
from jax.experimental import pallas as pl
from jax.experimental.pallas import tpu as pltpu

D_MODEL = 1024
CHUNK = 128
A_WIDTH = D_MODEL // 2
A_GROUPS = 4
A_GROUP_DIM = A_WIDTH // A_GROUPS
B_HEADS = 8
B_HEAD_DIM = 64
B_WIDTH = B_HEADS * B_HEAD_DIM
MOBA_BLOCK = 256
MOBA_TOPK = 3
N_BUCKETS = 32
MAX_DISTANCE = 128
D_FF = 2816
CONV_W = 3
EPS = 1e-6
N_IN = 2 * A_WIDTH + 3 * B_WIDTH + 2 * D_MODEL

_C_U, _C_VA, _C_Q, _C_K, _C_V, _C_GA, _C_GB = 0, 512, 1024, 1536, 2048, 2560, 3584

_NEG = -1e30
_HEADS_PER_PAIR = 2
_PAIRS_PER_STEP = 2
_BLOCKS_PER_ITER = 4
_TAB_OWN, _TAB_PREV, _TAB_FAR, _TAB_DEAD, _N_TABLES = 0, 1, 2, 3, 4
_PAIR_LANES = _HEADS_PER_PAIR * B_HEAD_DIM
_K_LANES = 2 * _PAIR_LANES
_V_ROWS = B_HEAD_DIM + 16
_LOG2E = math.log2(math.e)
_PAGES_PER_STEP = 16
_PAGES_PER_BLOCK = MOBA_BLOCK // CHUNK
_VMEM_LIMIT = 48 * 1024 * 1024
_SCALE = 1.0 / math.sqrt(B_HEAD_DIM)

_F32 = jnp.float32
_BF16 = jnp.bfloat16


def _gelu(x):
    k = -2.0 * 0.7978845608028654 * _LOG2E
    return x / (1.0 + jnp.exp2(x * (k + (k * 0.044715) * (x * x))))


def _sigmoid(x):
    return 1.0 / (1.0 + jnp.exp2(x * -_LOG2E))


def _rms(x, g):
    return x * lax.rsqrt(jnp.mean(x * x, axis=-1, keepdims=True) + EPS) * g


def _const_spec(shape):
    zeros = (0,) * len(shape)
    return pl.BlockSpec(shape, lambda *_: zeros, pipeline_mode=pl.Buffered(1))


def _proj_kernel(x_ref, g1_ref, w_ref, lng_ref, lnb_ref, wmix_ref, bmix_ref, *rest, tm, prompt,
                 nb=None):
    if prompt:
        (q_ref, kb_ref, vt_ref, kmean_ref, knat_ref, vnat_ref, ya_ref, sga_ref, sgb_ref) = rest[-9:]
    else:
        (q_ref, k_ref, v_ref, ya_ref, sga_ref, sgb_ref, va_ref) = rest
    xn = _rms(x_ref[...], g1_ref[...]).astype(_BF16)

    def proj(lo, width):
        return jnp.dot(xn, w_ref[:, lo:lo + width], preferred_element_type=_F32)

    q_ref[...] = proj(_C_Q, B_WIDTH)
    k = proj(_C_K, B_WIDTH)
    v = proj(_C_V, B_WIDTH)
    if prompt:
        vt = v.T
        kmean_ref[0] = jnp.mean(k, axis=0, keepdims=True)
        knat_ref[0] = k.T
        vnat_ref[0] = vt
        kb = k.astype(_BF16)
        blk_id = pl.program_id(0) % nb
        lane = lax.broadcasted_iota(jnp.int32, (tm, _K_LANES - _PAIR_LANES), 1)
        onehot = jnp.where(lane == blk_id, 1.0, 0.0).astype(_BF16)
        for hp in range(B_HEADS // _HEADS_PER_PAIR):
            kb_ref[:, hp * _K_LANES:hp * _K_LANES + _PAIR_LANES] = (
                kb[:, hp * _PAIR_LANES:(hp + 1) * _PAIR_LANES])
            kb_ref[:, hp * _K_LANES + _PAIR_LANES:(hp + 1) * _K_LANES] = onehot
        vtb = vt.astype(_BF16)
        row = lax.broadcasted_iota(jnp.int32, (_V_ROWS - B_HEAD_DIM, tm), 0)
        ones_row = jnp.where(row == 0, 1.0, 0.0).astype(_BF16)
        for h in range(B_HEADS):
            vt_ref[0, 0, h * _V_ROWS:h * _V_ROWS + B_HEAD_DIM, :] = (
                vtb[h * B_HEAD_DIM:(h + 1) * B_HEAD_DIM, :])
            vt_ref[0, 0, h * _V_ROWS + B_HEAD_DIM:(h + 1) * _V_ROWS, :] = ones_row
    else:
        k_ref[...] = k
        v_ref[...] = v
    sga_ref[...] = _sigmoid(proj(_C_GA, D_MODEL)).astype(_BF16)
    sgb_ref[...] = _sigmoid(proj(_C_GB, D_MODEL)).astype(_BF16)

    u = _gelu(proj(_C_U, A_WIDTH))
    va = _gelu(proj(_C_VA, A_WIDTH))
    mu = jnp.mean(va, axis=-1, keepdims=True)
    vc = va - mu
    var = jnp.mean(vc * vc, axis=-1, keepdims=True)
    va = vc * lax.rsqrt(var + EPS) * lng_ref[...] + lnb_ref[...]
    if not prompt:
        va_ref[...] = va
    for c in range(tm // CHUNK):
        rows = slice(c * CHUNK, (c + 1) * CHUNK)
        for g in range(A_GROUPS):
            cols = slice(g * A_GROUP_DIM, (g + 1) * A_GROUP_DIM)
            mixed = jnp.dot(wmix_ref[g], va[rows, cols].astype(_BF16),
                            preferred_element_type=_F32) + bmix_ref[g]
            ya_ref[rows, cols] = (u[rows, cols] * mixed).astype(_BF16)


def _proj_call(x2, g1, w_in, lng, lnb, wmix, bmix, *, tm, prompt, layer=0, depth=1, seq=None,
               kv_prev=None):
    n = x2.shape[0]
    nt = n // tm
    row = lambda width: pl.BlockSpec((tm, width), lambda i: (i, 0))
    in_specs = [row(D_MODEL), _const_spec((1, D_MODEL)), _const_spec((D_MODEL, N_IN)),
                _const_spec((1, A_WIDTH)), _const_spec((1, A_WIDTH)),
                _const_spec((A_GROUPS, CHUNK, CHUNK)), _const_spec((A_GROUPS, CHUNK, CHUNK))]
    args = [x2, g1, w_in, lng, lnb, wmix, bmix]
    aliases = {}
    tail_shape = [jax.ShapeDtypeStruct((n, A_WIDTH), _BF16),
                  jax.ShapeDtypeStruct((n, D_MODEL), _BF16),
                  jax.ShapeDtypeStruct((n, D_MODEL), _BF16)]
    tail_specs = [row(A_WIDTH), row(D_MODEL), row(D_MODEL)]
    nb = None
    if prompt:
        assert tm == MOBA_BLOCK
        nb = seq // MOBA_BLOCK
        assert nb <= _K_LANES - _PAIR_LANES
        nseq = n // seq
        k_cols = B_HEADS // _HEADS_PER_PAIR * _K_LANES
        nat = jax.ShapeDtypeStruct((depth * nseq, B_WIDTH, seq), _F32)
        nat_spec = pl.BlockSpec((1, B_WIDTH, tm), lambda i: (layer * nseq + i // nb, 0, i % nb))
        out_shape = [jax.ShapeDtypeStruct((n, B_WIDTH), _F32),
                     jax.ShapeDtypeStruct((n, k_cols), _BF16),
                     jax.ShapeDtypeStruct((nseq, nb, B_HEADS * _V_ROWS, MOBA_BLOCK), _BF16),
                     jax.ShapeDtypeStruct((nt, 1, B_WIDTH), _F32), nat, nat] + tail_shape
        out_specs = [row(B_WIDTH), row(k_cols),
                     pl.BlockSpec((1, 1, B_HEADS * _V_ROWS, MOBA_BLOCK),
                                  lambda i: (i // nb, i % nb, 0, 0)),
                     pl.BlockSpec((1, 1, B_WIDTH), lambda i: (i, 0, 0)), nat_spec, nat_spec
                     ] + tail_specs
        if kv_prev is not None:
            in_specs += [pl.BlockSpec(memory_space=pl.ANY)] * 2
            args += list(kv_prev)
            aliases = {7: 4, 8: 5}
    else:
        out_shape = [jax.ShapeDtypeStruct((n, B_WIDTH), _F32)] * 3 + tail_shape + [
            jax.ShapeDtypeStruct((n, A_WIDTH), _F32)]
        out_specs = [row(B_WIDTH)] * 3 + tail_specs + [row(A_WIDTH)]
    return pl.pallas_call(
        functools.partial(_proj_kernel, tm=tm, prompt=prompt, nb=nb),
        grid=(nt,),
        in_specs=in_specs,
        out_specs=out_specs,
        out_shape=out_shape,
        input_output_aliases=aliases,
        compiler_params=pltpu.CompilerParams(dimension_semantics=("arbitrary",),
                                             vmem_limit_bytes=_VMEM_LIMIT),
        name="proj_mixer_a",
    )(*args)


def _topk_block_mask(gate, own, nb, width):
    blk = lax.broadcasted_iota(jnp.int32, (nb, width), 0)
    gm = jnp.where(blk < own, gate, -jnp.inf)
    rank = jnp.zeros((nb, width), _F32)
    for jp in range(nb):
        r = gm[jp:jp + 1, :]
        beats = jnp.where(r > gm, 1.0, jnp.where(r == gm, jnp.where(blk > jp, 1.0, 0.0), 0.0))
        rank = rank + beats
    past_sel = jnp.where(blk < own, jnp.where(rank < MOBA_TOPK, 1.0, 0.0), 0.0)
    return jnp.where(blk == own, 1.0, past_sel)


def _moba_prompt_kernel(q_ref, k_ref, vt_ref, kmean_ref, btab_ref, o_ref, s_scr, *, nb):
    i = pl.program_id(2)
    blk = MOBA_BLOCK
    heads = range(_PAIRS_PER_STEP * _HEADS_PER_PAIR)

    qt = q_ref[0].T
    feat = lax.broadcasted_iota(jnp.int32, (_PAIR_LANES, blk), 0)
    qz = []
    for hh in heads:
        pair, sub = divmod(hh, _HEADS_PER_PAIR)
        in_head = (feat >= sub * B_HEAD_DIM) & (feat < (sub + 1) * B_HEAD_DIM)
        qh = jnp.where(in_head, qt[pair * _PAIR_LANES:(pair + 1) * _PAIR_LANES], 0.0)
        gate = jnp.dot(kmean_ref[0, :, pair * _PAIR_LANES:(pair + 1) * _PAIR_LANES], qh,
                       preferred_element_type=_F32, precision=lax.Precision.HIGHEST)
        mask_rows = jnp.where(_topk_block_mask(gate, i, nb, blk) > 0.5, 0.0, _NEG)
        pad = jnp.zeros((_K_LANES - _PAIR_LANES - nb, blk), _F32)
        qz.append(jnp.concatenate([qh * (_SCALE * _LOG2E), mask_rows, pad], axis=0).astype(_BF16))

    n_iter = lax.div(i + _BLOCKS_PER_ITER, _BLOCKS_PER_ITER)

    def slots(it):
        js = [it * _BLOCKS_PER_ITER + u for u in range(_BLOCKS_PER_ITER)]
        tabs = [jnp.where(j == i, _TAB_OWN, jnp.where(j == i - 1, _TAB_PREV,
                                                      jnp.where(j > i, _TAB_DEAD, _TAB_FAR)))
                for j in js]
        return js, [jnp.minimum(j, i) for j in js], tabs

    def logits_pass(it, ms):
        js, jcs, tabs = slots(it)
        out = []
        for hh in heads:
            pair = hh // _HEADS_PER_PAIR
            m = ms[hh]
            for j, jc, tab in zip(js, jcs, tabs):
                kj = k_ref[0, pl.ds(pl.multiple_of(jc * blk, blk), blk),
                           pair * _K_LANES:(pair + 1) * _K_LANES]
                s = jnp.dot(kj, qz[hh], preferred_element_type=_F32) + btab_ref[hh, tab]
                s_scr[hh, j] = s
                m = jnp.maximum(m, jnp.max(s, axis=0, keepdims=True))
            out.append(m)
        return tuple(out)

    ms = lax.fori_loop(0, n_iter, logits_pass,
                       tuple(jnp.full((1, blk), _NEG, _F32) for _ in heads))

    def value_pass(it, accs):
        js, jcs, _ = slots(it)
        out = []
        for hh in heads:
            acc = accs[hh]
            for j, jc in zip(js, jcs):
                p = jnp.exp2(s_scr[hh, j] - ms[hh]).astype(_BF16)
                acc = acc + jnp.dot(vt_ref[0, jc, hh * _V_ROWS:(hh + 1) * _V_ROWS, :], p,
                                    preferred_element_type=_F32)
            out.append(acc)
        return tuple(out)

    accs = lax.fori_loop(0, n_iter, value_pass,
                         tuple(jnp.zeros((_V_ROWS, blk), _F32) for _ in heads))
    outs = [acc[:B_HEAD_DIM] / acc[B_HEAD_DIM:B_HEAD_DIM + 1] for acc in accs]
    o_ref[0] = jnp.concatenate(outs, axis=0).T.astype(_BF16)


def _moba_prompt_call(q3, kb3, vt4, kmean3, btab):
    b, t, _ = q3.shape
    nb = t // MOBA_BLOCK
    g = _PAIRS_PER_STEP
    heads = g * _HEADS_PER_PAIR
    return pl.pallas_call(
        functools.partial(_moba_prompt_kernel, nb=nb),
        grid=(b, B_HEADS // heads, nb),
        in_specs=[pl.BlockSpec((1, MOBA_BLOCK, g * _PAIR_LANES), lambda bb, hp, i: (bb, i, hp)),
                  pl.BlockSpec((1, t, g * _K_LANES), lambda bb, hp, i: (bb, 0, hp)),
                  pl.BlockSpec((1, nb, heads * _V_ROWS, MOBA_BLOCK),
                               lambda bb, hp, i: (bb, 0, hp, 0)),
                  pl.BlockSpec((1, nb, g * _PAIR_LANES), lambda bb, hp, i: (bb, 0, hp)),
                  pl.BlockSpec((heads, _N_TABLES, MOBA_BLOCK, MOBA_BLOCK),
                               lambda bb, hp, i: (hp, 0, 0, 0), pipeline_mode=pl.Buffered(1))],
        out_specs=pl.BlockSpec((1, MOBA_BLOCK, g * _PAIR_LANES), lambda bb, hp, i: (bb, i, hp)),
        out_shape=jax.ShapeDtypeStruct((b, t, B_WIDTH), _BF16),
        scratch_shapes=[pltpu.VMEM((heads, nb + _BLOCKS_PER_ITER - 1, MOBA_BLOCK, MOBA_BLOCK), _F32)],
        compiler_params=pltpu.CompilerParams(
            dimension_semantics=("arbitrary", "arbitrary", "arbitrary"),
            vmem_limit_bytes=_VMEM_LIMIT),
        name="moba_prompt",
    )(q3, kb3, vt4, kmean3, btab)


def _moba_sample_kernel(pt_ref, qbd_ref, qbdt_ref, knew_ref, vnewt_ref, bnear_ref, bfar_ref,
                        bownt_ref, hmaskt_ref, rsum_ref, *rest, nb_past):
    del pt_ref
    pages = _PAGES_PER_STEP
    k_pages = rest[:pages]
    v_pages = rest[pages:2 * pages]
    o_ref = rest[2 * pages]
    gate_scr, m_scr, l_scr, acc_scr, st_scr = rest[2 * pages + 1:]
    c = pl.program_id(1)
    rows = qbd_ref.shape[1]
    tdec = o_ref.shape[1]
    blocks_per_step = pages // _PAGES_PER_BLOCK

    qz = (qbd_ref[0] * _SCALE).astype(_BF16)
    qbdt = qbdt_ref[0]
    row_pad = jnp.zeros((128 - rows, MOBA_BLOCK), _F32)

    for bl in range(blocks_per_step):
        jg = c * blocks_per_step + bl
        kts = [k_pages[bl * _PAGES_PER_BLOCK + p][0] for p in range(_PAGES_PER_BLOCK)]
        kt = jnp.concatenate(kts, axis=1)
        kmean = jnp.sum(functools.reduce(lambda a, b: a + b, kts), axis=-1,
                        keepdims=True) * (1.0 / MOBA_BLOCK)
        gate_scr[pl.ds(jg, 1), :] = jnp.sum(qbdt * kmean, axis=0, keepdims=True)
        s = jnp.dot(qz, kt.astype(_BF16), preferred_element_type=_F32)
        s = s + jnp.where(jg == nb_past - 1, bnear_ref[...], bfar_ref[:, 0:1])
        st_scr[bl] = jnp.concatenate([s, row_pad], axis=0).T
    for bl in range(blocks_per_step):
        jg = c * blocks_per_step + bl
        vt = jnp.concatenate([v_pages[bl * _PAGES_PER_BLOCK + p][0]
                              for p in range(_PAGES_PER_BLOCK)], axis=1)
        st = st_scr[bl]
        m = jnp.max(st, axis=0, keepdims=True)
        p = jnp.exp(st - m)
        m_scr[pl.ds(jg, 1), :] = m
        l_scr[pl.ds(jg, 1), :] = jnp.sum(p, axis=0, keepdims=True)
        acc_scr[jg] = jnp.dot(vt.astype(_BF16), p.astype(_BF16), preferred_element_type=_F32)

    @pl.when(c == pl.num_programs(1) - 1)
    def _():
        sel = _topk_block_mask(gate_scr[...], nb_past, nb_past, 128) > 0.5
        m_blk = m_scr[...]
        s_own = jnp.dot(knew_ref[0], qbdt * _SCALE, preferred_element_type=_F32) + bownt_ref[...]
        m_tot = jnp.maximum(jnp.max(s_own, axis=0, keepdims=True),
                            jnp.max(jnp.where(sel, m_blk, _NEG), axis=0, keepdims=True))
        w = jnp.where(sel, jnp.exp(m_blk - m_tot), 0.0)
        p_own = jnp.exp(s_own - m_tot)
        l_tot = jnp.sum(p_own, axis=0, keepdims=True) + jnp.sum(w * l_scr[...], axis=0, keepdims=True)
        accs = [jnp.dot(vnewt_ref[0], p_own, preferred_element_type=_F32)]
        accs += [jnp.zeros_like(accs[0]) for _ in range(3)]
        for j in range(nb_past):
            accs[j % 4] = accs[j % 4] + w[j:j + 1, :] * acc_scr[j]
        out_t = ((accs[0] + accs[1]) + (accs[2] + accs[3])) / l_tot
        folded = lax.dot_general(rsum_ref[...], out_t * hmaskt_ref[...], (((1,), (1,)), ((), ())),
                                 preferred_element_type=_F32, precision=lax.Precision.HIGHEST)
        o_ref[0] = folded[:tdec]


def _moba_sample_call(page_flat, qbd, qbdt, knew, vnewt, bnear, bfar, bownt, hmaskt, rsum,
                      cache_kt, cache_vt, *, layer, n_pool, n_pages, tdec):
    nbatch, rows, _ = qbd.shape
    pages = _PAGES_PER_STEP
    nb_past = n_pages // _PAGES_PER_BLOCK
    assert nb_past % 8 == 0 and rows <= 128
    base = layer * n_pool

    def page_spec(p):
        return pl.BlockSpec((1, B_WIDTH, CHUNK),
                            lambda bb, cc, pt, p=p: (base + pt[bb * n_pages + cc * pages + p], 0, 0))

    per_batch = lambda arr: pl.BlockSpec((1,) + arr.shape[1:], lambda bb, cc, pt: (bb, 0, 0))
    const = lambda arr: pl.BlockSpec(arr.shape, lambda bb, cc, pt: (0,) * arr.ndim)
    grid_spec = pltpu.PrefetchScalarGridSpec(
        num_scalar_prefetch=1,
        grid=(nbatch, n_pages // pages),
        in_specs=[per_batch(qbd), per_batch(qbdt), per_batch(knew), per_batch(vnewt),
                  const(bnear), const(bfar), const(bownt), const(hmaskt), const(rsum)]
                 + [page_spec(p) for p in range(pages)] * 2,
        out_specs=pl.BlockSpec((1, tdec, B_WIDTH), lambda bb, cc, pt: (bb, 0, 0)),
        scratch_shapes=[pltpu.VMEM((nb_past, 128), _F32),
                        pltpu.VMEM((nb_past, 128), _F32),
                        pltpu.VMEM((nb_past, 128), _F32),
                        pltpu.VMEM((nb_past, B_WIDTH, 128), _F32),
                        pltpu.VMEM((pages // _PAGES_PER_BLOCK, MOBA_BLOCK, 128), _F32)])
    return pl.pallas_call(
        functools.partial(_moba_sample_kernel, nb_past=nb_past),
        grid_spec=grid_spec,
        out_shape=jax.ShapeDtypeStruct((nbatch, tdec, B_WIDTH), _F32),
        compiler_params=pltpu.CompilerParams(dimension_semantics=("arbitrary", "arbitrary"),
                                             vmem_limit_bytes=_VMEM_LIMIT),
        name="moba_sample",
    )(page_flat, qbd, qbdt, knew, vnewt, bnear, bfar, bownt, hmaskt, rsum,
      *([cache_kt] * pages), *([cache_vt] * pages))


def _post_kernel(x_ref, ya_ref, yb_ref, sga_ref, sgb_ref, wpa_ref, wpb_ref, wo_ref, g2_ref,
                 wup_ref, cw_ref, cb_ref, wdn_ref, gf_ref, *rest, tm, tiles_per_seq, per_row_state,
                 final_norm):
    if per_row_state:
        s1_ref, s2_ref, tpos_ref = rest[:3]
        rest = rest[3:]
    else:
        st_ref = rest[0]
        rest = rest[1:]
    xo_ref, a_ref = rest[:2]
    rest = rest[2:]
    if final_norm:
        y_ref = rest[0]
        rest = rest[1:]
    if not per_row_state:
        carry_scr = rest[0]

    ma = jnp.dot(ya_ref[...], wpa_ref[...], preferred_element_type=_F32)
    mb = jnp.dot(yb_ref[...], wpb_ref[...], preferred_element_type=_F32)
    merged = sga_ref[...].astype(_F32) * ma + sgb_ref[...].astype(_F32) * mb
    x1 = x_ref[...] + jnp.dot(merged.astype(_BF16), wo_ref[...], preferred_element_type=_F32)

    xn = _rms(x1, g2_ref[...]).astype(_BF16)
    a = jnp.dot(xn, wup_ref[:, :D_FF], preferred_element_type=_F32)
    gate = jnp.dot(xn, wup_ref[:, D_FF:], preferred_element_type=_F32)

    prev1 = pltpu.roll(a, 1, 0)
    prev2 = pltpu.roll(a, 2, 0)
    if per_row_state:
        tpos = tpos_ref[...]
        prev1 = jnp.where(tpos >= 1, prev1, s1_ref[...])
        prev2 = jnp.where(tpos >= 2, prev2, s2_ref[...])
        a_ref[...] = a
    else:
        i = pl.program_id(0)

        @pl.when(i % tiles_per_seq == 0)
        def _():
            carry_scr[...] = st_ref[0]

        row = lax.broadcasted_iota(jnp.int32, (8, D_FF), 0)
        c0 = carry_scr[0:1, :]
        c1 = carry_scr[1:2, :]
        top1 = jnp.where(row == 0, c1, prev1[:8])
        top2 = jnp.where(row == 0, c0, jnp.where(row == 1, c1, prev2[:8]))
        prev1 = jnp.concatenate([top1, prev1[8:]], axis=0)
        prev2 = jnp.concatenate([top2, prev2[8:]], axis=0)
        tail = a[tm - (CONV_W - 1):, :]
        carry_scr[...] = tail
        a_ref[0] = tail
    conv = cb_ref[...] + prev2 * cw_ref[0:1, :] + prev1 * cw_ref[1:2, :] + a * cw_ref[2:3, :]
    act = (_gelu(conv) * gate).astype(_BF16)
    x2 = x1 + jnp.dot(act, wdn_ref[...], preferred_element_type=_F32)
    xo_ref[...] = x2
    if final_norm:
        y_ref[...] = _rms(x2, gf_ref[...])


def _post_call(x2, ya, yb, sga, sgb, wpa, wpb, wo, g2, wup, cw, cb, wdn, gf, state_args, *,
               tm, tiles_per_seq, per_row_state, final_norm):
    n = x2.shape[0]
    row = lambda width: pl.BlockSpec((tm, width), lambda i: (i, 0))
    in_specs = [row(D_MODEL), row(A_WIDTH), row(B_WIDTH), row(D_MODEL), row(D_MODEL),
                _const_spec((A_WIDTH, D_MODEL)), _const_spec((B_WIDTH, D_MODEL)),
                _const_spec((D_MODEL, D_MODEL)), _const_spec((1, D_MODEL)),
                _const_spec((D_MODEL, 2 * D_FF)), _const_spec((CONV_W, D_FF)),
                _const_spec((1, D_FF)), _const_spec((D_FF, D_MODEL)), _const_spec((1, D_MODEL))]
    out_shape = [jax.ShapeDtypeStruct((n, D_MODEL), _F32)]
    out_specs = [row(D_MODEL)]
    scratch = []
    if per_row_state:
        in_specs += [row(D_FF), row(D_FF), row(D_FF)]
        out_shape.append(jax.ShapeDtypeStruct((n, D_FF), _F32))
        out_specs.append(row(D_FF))
    else:
        nseq = n // (tm * tiles_per_seq)
        in_specs.append(pl.BlockSpec((1, CONV_W - 1, D_FF), lambda i: (i // tiles_per_seq, 0, 0)))
        out_shape.append(jax.ShapeDtypeStruct((nseq, CONV_W - 1, D_FF), _F32))
        out_specs.append(pl.BlockSpec((1, CONV_W - 1, D_FF), lambda i: (i // tiles_per_seq, 0, 0)))
        scratch.append(pltpu.VMEM((CONV_W - 1, D_FF), _F32))
    if final_norm:
        out_shape.append(jax.ShapeDtypeStruct((n, D_MODEL), _F32))
        out_specs.append(row(D_MODEL))
    return pl.pallas_call(
        functools.partial(_post_kernel, tm=tm, tiles_per_seq=tiles_per_seq,
                          per_row_state=per_row_state, final_norm=final_norm),
        grid=(n // tm,),
        in_specs=in_specs,
        out_specs=out_specs,
        out_shape=out_shape,
        scratch_shapes=scratch,
        compiler_params=pltpu.CompilerParams(dimension_semantics=("arbitrary",),
                                             vmem_limit_bytes=_VMEM_LIMIT),
        name="merge_convffn",
    )(x2, ya, yb, sga, sgb, wpa, wpb, wo, g2, wup, cw, cb, wdn, gf, *state_args)


def _t5_bucket(rel):
    max_exact = N_BUCKETS // 2
    relf = jnp.maximum(rel, 1).astype(_F32)
    large = max_exact + (jnp.log(relf / max_exact) / math.log(MAX_DISTANCE / max_exact)
                         * (N_BUCKETS - max_exact)).astype(jnp.int32)
    large = jnp.minimum(large, N_BUCKETS - 1)
    return jnp.where(rel < max_exact, rel, large)


def _far_bucket_is_last(min_rel):
    max_exact = N_BUCKETS // 2
    v = max_exact + int(math.log(min_rel / max_exact) / math.log(MAX_DISTANCE / max_exact)
                        * (N_BUCKETS - max_exact) - 1e-6)
    return v >= N_BUCKETS - 1


def _toeplitz(diag, n):
    width = 2 * n
    flat = jnp.tile(diag, (1,) * (diag.ndim - 1) + (n,))
    skew = flat[..., :n * (width - 1)].reshape(diag.shape[:-1] + (n, width - 1))
    return skew[..., n - 1:2 * n - 1]


def _prompt_bias_tables(rel_bias):
    assert _far_bucket_is_last(MOBA_BLOCK + 1)
    bt = rel_bias.T.astype(_F32)
    bt = (bt - bt[:, N_BUCKETS - 1:]) * _LOG2E
    rel = jnp.arange(2 * MOBA_BLOCK, dtype=jnp.int32) - (MOBA_BLOCK - 1)
    own = jnp.where(rel >= 0, bt[:, _t5_bucket(jnp.maximum(rel, 0))], _NEG)
    prev = bt[:, _t5_bucket(rel + MOBA_BLOCK)]
    own, prev = _toeplitz(own, MOBA_BLOCK), _toeplitz(prev, MOBA_BLOCK)
    tables = [None] * _N_TABLES
    tables[_TAB_OWN], tables[_TAB_PREV] = own, prev
    tables[_TAB_FAR], tables[_TAB_DEAD] = jnp.zeros_like(own), jnp.full_like(own, _NEG)
    return jnp.stack(tables, axis=1)


def _sample_bias_tables(rel_bias, tdec, tpad):
    assert _far_bucket_is_last(MOBA_BLOCK + 1)
    bt = rel_bias.T.astype(_F32)
    tok = jnp.arange(tdec, dtype=jnp.int32)[:, None]
    key = jnp.arange(MOBA_BLOCK, dtype=jnp.int32)[None, :]
    near = bt[:, _t5_bucket(MOBA_BLOCK + tok - key)]
    far = jnp.broadcast_to(bt[:, N_BUCKETS - 1][:, None, None], (B_HEADS, tdec, 128))
    new = jnp.arange(tpad, dtype=jnp.int32)[None, :]
    own = jnp.where((tok - new >= 0) & (new < tdec), bt[:, _t5_bucket(jnp.maximum(tok - new, 0))],
                    _NEG)
    flat = lambda a: a.reshape(B_HEADS * tdec, a.shape[-1])
    return flat(near), flat(far), flat(own)


def kernel(x_prompt, x_sample, cache_k, cache_v, state_conv, page_table, w_in, ln_v_g, ln_v_b, w_s,
           b_s, w_pa, w_pb, w_o, norm1_g, norm2_g, w_up, conv_w, conv_b, w_down, rel_bias, norm_f):
    nbp_, seq, _ = x_prompt.shape
    nbs, tdec, _ = x_sample.shape
    depth, n_pool, page_size = cache_k.shape[:3]
    n_pages = page_table.shape[1]
    ns = nbs * tdec
    np_ = nbp_ * seq
    assert page_size == CHUNK and n_pages % _PAGES_PER_STEP == 0 and seq % MOBA_BLOCK == 0
    assert _PAGES_PER_STEP % _PAGES_PER_BLOCK == 0
    assert CONV_W - 1 <= tdec <= CHUNK and ns % CHUNK == 0 and CHUNK % tdec == 0
    tpad = -(-tdec // 8) * 8
    tm_p = MOBA_BLOCK
    nb = seq // MOBA_BLOCK
    nb_past = n_pages // _PAGES_PER_BLOCK
    rows = B_HEADS * tdec

    xp = x_prompt.reshape(np_, D_MODEL)
    xs = x_sample.reshape(ns, D_MODEL)
    cache_kt = cache_k.transpose(0, 1, 3, 4, 2).reshape(depth * n_pool, B_WIDTH, page_size)
    cache_vt = cache_v.transpose(0, 1, 3, 4, 2).reshape(depth * n_pool, B_WIDTH, page_size)
    page_flat = page_table.reshape(-1).astype(jnp.int32)

    btab = _prompt_bias_tables(rel_bias)
    bnear_s, bfar_s, bown_s = _sample_bias_tables(rel_bias, tdec, tpad)
    row_head = jnp.arange(rows, dtype=jnp.int32)[:, None] // tdec
    hmask = (jnp.arange(B_WIDTH, dtype=jnp.int32)[None, :] // B_HEAD_DIM == row_head).astype(_F32)
    rsum = (jnp.arange(rows, dtype=jnp.int32)[None, :] % tdec
            == jnp.arange(tdec, dtype=jnp.int32)[:, None]).astype(_F32)
    lane_pad = lambda a: jnp.pad(a, ((0, 0), (0, 128 - rows)))
    bownt_s, hmaskt = lane_pad(bown_s.T), lane_pad(hmask.T)
    rsum_p = lane_pad(jnp.pad(rsum, ((0, tpad - tdec), (0, 0))))
    tpos = jnp.broadcast_to((jnp.arange(ns, dtype=jnp.int32) % tdec)[:, None], (ns, D_FF))
    tril = jnp.tril(jnp.ones((CHUNK, CHUNK), bool))
    eye_s = jnp.eye(CHUNK // tdec, dtype=_F32)
    zero_state = jnp.zeros((nbp_, CONV_W - 1, D_FF), _F32)

    ks_l, vs_l, as_l, cp_l, cs_l = [], [], [], [], []
    yp = ys = kv_nat = None
    for l in range(depth):
        last = l == depth - 1
        w_in_b = w_in[l].astype(_BF16)
        g1 = norm1_g[l][None, :]
        lng, lnb = ln_v_g[l][None, :], ln_v_b[l][None, :]
        wpa, wpb, wo = w_pa[l].astype(_BF16), w_pb[l].astype(_BF16), w_o[l].astype(_BF16)
        g2 = norm2_g[l][None, :]
        wup, wdn = w_up[l].astype(_BF16), w_down[l].astype(_BF16)
        cw, cb = conv_w[l], conv_b[l][None, :]
        gf = norm_f[None, :]
        wmix_p = jnp.where(tril[None], w_s[l], 0.0).astype(_BF16)
        bmix_p = jnp.broadcast_to(b_s[l][:, :, None], (A_GROUPS, CHUNK, CHUNK))
        w_small = jnp.where(tril[None, :tdec, :tdec], w_s[l][:, :tdec, :tdec], 0.0)
        wmix_s = jnp.einsum("ab,gts->gatbs", eye_s, w_small).reshape(A_GROUPS, CHUNK, CHUNK)
        wmix_s = wmix_s.astype(_BF16)
        bmix_s = jnp.broadcast_to(jnp.tile(b_s[l][:, :tdec], (1, CHUNK // tdec))[:, :, None],
                                  (A_GROUPS, CHUNK, CHUNK))

        (qp, kbp, vtp, kmeanp, knat, vnat, yap, sgap, sgbp) = _proj_call(
            xp, g1, w_in_b, lng, lnb, wmix_p, bmix_p, tm=tm_p, prompt=True, layer=l, depth=depth,
            seq=seq, kv_prev=kv_nat)
        kv_nat = (knat, vnat)
        ybp = _moba_prompt_call(qp.reshape(nbp_, seq, B_WIDTH), kbp.reshape(nbp_, seq, -1),
                                vtp, kmeanp.reshape(nbp_, nb, B_WIDTH), btab)
        outs = _post_call(xp, yap, ybp.reshape(np_, B_WIDTH), sgap, sgbp, wpa, wpb, wo, g2,
                          wup, cw, cb, wdn, gf, (zero_state,), tm=tm_p,
                          tiles_per_seq=seq // tm_p, per_row_state=False, final_norm=last)
        xp, cp = outs[0], outs[1]
        if last:
            yp = outs[2]

        qs, ks, vs, yas, sgas, sgbs, vas = _proj_call(xs, g1, w_in_b, lng, lnb, wmix_s, bmix_s,
                                                      tm=CHUNK, prompt=False)
        ks4 = ks.reshape(nbs, tdec, B_HEADS, B_HEAD_DIM)
        vs4 = vs.reshape(nbs, tdec, B_HEADS, B_HEAD_DIM)
        qbd = (qs.reshape(nbs, 1, tdec, B_WIDTH) * hmask.reshape(B_HEADS, tdec, B_WIDTH)[None]
               ).reshape(nbs, rows, B_WIDTH)
        qbdt = jnp.pad(qbd.transpose(0, 2, 1), ((0, 0), (0, 0), (0, 128 - rows)))
        pad = ((0, 0), (0, tpad - tdec), (0, 0))
        knew = jnp.pad(ks.reshape(nbs, tdec, B_WIDTH), pad)
        vnewt = jnp.pad(vs.reshape(nbs, tdec, B_WIDTH), pad).transpose(0, 2, 1)
        ybs = _moba_sample_call(page_flat, qbd, qbdt, knew, vnewt, bnear_s, bfar_s, bownt_s, hmaskt,
                                rsum_p, cache_kt, cache_vt, layer=l, n_pool=n_pool, n_pages=n_pages,
                                tdec=tdec)
        ybs = ybs.reshape(ns, B_WIDTH)
        st = state_conv[l]
        s1 = jnp.broadcast_to(st[:, 1:2, :], (nbs, tdec, D_FF)).reshape(ns, D_FF)
        s2 = jnp.concatenate([st, jnp.zeros((nbs, tdec - (CONV_W - 1), D_FF), _F32)],
                             axis=1).reshape(ns, D_FF)
        outs = _post_call(xs, yas, ybs.astype(_BF16), sgas, sgbs, wpa, wpb, wo,
                          g2, wup, cw, cb, wdn, gf, (s1, s2, tpos), tm=CHUNK, tiles_per_seq=1,
                          per_row_state=True, final_norm=last)
        xs, a_s = outs[0], outs[1]
        if last:
            ys = outs[2]

        ks_l.append(ks4)
        vs_l.append(vs4)
        as_l.append(vas.reshape(nbs, tdec, A_WIDTH))
        cp_l.append(cp)
        a_ext = jnp.concatenate([st, a_s.reshape(nbs, tdec, D_FF)], axis=1)
        cs_l.append(a_ext[:, -(CONV_W - 1):])

    def kv_out(t):
        return t.reshape(depth, nbp_, B_HEADS, B_HEAD_DIM, seq).transpose(0, 1, 4, 2, 3)

    return (yp.reshape(nbp_, seq, D_MODEL), ys.reshape(nbs, tdec, D_MODEL),
            kv_out(kv_nat[0]), kv_out(kv_nat[1]),
            jnp.stack(ks_l, 0), jnp.stack(vs_l, 0),
            jnp.stack(as_l, 0), jnp.stack(cp_l, 0), jnp.stack(cs_l, 0))
```

```python
import functools
import math

import jax
import jax.numpy as jnp
from jax import lax
from jax.experimental import pallas as pl
from jax.experimental.pallas import tpu as pltpu

D_MODEL = 1024
CHUNK = 128
A_WIDTH = D_MODEL // 2
A_GROUPS = 4
A_GROUP_DIM = A_WIDTH // A_GROUPS
B_HEADS = 8
B_HEAD_DIM = 64
B_WIDTH = B_HEADS * B_HEAD_DIM
MOBA_BLOCK = 256
MOBA_TOPK = 3
N_BUCKETS = 32
MAX_DISTANCE = 128
D_FF = 2816
CONV_W = 3
EPS = 1e-6
N_IN = 2 * A_WIDTH + 3 * B_WIDTH + 2 * D_MODEL

_C_U, _C_VA, _C_Q, _C_K, _C_V, _C_GA, _C_GB = 0, 512, 1024, 1536, 2048, 2560, 3584

_NEG = -1e30
_HEADS_PER_PAIR = 2
_PAIRS_PER_STEP = 2
_BLOCKS_PER_ITER = 4
_TAB_OWN, _TAB_PREV, _TAB_FAR, _TAB_DEAD, _N_TABLES = 0, 1, 2, 3, 4
_PAIR_LANES = _HEADS_PER_PAIR * B_HEAD_DIM
_K_LANES = 2 * _PAIR_LANES
_V_ROWS = B_HEAD_DIM + 16
_LOG2E = math.log2(math.e)
_PAGES_PER_STEP = 16
_PAGES_PER_BLOCK = MOBA_BLOCK // CHUNK
_PROJ_TILE = 512
_POST_TILE = 512
_FFN_SPLITS = (0, 1536, D_FF)
_VMEM_LIMIT = 48 * 1024 * 1024
_SCALE = 1.0 / math.sqrt(B_HEAD_DIM)

_F32 = jnp.float32
_BF16 = jnp.bfloat16


def _gelu(x):
    k = -2.0 * 0.7978845608028654 * _LOG2E
    return x / (1.0 + jnp.exp2(x * (k + (k * 0.044715) * (x * x))))


def _sigmoid(x):
    return 1.0 / (1.0 + jnp.exp2(x * -_LOG2E))


def _rms(x, g):
    return x * lax.rsqrt(jnp.mean(x * x, axis=-1, keepdims=True) + EPS) * g


def _const_spec(shape):
    zeros = (0,) * len(shape)
    return pl.BlockSpec(shape, lambda *_: zeros, pipeline_mode=pl.Buffered(1))


def _proj_kernel(x_ref, g1_ref, w_ref, lng_ref, lnb_ref, wmix_ref, bmix_ref, *rest, tm, prompt,
                 nb=None):
    if prompt:
        (q_ref, kb_ref, vt_ref, kmean_ref, knat_ref, vnat_ref, ya_ref, sga_ref, sgb_ref) = rest[-9:]
    else:
        (q_ref, k_ref, v_ref, ya_ref, sga_ref, sgb_ref, va_ref) = rest
    xn = _rms(x_ref[...], g1_ref[...]).astype(_BF16)

    def proj(lo, width):
        return jnp.dot(xn, w_ref[:, lo:lo + width], preferred_element_type=_F32)

    u = _gelu(proj(_C_U, A_WIDTH))
    va = _gelu(proj(_C_VA, A_WIDTH))
    mu = jnp.mean(va, axis=-1, keepdims=True)
    vc = va - mu
    var = jnp.mean(vc * vc, axis=-1, keepdims=True)
    va = vc * lax.rsqrt(var + EPS) * lng_ref[...] + lnb_ref[...]
    if not prompt:
        va_ref[...] = va
    for c in range(tm // CHUNK):
        rows = slice(c * CHUNK, (c + 1) * CHUNK)
        for g in range(A_GROUPS):
            cols = slice(g * A_GROUP_DIM, (g + 1) * A_GROUP_DIM)
            mixed = jnp.dot(wmix_ref[g], va[rows, cols].astype(_BF16),
                            preferred_element_type=_F32) + bmix_ref[g]
            ya_ref[rows, cols] = (u[rows, cols] * mixed).astype(_BF16)

    q_ref[...] = proj(_C_Q, B_WIDTH)
    k = proj(_C_K, B_WIDTH)
    v = proj(_C_V, B_WIDTH)
    if prompt:
        blocks = tm // MOBA_BLOCK
        vt = v.T
        knat_ref[0] = k.T
        vnat_ref[0] = vt
        for r in range(blocks):
            kmean_ref[0, r:r + 1, :] = jnp.mean(k[r * MOBA_BLOCK:(r + 1) * MOBA_BLOCK], axis=0,
                                                keepdims=True)
        kb = k.astype(_BF16)
        first_blk = (pl.program_id(0) * blocks) % nb
        shape = (tm, _K_LANES - _PAIR_LANES)
        blk_of_row = first_blk + lax.broadcasted_iota(jnp.int32, shape, 0) // MOBA_BLOCK
        onehot = jnp.where(lax.broadcasted_iota(jnp.int32, shape, 1) == blk_of_row, 1.0, 0.0
                           ).astype(_BF16)
        for hp in range(B_HEADS // _HEADS_PER_PAIR):
            kb_ref[:, hp * _K_LANES:hp * _K_LANES + _PAIR_LANES] = (
                kb[:, hp * _PAIR_LANES:(hp + 1) * _PAIR_LANES])
            kb_ref[:, hp * _K_LANES + _PAIR_LANES:(hp + 1) * _K_LANES] = onehot
        vtb = vt.astype(_BF16)
        row = lax.broadcasted_iota(jnp.int32, (_V_ROWS - B_HEAD_DIM, MOBA_BLOCK), 0)
        ones_row = jnp.where(row == 0, 1.0, 0.0).astype(_BF16)
        for r in range(blocks):
            keys = slice(r * MOBA_BLOCK, (r + 1) * MOBA_BLOCK)
            for h in range(B_HEADS):
                vt_ref[0, r, h * _V_ROWS:h * _V_ROWS + B_HEAD_DIM, :] = (
                    vtb[h * B_HEAD_DIM:(h + 1) * B_HEAD_DIM, keys])
                vt_ref[0, r, h * _V_ROWS + B_HEAD_DIM:(h + 1) * _V_ROWS, :] = ones_row
    else:
        k_ref[...] = k
        v_ref[...] = v
    sga_ref[...] = _sigmoid(proj(_C_GA, D_MODEL)).astype(_BF16)
    sgb_ref[...] = _sigmoid(proj(_C_GB, D_MODEL)).astype(_BF16)


def _proj_call(x2, g1, w_in, lng, lnb, wmix, bmix, *, tm, prompt, layer=0, depth=1, seq=None,
               kv_prev=None):
    n = x2.shape[0]
    nt = n // tm
    row = lambda width: pl.BlockSpec((tm, width), lambda i: (i, 0))
    in_specs = [row(D_MODEL), _const_spec((1, D_MODEL)), _const_spec((D_MODEL, N_IN)),
                _const_spec((1, A_WIDTH)), _const_spec((1, A_WIDTH)),
                _const_spec((A_GROUPS, CHUNK, CHUNK)), _const_spec((A_GROUPS, CHUNK, CHUNK))]
    args = [x2, g1, w_in, lng, lnb, wmix, bmix]
    aliases = {}
    tail_shape = [jax.ShapeDtypeStruct((n, A_WIDTH), _BF16),
                  jax.ShapeDtypeStruct((n, D_MODEL), _BF16),
                  jax.ShapeDtypeStruct((n, D_MODEL), _BF16)]
    tail_specs = [row(A_WIDTH), row(D_MODEL), row(D_MODEL)]
    nb = None
    if prompt:
        nb = seq // MOBA_BLOCK
        blocks = tm // MOBA_BLOCK
        tps = seq // tm
        assert tm % MOBA_BLOCK == 0 and seq % tm == 0 and nb <= _K_LANES - _PAIR_LANES
        nseq = n // seq
        k_cols = B_HEADS // _HEADS_PER_PAIR * _K_LANES
        nat = jax.ShapeDtypeStruct((depth * nseq, B_WIDTH, seq), _F32)
        nat_spec = pl.BlockSpec((1, B_WIDTH, tm), lambda i: (layer * nseq + i // tps, 0, i % tps))
        out_shape = [jax.ShapeDtypeStruct((n, B_WIDTH), _F32),
                     jax.ShapeDtypeStruct((n, k_cols), _BF16),
                     jax.ShapeDtypeStruct((nseq, nb, B_HEADS * _V_ROWS, MOBA_BLOCK), _BF16),
                     jax.ShapeDtypeStruct((nt, blocks, B_WIDTH), _F32), nat, nat] + tail_shape
        out_specs = [row(B_WIDTH), row(k_cols),
                     pl.BlockSpec((1, blocks, B_HEADS * _V_ROWS, MOBA_BLOCK),
                                  lambda i: (i // tps, i % tps, 0, 0)),
                     pl.BlockSpec((1, blocks, B_WIDTH), lambda i: (i, 0, 0)), nat_spec, nat_spec
                     ] + tail_specs
        if kv_prev is not None:
            in_specs += [pl.BlockSpec(memory_space=pl.ANY)] * 2
            args += list(kv_prev)
            aliases = {7: 4, 8: 5}
    else:
        out_shape = [jax.ShapeDtypeStruct((n, B_WIDTH), _F32)] * 3 + tail_shape + [
            jax.ShapeDtypeStruct((n, A_WIDTH), _F32)]
        out_specs = [row(B_WIDTH)] * 3 + tail_specs + [row(A_WIDTH)]
    return pl.pallas_call(
        functools.partial(_proj_kernel, tm=tm, prompt=prompt, nb=nb),
        grid=(nt,),
        in_specs=in_specs,
        out_specs=out_specs,
        out_shape=out_shape,
        input_output_aliases=aliases,
        compiler_params=pltpu.CompilerParams(dimension_semantics=("arbitrary",),
                                             vmem_limit_bytes=_VMEM_LIMIT),
        name="proj_mixer_a",
    )(*args)


def _topk_block_mask(gate, own, nb, width):
    blk = lax.broadcasted_iota(jnp.int32, (nb, width), 0)
    gm = jnp.where(blk < own, gate, -jnp.inf)
    rank = jnp.zeros((nb, width), _F32)
    for jp in range(nb):
        r = gm[jp:jp + 1, :]
        beats = jnp.where(r > gm, 1.0, jnp.where(r == gm, jnp.where(blk > jp, 1.0, 0.0), 0.0))
        rank = rank + beats
    past_sel = jnp.where(blk < own, jnp.where(rank < MOBA_TOPK, 1.0, 0.0), 0.0)
    return jnp.where(blk == own, 1.0, past_sel)


def _moba_prompt_kernel(q_ref, k_ref, vt_ref, kmean_ref, btab_ref, o_ref, s_scr, *, nb):
    i = pl.program_id(2)
    blk = MOBA_BLOCK
    heads = range(_PAIRS_PER_STEP * _HEADS_PER_PAIR)

    qt = q_ref[0].T
    feat = lax.broadcasted_iota(jnp.int32, (_PAIR_LANES, blk), 0)
    qz = []
    for hh in heads:
        pair, sub = divmod(hh, _HEADS_PER_PAIR)
        in_head = (feat >= sub * B_HEAD_DIM) & (feat < (sub + 1) * B_HEAD_DIM)
        qh = jnp.where(in_head, qt[pair * _PAIR_LANES:(pair + 1) * _PAIR_LANES], 0.0)
        gate = jnp.dot(kmean_ref[0, :, pair * _PAIR_LANES:(pair + 1) * _PAIR_LANES], qh,
                       preferred_element_type=_F32, precision=lax.Precision.HIGHEST)
        mask_rows = jnp.where(_topk_block_mask(gate, i, nb, blk) > 0.5, 0.0, _NEG)
        pad = jnp.zeros((_K_LANES - _PAIR_LANES - nb, blk), _F32)
        qz.append(jnp.concatenate([qh * (_SCALE * _LOG2E), mask_rows, pad], axis=0).astype(_BF16))

    n_iter = lax.div(i + _BLOCKS_PER_ITER, _BLOCKS_PER_ITER)

    def slots(it):
        js = [it * _BLOCKS_PER_ITER + u for u in range(_BLOCKS_PER_ITER)]
        tabs = [jnp.where(j == i, _TAB_OWN, jnp.where(j == i - 1, _TAB_PREV,
                                                      jnp.where(j > i, _TAB_DEAD, _TAB_FAR)))
                for j in js]
        return js, [jnp.minimum(j, i) for j in js], tabs

    def logits_pass(it, ms, group):
        js, jcs, tabs = slots(it)
        out = []
        for hh, m in zip(group, ms):
            pair = hh // _HEADS_PER_PAIR
            for j, jc, tab in zip(js, jcs, tabs):
                kj = k_ref[0, pl.ds(pl.multiple_of(jc * blk, blk), blk),
                           pair * _K_LANES:(pair + 1) * _K_LANES]
                s = jnp.dot(kj, qz[hh], preferred_element_type=_F32) + btab_ref[hh, tab]
                s_scr[hh, j] = s
                m = jnp.maximum(m, jnp.max(s, axis=0, keepdims=True))
            out.append(m)
        return tuple(out)

    def value_pass(it, accs, group, ms):
        js, jcs, _ = slots(it)
        out = []
        for hh, acc, m in zip(group, accs, ms):
            for j, jc in zip(js, jcs):
                p = jnp.exp2(s_scr[hh, j] - m).astype(_BF16)
                acc = acc + jnp.dot(vt_ref[0, jc, hh * _V_ROWS:(hh + 1) * _V_ROWS, :], p,
                                    preferred_element_type=_F32)
            out.append(acc)
        return tuple(out)

    group = tuple(heads)
    ms = lax.fori_loop(0, n_iter, lambda it, ms: logits_pass(it, ms, group),
                       tuple(jnp.full((1, blk), _NEG, _F32) for _ in group))
    accs = lax.fori_loop(0, n_iter, lambda it, accs: value_pass(it, accs, group, ms),
                         tuple(jnp.zeros((_V_ROWS, blk), _F32) for _ in group))
    outs = [acc[:B_HEAD_DIM] / acc[B_HEAD_DIM:B_HEAD_DIM + 1] for acc in accs]
    o_ref[0] = jnp.concatenate(outs, axis=0).T.astype(_BF16)


def _moba_prompt_call(q3, kb3, vt4, kmean3, btab):
    b, t, _ = q3.shape
    nb = t // MOBA_BLOCK
    g = _PAIRS_PER_STEP
    heads = g * _HEADS_PER_PAIR
    return pl.pallas_call(
        functools.partial(_moba_prompt_kernel, nb=nb),
        grid=(b, B_HEADS // heads, nb),
        in_specs=[pl.BlockSpec((1, MOBA_BLOCK, g * _PAIR_LANES), lambda bb, hp, i: (bb, i, hp)),
                  pl.BlockSpec((1, t, g * _K_LANES), lambda bb, hp, i: (bb, 0, hp)),
                  pl.BlockSpec((1, nb, heads * _V_ROWS, MOBA_BLOCK),
                               lambda bb, hp, i: (bb, 0, hp, 0)),
                  pl.BlockSpec((1, nb, g * _PAIR_LANES), lambda bb, hp, i: (bb, 0, hp)),
                  pl.BlockSpec((heads, _N_TABLES, MOBA_BLOCK, MOBA_BLOCK),
                               lambda bb, hp, i: (hp, 0, 0, 0), pipeline_mode=pl.Buffered(1))],
        out_specs=pl.BlockSpec((1, MOBA_BLOCK, g * _PAIR_LANES), lambda bb, hp, i: (bb, i, hp)),
        out_shape=jax.ShapeDtypeStruct((b, t, B_WIDTH), _BF16),
        scratch_shapes=[pltpu.VMEM((heads, nb + _BLOCKS_PER_ITER - 1, MOBA_BLOCK, MOBA_BLOCK), _F32)],
        compiler_params=pltpu.CompilerParams(
            dimension_semantics=("arbitrary", "arbitrary", "arbitrary"),
            vmem_limit_bytes=_VMEM_LIMIT),
        name="moba_prompt",
    )(q3, kb3, vt4, kmean3, btab)


def _moba_sample_kernel(pt_ref, qbd_ref, qbdt_ref, knew_ref, vnewt_ref, bnear_ref, bfar_ref,
                        bownt_ref, hmaskt_ref, rsum_ref, *rest, nb_past):
    del pt_ref
    pages = _PAGES_PER_STEP
    k_pages = rest[:pages]
    v_pages = rest[pages:2 * pages]
    o_ref = rest[2 * pages]
    gate_scr, m_scr, l_scr, acc_scr, st_scr = rest[2 * pages + 1:]
    c = pl.program_id(1)
    rows = qbd_ref.shape[1]
    tdec = o_ref.shape[1]
    blocks_per_step = pages // _PAGES_PER_BLOCK

    qz = (qbd_ref[0] * _SCALE).astype(_BF16)
    qbdt = qbdt_ref[0]
    row_pad = jnp.zeros((128 - rows, MOBA_BLOCK), _F32)

    for bl in range(blocks_per_step):
        jg = c * blocks_per_step + bl
        kts = [k_pages[bl * _PAGES_PER_BLOCK + p][0] for p in range(_PAGES_PER_BLOCK)]
        kt = jnp.concatenate(kts, axis=1)
        kmean = jnp.sum(functools.reduce(lambda a, b: a + b, kts), axis=-1,
                        keepdims=True) * (1.0 / MOBA_BLOCK)
        gate_scr[pl.ds(jg, 1), :] = jnp.sum(qbdt * kmean, axis=0, keepdims=True)
        s = jnp.dot(qz, kt.astype(_BF16), preferred_element_type=_F32)
        s = s + jnp.where(jg == nb_past - 1, bnear_ref[...], bfar_ref[:, 0:1])
        st_scr[bl] = jnp.concatenate([s, row_pad], axis=0).T
    for bl in range(blocks_per_step):
        jg = c * blocks_per_step + bl
        vt = jnp.concatenate([v_pages[bl * _PAGES_PER_BLOCK + p][0]
                              for p in range(_PAGES_PER_BLOCK)], axis=1)
        st = st_scr[bl]
        m = jnp.max(st, axis=0, keepdims=True)
        p = jnp.exp(st - m)
        m_scr[pl.ds(jg, 1), :] = m
        l_scr[pl.ds(jg, 1), :] = jnp.sum(p, axis=0, keepdims=True)
        acc_scr[jg] = jnp.dot(vt.astype(_BF16), p.astype(_BF16), preferred_element_type=_F32)

    @pl.when(c == pl.num_programs(1) - 1)
    def _():
        sel = _topk_block_mask(gate_scr[...], nb_past, nb_past, 128) > 0.5
        m_blk = m_scr[...]
        s_own = jnp.dot(knew_ref[0], qbdt * _SCALE, preferred_element_type=_F32) + bownt_ref[...]
        m_tot = jnp.maximum(jnp.max(s_own, axis=0, keepdims=True),
                            jnp.max(jnp.where(sel, m_blk, _NEG), axis=0, keepdims=True))
        w = jnp.where(sel, jnp.exp(m_blk - m_tot), 0.0)
        p_own = jnp.exp(s_own - m_tot)
        l_tot = jnp.sum(p_own, axis=0, keepdims=True) + jnp.sum(w * l_scr[...], axis=0, keepdims=True)
        accs = [jnp.dot(vnewt_ref[0], p_own, preferred_element_type=_F32)]
        accs += [jnp.zeros_like(accs[0]) for _ in range(3)]
        for j in range(nb_past):
            accs[j % 4] = accs[j % 4] + w[j:j + 1, :] * acc_scr[j]
        out_t = ((accs[0] + accs[1]) + (accs[2] + accs[3])) / l_tot
        folded = lax.dot_general(rsum_ref[...], out_t * hmaskt_ref[...], (((1,), (1,)), ((), ())),
                                 preferred_element_type=_F32, precision=lax.Precision.HIGHEST)
        o_ref[0] = folded[:tdec]


def _moba_sample_call(page_flat, qbd, qbdt, knew, vnewt, bnear, bfar, bownt, hmaskt, rsum,
                      cache_kt, cache_vt, *, layer, n_pool, n_pages, tdec):
    nbatch, rows, _ = qbd.shape
    pages = _PAGES_PER_STEP
    nb_past = n_pages // _PAGES_PER_BLOCK
    assert nb_past % 8 == 0 and rows <= 128
    base = layer * n_pool

    def page_spec(p):
        return pl.BlockSpec((1, B_WIDTH, CHUNK),
                            lambda bb, cc, pt, p=p: (base + pt[bb * n_pages + cc * pages + p], 0, 0))

    per_batch = lambda arr: pl.BlockSpec((1,) + arr.shape[1:], lambda bb, cc, pt: (bb, 0, 0))
    const = lambda arr: pl.BlockSpec(arr.shape, lambda bb, cc, pt: (0,) * arr.ndim)
    grid_spec = pltpu.PrefetchScalarGridSpec(
        num_scalar_prefetch=1,
        grid=(nbatch, n_pages // pages),
        in_specs=[per_batch(qbd), per_batch(qbdt), per_batch(knew), per_batch(vnewt),
                  const(bnear), const(bfar), const(bownt), const(hmaskt), const(rsum)]
                 + [page_spec(p) for p in range(pages)] * 2,
        out_specs=pl.BlockSpec((1, tdec, B_WIDTH), lambda bb, cc, pt: (bb, 0, 0)),
        scratch_shapes=[pltpu.VMEM((nb_past, 128), _F32),
                        pltpu.VMEM((nb_past, 128), _F32),
                        pltpu.VMEM((nb_past, 128), _F32),
                        pltpu.VMEM((nb_past, B_WIDTH, 128), _F32),
                        pltpu.VMEM((pages // _PAGES_PER_BLOCK, MOBA_BLOCK, 128), _F32)])
    return pl.pallas_call(
        functools.partial(_moba_sample_kernel, nb_past=nb_past),
        grid_spec=grid_spec,
        out_shape=jax.ShapeDtypeStruct((nbatch, tdec, B_WIDTH), _F32),
        compiler_params=pltpu.CompilerParams(dimension_semantics=("arbitrary", "arbitrary"),
                                             vmem_limit_bytes=_VMEM_LIMIT),
        name="moba_sample",
    )(page_flat, qbd, qbdt, knew, vnewt, bnear, bfar, bownt, hmaskt, rsum,
      *([cache_kt] * pages), *([cache_vt] * pages))


def _post_kernel(x_ref, ya_ref, yb_ref, sga_ref, sgb_ref, wpa_ref, wpb_ref, wo_ref, g2_ref,
                 wup_ref, cw_ref, cb_ref, wdn_ref, gf_ref, *rest, tm, tiles_per_seq, per_row_state,
                 final_norm):
    if per_row_state:
        s1_ref, s2_ref, tpos_ref = rest[:3]
        rest = rest[3:]
    else:
        st_ref = rest[0]
        rest = rest[1:]
    xo_ref, a_ref = rest[:2]
    rest = rest[2:]
    if final_norm:
        y_ref = rest[0]
        rest = rest[1:]
    if not per_row_state:
        carry_scr = rest[0]

    ma = jnp.dot(ya_ref[...], wpa_ref[...], preferred_element_type=_F32)
    mb = jnp.dot(yb_ref[...], wpb_ref[...], preferred_element_type=_F32)
    merged = sga_ref[...].astype(_F32) * ma + sgb_ref[...].astype(_F32) * mb
    x1 = x_ref[...] + jnp.dot(merged.astype(_BF16), wo_ref[...], preferred_element_type=_F32)

    xn = _rms(x1, g2_ref[...]).astype(_BF16)
    if not per_row_state:
        @pl.when(pl.program_id(0) % tiles_per_seq == 0)
        def _():
            carry_scr[...] = st_ref[0]

    x2 = x1
    for lo, hi in zip(_FFN_SPLITS[:-1], _FFN_SPLITS[1:]):
        cols = slice(lo, hi)
        width = hi - lo
        a = jnp.dot(xn, wup_ref[:, cols], preferred_element_type=_F32)
        gate = jnp.dot(xn, wup_ref[:, D_FF + lo:D_FF + hi], preferred_element_type=_F32)
        prev1 = pltpu.roll(a, 1, 0)
        prev2 = pltpu.roll(a, 2, 0)
        if per_row_state:
            tpos = tpos_ref[:, cols]
            prev1 = jnp.where(tpos >= 1, prev1, s1_ref[:, cols])
            prev2 = jnp.where(tpos >= 2, prev2, s2_ref[:, cols])
            a_ref[:, cols] = a
        else:
            row = lax.broadcasted_iota(jnp.int32, (8, width), 0)
            c0 = carry_scr[0:1, cols]
            c1 = carry_scr[1:2, cols]
            top1 = jnp.where(row == 0, c1, prev1[:8])
            top2 = jnp.where(row == 0, c0, jnp.where(row == 1, c1, prev2[:8]))
            prev1 = jnp.concatenate([top1, prev1[8:]], axis=0)
            prev2 = jnp.concatenate([top2, prev2[8:]], axis=0)
            tail = a[tm - (CONV_W - 1):, :]
            carry_scr[:, cols] = tail
            a_ref[0, :, cols] = tail
        conv = (cb_ref[:, cols] + prev2 * cw_ref[0:1, cols] + prev1 * cw_ref[1:2, cols]
                + a * cw_ref[2:3, cols])
        act = (_gelu(conv) * gate).astype(_BF16)
        x2 = x2 + jnp.dot(act, wdn_ref[cols, :], preferred_element_type=_F32)
    xo_ref[...] = x2
    if final_norm:
        y_ref[...] = _rms(x2, gf_ref[...])


def _post_call(x2, ya, yb, sga, sgb, wpa, wpb, wo, g2, wup, cw, cb, wdn, gf, state_args, *,
               tm, tiles_per_seq, per_row_state, final_norm):
    n = x2.shape[0]
    row = lambda width: pl.BlockSpec((tm, width), lambda i: (i, 0))
    in_specs = [row(D_MODEL), row(A_WIDTH), row(B_WIDTH), row(D_MODEL), row(D_MODEL),
                _const_spec((A_WIDTH, D_MODEL)), _const_spec((B_WIDTH, D_MODEL)),
                _const_spec((D_MODEL, D_MODEL)), _const_spec((1, D_MODEL)),
                _const_spec((D_MODEL, 2 * D_FF)), _const_spec((CONV_W, D_FF)),
                _const_spec((1, D_FF)), _const_spec((D_FF, D_MODEL)), _const_spec((1, D_MODEL))]
    out_shape = [jax.ShapeDtypeStruct((n, D_MODEL), _F32)]
    out_specs = [row(D_MODEL)]
    scratch = []
    if per_row_state:
        in_specs += [row(D_FF), row(D_FF), row(D_FF)]
        out_shape.append(jax.ShapeDtypeStruct((n, D_FF), _F32))
        out_specs.append(row(D_FF))
    else:
        nseq = n // (tm * tiles_per_seq)
        in_specs.append(pl.BlockSpec((1, CONV_W - 1, D_FF), lambda i: (i // tiles_per_seq, 0, 0)))
        out_shape.append(jax.ShapeDtypeStruct((nseq, CONV_W - 1, D_FF), _F32))
        out_specs.append(pl.BlockSpec((1, CONV_W - 1, D_FF), lambda i: (i // tiles_per_seq, 0, 0)))
        scratch.append(pltpu.VMEM((CONV_W - 1, D_FF), _F32))
    if final_norm:
        out_shape.append(jax.ShapeDtypeStruct((n, D_MODEL), _F32))
        out_specs.append(row(D_MODEL))
    return pl.pallas_call(
        functools.partial(_post_kernel, tm=tm, tiles_per_seq=tiles_per_seq,
                          per_row_state=per_row_state, final_norm=final_norm),
        grid=(n // tm,),
        in_specs=in_specs,
        out_specs=out_specs,
        out_shape=out_shape,
        scratch_shapes=scratch,
        compiler_params=pltpu.CompilerParams(dimension_semantics=("arbitrary",),
                                             vmem_limit_bytes=_VMEM_LIMIT),
        name="merge_convffn",
    )(x2, ya, yb, sga, sgb, wpa, wpb, wo, g2, wup, cw, cb, wdn, gf, *state_args)


def _t5_bucket(rel):
    max_exact = N_BUCKETS // 2
    relf = jnp.maximum(rel, 1).astype(_F32)
    large = max_exact + (jnp.log(relf / max_exact) / math.log(MAX_DISTANCE / max_exact)
                         * (N_BUCKETS - max_exact)).astype(jnp.int32)
    large = jnp.minimum(large, N_BUCKETS - 1)
    return jnp.where(rel < max_exact, rel, large)


def _far_bucket_is_last(min_rel):
    max_exact = N_BUCKETS // 2
    v = max_exact + int(math.log(min_rel / max_exact) / math.log(MAX_DISTANCE / max_exact)
                        * (N_BUCKETS - max_exact) - 1e-6)
    return v >= N_BUCKETS - 1


def _toeplitz(diag, n):
    width = 2 * n
    flat = jnp.tile(diag, (1,) * (diag.ndim - 1) + (n,))
    skew = flat[..., :n * (width - 1)].reshape(diag.shape[:-1] + (n, width - 1))
    return skew[..., n - 1:2 * n - 1]


def _prompt_bias_tables(rel_bias):
    assert _far_bucket_is_last(MOBA_BLOCK + 1)
    bt = rel_bias.T.astype(_F32)
    bt = (bt - bt[:, N_BUCKETS - 1:]) * _LOG2E
    rel = jnp.arange(2 * MOBA_BLOCK, dtype=jnp.int32) - (MOBA_BLOCK - 1)
    own = jnp.where(rel >= 0, bt[:, _t5_bucket(jnp.maximum(rel, 0))], _NEG)
    prev = bt[:, _t5_bucket(rel + MOBA_BLOCK)]
    own, prev = _toeplitz(own, MOBA_BLOCK), _toeplitz(prev, MOBA_BLOCK)
    tables = [None] * _N_TABLES
    tables[_TAB_OWN], tables[_TAB_PREV] = own, prev
    tables[_TAB_FAR], tables[_TAB_DEAD] = jnp.zeros_like(own), jnp.full_like(own, _NEG)
    return jnp.stack(tables, axis=1)


def _sample_bias_tables(rel_bias, tdec, tpad):
    assert _far_bucket_is_last(MOBA_BLOCK + 1)
    bt = rel_bias.T.astype(_F32)
    tok = jnp.arange(tdec, dtype=jnp.int32)[:, None]
    key = jnp.arange(MOBA_BLOCK, dtype=jnp.int32)[None, :]
    near = bt[:, _t5_bucket(MOBA_BLOCK + tok - key)]
    far = jnp.broadcast_to(bt[:, N_BUCKETS - 1][:, None, None], (B_HEADS, tdec, 128))
    new = jnp.arange(tpad, dtype=jnp.int32)[None, :]
    own = jnp.where((tok - new >= 0) & (new < tdec), bt[:, _t5_bucket(jnp.maximum(tok - new, 0))],
                    _NEG)
    flat = lambda a: a.reshape(B_HEADS * tdec, a.shape[-1])
    return flat(near), flat(far), flat(own)


def kernel(x_prompt, x_sample, cache_k, cache_v, state_conv, page_table, w_in, ln_v_g, ln_v_b, w_s,
           b_s, w_pa, w_pb, w_o, norm1_g, norm2_g, w_up, conv_w, conv_b, w_down, rel_bias, norm_f):
    nbp_, seq, _ = x_prompt.shape
    nbs, tdec, _ = x_sample.shape
    depth, n_pool, page_size = cache_k.shape[:3]
    n_pages = page_table.shape[1]
    ns = nbs * tdec
    np_ = nbp_ * seq
    assert page_size == CHUNK and n_pages % _PAGES_PER_STEP == 0 and seq % MOBA_BLOCK == 0
    assert _PAGES_PER_STEP % _PAGES_PER_BLOCK == 0 and seq % _POST_TILE == 0
    assert CONV_W - 1 <= tdec <= CHUNK and ns % CHUNK == 0 and CHUNK % tdec == 0
    tpad = -(-tdec // 8) * 8
    tm_p = _PROJ_TILE
    nb = seq // MOBA_BLOCK
    nb_past = n_pages // _PAGES_PER_BLOCK
    rows = B_HEADS * tdec

    xp = x_prompt.reshape(np_, D_MODEL)
    xs = x_sample.reshape(ns, D_MODEL)
    cache_kt = cache_k.transpose(0, 1, 3, 4, 2).reshape(depth * n_pool, B_WIDTH, page_size)
    cache_vt = cache_v.transpose(0, 1, 3, 4, 2).reshape(depth * n_pool, B_WIDTH, page_size)
    page_flat = page_table.reshape(-1).astype(jnp.int32)

    btab = _prompt_bias_tables(rel_bias)
    bnear_s, bfar_s, bown_s = _sample_bias_tables(rel_bias, tdec, tpad)
    row_head = jnp.arange(rows, dtype=jnp.int32)[:, None] // tdec
    hmask = (jnp.arange(B_WIDTH, dtype=jnp.int32)[None, :] // B_HEAD_DIM == row_head).astype(_F32)
    rsum = (jnp.arange(rows, dtype=jnp.int32)[None, :] % tdec
            == jnp.arange(tdec, dtype=jnp.int32)[:, None]).astype(_F32)
    lane_pad = lambda a: jnp.pad(a, ((0, 0), (0, 128 - rows)))
    bownt_s, hmaskt = lane_pad(bown_s.T), lane_pad(hmask.T)
    rsum_p = lane_pad(jnp.pad(rsum, ((0, tpad - tdec), (0, 0))))
    tpos = jnp.broadcast_to((jnp.arange(ns, dtype=jnp.int32) % tdec)[:, None], (ns, D_FF))
    tril = jnp.tril(jnp.ones((CHUNK, CHUNK), bool))
    eye_s = jnp.eye(CHUNK // tdec, dtype=_F32)
    zero_state = jnp.zeros((nbp_, CONV_W - 1, D_FF), _F32)

    ks_l, vs_l, as_l, cp_l, cs_l = [], [], [], [], []
    yp = ys = kv_nat = None
    for l in range(depth):
        last = l == depth - 1
        w_in_b = w_in[l].astype(_BF16)
        g1 = norm1_g[l][None, :]
        lng, lnb = ln_v_g[l][None, :], ln_v_b[l][None, :]
        wpa, wpb, wo = w_pa[l].astype(_BF16), w_pb[l].astype(_BF16), w_o[l].astype(_BF16)
        g2 = norm2_g[l][None, :]
        wup, wdn = w_up[l].astype(_BF16), w_down[l].astype(_BF16)
        cw, cb = conv_w[l], conv_b[l][None, :]
        gf = norm_f[None, :]
        wmix_p = jnp.where(tril[None], w_s[l], 0.0).astype(_BF16)
        bmix_p = jnp.broadcast_to(b_s[l][:, :, None], (A_GROUPS, CHUNK, CHUNK))
        w_small = jnp.where(tril[None, :tdec, :tdec], w_s[l][:, :tdec, :tdec], 0.0)
        wmix_s = jnp.einsum("ab,gts->gatbs", eye_s, w_small).reshape(A_GROUPS, CHUNK, CHUNK)
        wmix_s = wmix_s.astype(_BF16)
        bmix_s = jnp.broadcast_to(jnp.tile(b_s[l][:, :tdec], (1, CHUNK // tdec))[:, :, None],
                                  (A_GROUPS, CHUNK, CHUNK))

        (qp, kbp, vtp, kmeanp, knat, vnat, yap, sgap, sgbp) = _proj_call(
            xp, g1, w_in_b, lng, lnb, wmix_p, bmix_p, tm=tm_p, prompt=True, layer=l, depth=depth,
            seq=seq, kv_prev=kv_nat)
        kv_nat = (knat, vnat)
        ybp = _moba_prompt_call(qp.reshape(nbp_, seq, B_WIDTH), kbp.reshape(nbp_, seq, -1),
                                vtp, kmeanp.reshape(nbp_, nb, B_WIDTH), btab)
        outs = _post_call(xp, yap, ybp.reshape(np_, B_WIDTH), sgap, sgbp, wpa, wpb, wo, g2,
                          wup, cw, cb, wdn, gf, (zero_state,), tm=_POST_TILE,
                          tiles_per_seq=seq // _POST_TILE, per_row_state=False, final_norm=last)
        xp, cp = outs[0], outs[1]
        if last:
            yp = outs[2]

        qs, ks, vs, yas, sgas, sgbs, vas = _proj_call(xs, g1, w_in_b, lng, lnb, wmix_s, bmix_s,
                                                      tm=CHUNK, prompt=False)
        ks4 = ks.reshape(nbs, tdec, B_HEADS, B_HEAD_DIM)
        vs4 = vs.reshape(nbs, tdec, B_HEADS, B_HEAD_DIM)
        qbd = (qs.reshape(nbs, 1, tdec, B_WIDTH) * hmask.reshape(B_HEADS, tdec, B_WIDTH)[None]
               ).reshape(nbs, rows, B_WIDTH)
        qbdt = jnp.pad(qbd.transpose(0, 2, 1), ((0, 0), (0, 0), (0, 128 - rows)))
        pad = ((0, 0), (0, tpad - tdec), (0, 0))
        knew = jnp.pad(ks.reshape(nbs, tdec, B_WIDTH), pad)
        vnewt = jnp.pad(vs.reshape(nbs, tdec, B_WIDTH), pad).transpose(0, 2, 1)
        ybs = _moba_sample_call(page_flat, qbd, qbdt, knew, vnewt, bnear_s, bfar_s, bownt_s, hmaskt,
                                rsum_p, cache_kt, cache_vt, layer=l, n_pool=n_pool, n_pages=n_pages,
                                tdec=tdec)
        ybs = ybs.reshape(ns, B_WIDTH)
        st = state_conv[l]
        s1 = jnp.broadcast_to(st[:, 1:2, :], (nbs, tdec, D_FF)).reshape(ns, D_FF)
        s2 = jnp.concatenate([st, jnp.zeros((nbs, tdec - (CONV_W - 1), D_FF), _F32)],
                             axis=1).reshape(ns, D_FF)
        outs = _post_call(xs, yas, ybs.astype(_BF16), sgas, sgbs, wpa, wpb, wo,
                          g2, wup, cw, cb, wdn, gf, (s1, s2, tpos), tm=CHUNK, tiles_per_seq=1,
                          per_row_state=True, final_norm=last)
        xs, a_s = outs[0], outs[1]
        if last:
            ys = outs[2]

        ks_l.append(ks4)
        vs_l.append(vs4)
        as_l.append(vas.reshape(nbs, tdec, A_WIDTH))
        cp_l.append(cp)
        a_ext = jnp.concatenate([st, a_s.reshape(nbs, tdec, D_FF)], axis=1)
        cs_l.append(a_ext[:, -(CONV_W - 1):])

    def kv_out(t):
        return t.reshape(depth, nbp_, B_HEADS, B_HEAD_DIM, seq).transpose(0, 1, 4, 2, 3)

    return (yp.reshape(nbp_, seq, D_MODEL), ys.reshape(nbs, tdec, D_MODEL),
            kv_out(kv_nat[0]), kv_out(kv_nat[1]),
            jnp.stack(ks_l, 0), jnp.stack(vs_l, 0),
            jnp.stack(as_l, 0), jnp.stack(cp_l, 0), jnp.stack(cs_l, 0))
```

```python
import functools
import math

import jax
import jax.numpy as jnp
from jax import lax
from jax.experimental import pallas as pl
from jax.experimental.pallas import tpu as pltpu

D_MODEL = 1024
CHUNK = 128
A_WIDTH = D_MODEL // 2
A_GROUPS = 4
A_GROUP_DIM = A_WIDTH // A_GROUPS
B_HEADS = 8
B_HEAD_DIM = 64
B_WIDTH = B_HEADS * B_HEAD_DIM
MOBA_BLOCK = 256
MOBA_TOPK = 3
N_BUCKETS = 32
MAX_DISTANCE = 128
D_FF = 2816
CONV_W = 3
EPS = 1e-6
N_IN = 2 * A_WIDTH + 3 * B_WIDTH + 2 * D_MODEL

_C_U, _C_VA, _C_Q, _C_K, _C_V, _C_GA, _C_GB = 0, 512, 1024, 1536, 2048, 2560, 3584

_NEG = -1e30
_HEADS_PER_PAIR = 2
_PAIRS_PER_STEP = 2
_BLOCKS_PER_ITER = 4
_TAB_OWN, _TAB_PREV, _TAB_FAR, _TAB_DEAD, _N_TABLES = 0, 1, 2, 3, 4
_PAIR_LANES = _HEADS_PER_PAIR * B_HEAD_DIM
_K_LANES = 2 * _PAIR_LANES
_V_ROWS = B_HEAD_DIM + 16
_LOG2E = math.log2(math.e)
_PAGES_PER_STEP = 16
_PAGES_PER_BLOCK = MOBA_BLOCK // CHUNK
_PROJ_TILE = 512
_POST_TILE = 512
_FFN_SPLITS = (0, 1536, D_FF)
_VMEM_LIMIT = 48 * 1024 * 1024
_SCALE = 1.0 / math.sqrt(B_HEAD_DIM)

_F32 = jnp.float32
_BF16 = jnp.bfloat16


def _gelu(x):
    k = -2.0 * 0.7978845608028654 * _LOG2E
    return x / (1.0 + jnp.exp2(x * (k + (k * 0.044715) * (x * x))))


def _sigmoid(x):
    return 1.0 / (1.0 + jnp.exp2(x * -_LOG2E))


def _rms(x, g):
    return x * lax.rsqrt(jnp.mean(x * x, axis=-1, keepdims=True) + EPS) * g


def _const_spec(shape):
    zeros = (0,) * len(shape)
    return pl.BlockSpec(shape, lambda *_: zeros, pipeline_mode=pl.Buffered(1))


def _layer_spec(shape, layer):
    index = (layer,) + (0,) * len(shape)
    return pl.BlockSpec((None,) + shape, lambda *_: index, pipeline_mode=pl.Buffered(1))


def _proj_kernel(x_ref, g1_ref, w_ref, lng_ref, lnb_ref, wmix_ref, bmix_ref, *rest, tm, prompt,
                 nb=None):
    if prompt:
        (q_ref, kb_ref, vt_ref, kmean_ref, knat_ref, vnat_ref, ya_ref, sga_ref, sgb_ref) = rest[-9:]
    else:
        (q_ref, k_ref, v_ref, ya_ref, sga_ref, sgb_ref, va_ref) = rest
    xn = _rms(x_ref[...], g1_ref[...]).astype(_BF16)

    def proj(lo, width):
        return jnp.dot(xn, w_ref[:, lo:lo + width], preferred_element_type=_F32)

    u = _gelu(proj(_C_U, A_WIDTH))
    va = _gelu(proj(_C_VA, A_WIDTH))
    mu = jnp.mean(va, axis=-1, keepdims=True)
    vc = va - mu
    var = jnp.mean(vc * vc, axis=-1, keepdims=True)
    va = vc * lax.rsqrt(var + EPS) * lng_ref[...] + lnb_ref[...]
    if not prompt:
        va_ref[...] = va
    for c in range(tm // CHUNK):
        rows = slice(c * CHUNK, (c + 1) * CHUNK)
        for g in range(A_GROUPS):
            cols = slice(g * A_GROUP_DIM, (g + 1) * A_GROUP_DIM)
            mixed = jnp.dot(wmix_ref[g], va[rows, cols].astype(_BF16),
                            preferred_element_type=_F32) + bmix_ref[g]
            ya_ref[rows, cols] = (u[rows, cols] * mixed).astype(_BF16)

    q_ref[...] = proj(_C_Q, B_WIDTH)
    k = proj(_C_K, B_WIDTH)
    v = proj(_C_V, B_WIDTH)
    if prompt:
        blocks = tm // MOBA_BLOCK
        vt = v.T
        knat_ref[0] = k.T
        vnat_ref[0] = vt
        for r in range(blocks):
            kmean_ref[0, r:r + 1, :] = jnp.mean(k[r * MOBA_BLOCK:(r + 1) * MOBA_BLOCK], axis=0,
                                                keepdims=True)
        kb = k.astype(_BF16)
        first_blk = (pl.program_id(0) * blocks) % nb
        shape = (tm, _K_LANES - _PAIR_LANES)
        blk_of_row = first_blk + lax.broadcasted_iota(jnp.int32, shape, 0) // MOBA_BLOCK
        onehot = jnp.where(lax.broadcasted_iota(jnp.int32, shape, 1) == blk_of_row, 1.0, 0.0
                           ).astype(_BF16)
        for hp in range(B_HEADS // _HEADS_PER_PAIR):
            kb_ref[:, hp * _K_LANES:hp * _K_LANES + _PAIR_LANES] = (
                kb[:, hp * _PAIR_LANES:(hp + 1) * _PAIR_LANES])
            kb_ref[:, hp * _K_LANES + _PAIR_LANES:(hp + 1) * _K_LANES] = onehot
        vtb = vt.astype(_BF16)
        row = lax.broadcasted_iota(jnp.int32, (_V_ROWS - B_HEAD_DIM, MOBA_BLOCK), 0)
        ones_row = jnp.where(row == 0, 1.0, 0.0).astype(_BF16)
        for r in range(blocks):
            keys = slice(r * MOBA_BLOCK, (r + 1) * MOBA_BLOCK)
            for h in range(B_HEADS):
                vt_ref[0, r, h * _V_ROWS:h * _V_ROWS + B_HEAD_DIM, :] = (
                    vtb[h * B_HEAD_DIM:(h + 1) * B_HEAD_DIM, keys])
                vt_ref[0, r, h * _V_ROWS + B_HEAD_DIM:(h + 1) * _V_ROWS, :] = ones_row
    else:
        k_ref[...] = k
        v_ref[...] = v
    sga_ref[...] = _sigmoid(proj(_C_GA, D_MODEL)).astype(_BF16)
    sgb_ref[...] = _sigmoid(proj(_C_GB, D_MODEL)).astype(_BF16)


def _proj_call(x2, g1, w_in, lng, lnb, wmix, bmix, *, tm, prompt, layer=0, depth=1, seq=None,
               kv_prev=None):
    n = x2.shape[0]
    nt = n // tm
    row = lambda width: pl.BlockSpec((tm, width), lambda i: (i, 0))
    in_specs = [row(D_MODEL), _const_spec((1, D_MODEL)), _layer_spec((D_MODEL, N_IN), layer),
                _const_spec((1, A_WIDTH)), _const_spec((1, A_WIDTH)),
                _const_spec((A_GROUPS, CHUNK, CHUNK)), _const_spec((A_GROUPS, CHUNK, CHUNK))]
    args = [x2, g1, w_in, lng, lnb, wmix, bmix]
    aliases = {}
    tail_shape = [jax.ShapeDtypeStruct((n, A_WIDTH), _BF16),
                  jax.ShapeDtypeStruct((n, D_MODEL), _BF16),
                  jax.ShapeDtypeStruct((n, D_MODEL), _BF16)]
    tail_specs = [row(A_WIDTH), row(D_MODEL), row(D_MODEL)]
    nb = None
    if prompt:
        nb = seq // MOBA_BLOCK
        blocks = tm // MOBA_BLOCK
        tps = seq // tm
        assert tm % MOBA_BLOCK == 0 and seq % tm == 0 and nb <= _K_LANES - _PAIR_LANES
        nseq = n // seq
        k_cols = B_HEADS // _HEADS_PER_PAIR * _K_LANES
        nat = jax.ShapeDtypeStruct((depth * nseq, B_WIDTH, seq), _F32)
        nat_spec = pl.BlockSpec((1, B_WIDTH, tm), lambda i: (layer * nseq + i // tps, 0, i % tps))
        out_shape = [jax.ShapeDtypeStruct((n, B_WIDTH), _F32),
                     jax.ShapeDtypeStruct((n, k_cols), _BF16),
                     jax.ShapeDtypeStruct((nseq, nb, B_HEADS * _V_ROWS, MOBA_BLOCK), _BF16),
                     jax.ShapeDtypeStruct((nt, blocks, B_WIDTH), _F32), nat, nat] + tail_shape
        out_specs = [row(B_WIDTH), row(k_cols),
                     pl.BlockSpec((1, blocks, B_HEADS * _V_ROWS, MOBA_BLOCK),
                                  lambda i: (i // tps, i % tps, 0, 0)),
                     pl.BlockSpec((1, blocks, B_WIDTH), lambda i: (i, 0, 0)), nat_spec, nat_spec
                     ] + tail_specs
        if kv_prev is not None:
            in_specs += [pl.BlockSpec(memory_space=pl.ANY)] * 2
            args += list(kv_prev)
            aliases = {7: 4, 8: 5}
    else:
        out_shape = [jax.ShapeDtypeStruct((n, B_WIDTH), _F32)] * 3 + tail_shape + [
            jax.ShapeDtypeStruct((n, A_WIDTH), _F32)]
        out_specs = [row(B_WIDTH)] * 3 + tail_specs + [row(A_WIDTH)]
    return pl.pallas_call(
        functools.partial(_proj_kernel, tm=tm, prompt=prompt, nb=nb),
        grid=(nt,),
        in_specs=in_specs,
        out_specs=out_specs,
        out_shape=out_shape,
        input_output_aliases=aliases,
        compiler_params=pltpu.CompilerParams(dimension_semantics=("arbitrary",),
                                             vmem_limit_bytes=_VMEM_LIMIT),
        name="proj_mixer_a",
    )(*args)


def _topk_block_mask(gate, own, nb, width):
    blk = lax.broadcasted_iota(jnp.int32, (nb, width), 0)
    gm = jnp.where(blk < own, gate, -jnp.inf)
    rank = jnp.zeros((nb, width), _F32)
    for jp in range(nb):
        r = gm[jp:jp + 1, :]
        beats = jnp.where(r > gm, 1.0, jnp.where(r == gm, jnp.where(blk > jp, 1.0, 0.0), 0.0))
        rank = rank + beats
    past_sel = jnp.where(blk < own, jnp.where(rank < MOBA_TOPK, 1.0, 0.0), 0.0)
    return jnp.where(blk == own, 1.0, past_sel)


def _moba_prompt_kernel(q_ref, k_ref, vt_ref, kmean_ref, btab_ref, o_ref, s_scr, *, nb):
    i = pl.program_id(2)
    blk = MOBA_BLOCK
    heads = range(_PAIRS_PER_STEP * _HEADS_PER_PAIR)

    qt = q_ref[0].T
    feat = lax.broadcasted_iota(jnp.int32, (_PAIR_LANES, blk), 0)
    qz = []
    for hh in heads:
        pair, sub = divmod(hh, _HEADS_PER_PAIR)
        in_head = (feat >= sub * B_HEAD_DIM) & (feat < (sub + 1) * B_HEAD_DIM)
        qh = jnp.where(in_head, qt[pair * _PAIR_LANES:(pair + 1) * _PAIR_LANES], 0.0)
        gate = jnp.dot(kmean_ref[0, :, pair * _PAIR_LANES:(pair + 1) * _PAIR_LANES], qh,
                       preferred_element_type=_F32, precision=lax.Precision.HIGHEST)
        mask_rows = jnp.where(_topk_block_mask(gate, i, nb, blk) > 0.5, 0.0, _NEG)
        pad = jnp.zeros((_K_LANES - _PAIR_LANES - nb, blk), _F32)
        qz.append(jnp.concatenate([qh * (_SCALE * _LOG2E), mask_rows, pad], axis=0).astype(_BF16))

    n_iter = lax.div(i + _BLOCKS_PER_ITER, _BLOCKS_PER_ITER)

    def slots(it):
        js = [it * _BLOCKS_PER_ITER + u for u in range(_BLOCKS_PER_ITER)]
        tabs = [jnp.where(j == i, _TAB_OWN, jnp.where(j == i - 1, _TAB_PREV,
                                                      jnp.where(j > i, _TAB_DEAD, _TAB_FAR)))
                for j in js]
        return js, [jnp.minimum(j, i) for j in js], tabs

    def logits_pass(it, ms, group):
        js, jcs, tabs = slots(it)
        out = []
        for hh, m in zip(group, ms):
            pair = hh // _HEADS_PER_PAIR
            for j, jc, tab in zip(js, jcs, tabs):
                kj = k_ref[0, pl.ds(pl.multiple_of(jc * blk, blk), blk),
                           pair * _K_LANES:(pair + 1) * _K_LANES]
                s = jnp.dot(kj, qz[hh], preferred_element_type=_F32) + btab_ref[hh, tab]
                s_scr[hh, j] = s
                m = jnp.maximum(m, jnp.max(s, axis=0, keepdims=True))
            out.append(m)
        return tuple(out)

    def value_pass(it, accs, group, ms):
        js, jcs, _ = slots(it)
        out = []
        for hh, acc, m in zip(group, accs, ms):
            for j, jc in zip(js, jcs):
                p = jnp.exp2(s_scr[hh, j] - m).astype(_BF16)
                acc = acc + jnp.dot(vt_ref[0, jc, hh * _V_ROWS:(hh + 1) * _V_ROWS, :], p,
                                    preferred_element_type=_F32)
            out.append(acc)
        return tuple(out)

    group = tuple(heads)
    ms = lax.fori_loop(0, n_iter, lambda it, ms: logits_pass(it, ms, group),
                       tuple(jnp.full((1, blk), _NEG, _F32) for _ in group))
    accs = lax.fori_loop(0, n_iter, lambda it, accs: value_pass(it, accs, group, ms),
                         tuple(jnp.zeros((_V_ROWS, blk), _F32) for _ in group))
    outs = [acc[:B_HEAD_DIM] / acc[B_HEAD_DIM:B_HEAD_DIM + 1] for acc in accs]
    o_ref[0] = jnp.concatenate(outs, axis=0).T.astype(_BF16)


def _moba_prompt_call(q3, kb3, vt4, kmean3, btab):
    b, t, _ = q3.shape
    nb = t // MOBA_BLOCK
    g = _PAIRS_PER_STEP
    heads = g * _HEADS_PER_PAIR
    return pl.pallas_call(
        functools.partial(_moba_prompt_kernel, nb=nb),
        grid=(b, B_HEADS // heads, nb),
        in_specs=[pl.BlockSpec((1, MOBA_BLOCK, g * _PAIR_LANES), lambda bb, hp, i: (bb, i, hp)),
                  pl.BlockSpec((1, t, g * _K_LANES), lambda bb, hp, i: (bb, 0, hp)),
                  pl.BlockSpec((1, nb, heads * _V_ROWS, MOBA_BLOCK),
                               lambda bb, hp, i: (bb, 0, hp, 0)),
                  pl.BlockSpec((1, nb, g * _PAIR_LANES), lambda bb, hp, i: (bb, 0, hp)),
                  pl.BlockSpec((heads, _N_TABLES, MOBA_BLOCK, MOBA_BLOCK),
                               lambda bb, hp, i: (hp, 0, 0, 0), pipeline_mode=pl.Buffered(1))],
        out_specs=pl.BlockSpec((1, MOBA_BLOCK, g * _PAIR_LANES), lambda bb, hp, i: (bb, i, hp)),
        out_shape=jax.ShapeDtypeStruct((b, t, B_WIDTH), _BF16),
        scratch_shapes=[pltpu.VMEM((heads, nb + _BLOCKS_PER_ITER - 1, MOBA_BLOCK, MOBA_BLOCK), _F32)],
        compiler_params=pltpu.CompilerParams(
            dimension_semantics=("arbitrary", "arbitrary", "arbitrary"),
            vmem_limit_bytes=_VMEM_LIMIT),
        name="moba_prompt",
    )(q3, kb3, vt4, kmean3, btab)


def _moba_sample_kernel(pt_ref, qbd_ref, knew_ref, vnew_ref, bnear_ref, bfar_ref,
                        bownt_ref, hmaskt_ref, rsum_ref, *rest, nb_past):
    del pt_ref
    pages = _PAGES_PER_STEP
    k_pages = rest[:pages]
    v_pages = rest[pages:2 * pages]
    o_ref = rest[2 * pages]
    gate_scr, m_scr, l_scr, acc_scr, st_scr, qbdt_scr = rest[2 * pages + 1:]
    c = pl.program_id(1)
    rows = qbd_ref.shape[1]
    tdec = o_ref.shape[1]
    blocks_per_step = pages // _PAGES_PER_BLOCK

    def lanes_t(x):
        return jnp.concatenate([x, jnp.zeros((128 - x.shape[0], x.shape[1]), x.dtype)], axis=0).T

    qbd = qbd_ref[0]
    qz = (qbd * _SCALE).astype(_BF16)

    @pl.when(c == 0)
    def _():
        qbdt_scr[...] = lanes_t(qbd)

    qbdt = qbdt_scr[...]
    row_pad = jnp.zeros((128 - rows, MOBA_BLOCK), _F32)

    for bl in range(blocks_per_step):
        jg = c * blocks_per_step + bl
        kts = [k_pages[bl * _PAGES_PER_BLOCK + p][0] for p in range(_PAGES_PER_BLOCK)]
        kt = jnp.concatenate(kts, axis=1)
        kmean = jnp.sum(functools.reduce(lambda a, b: a + b, kts), axis=-1,
                        keepdims=True) * (1.0 / MOBA_BLOCK)
        gate_scr[pl.ds(jg, 1), :] = jnp.sum(qbdt * kmean, axis=0, keepdims=True)
        s = jnp.dot(qz, kt.astype(_BF16), preferred_element_type=_F32)
        s = s + jnp.where(jg == nb_past - 1, bnear_ref[...], bfar_ref[:, 0:1])
        st_scr[bl] = jnp.concatenate([s, row_pad], axis=0).T
    for bl in range(blocks_per_step):
        jg = c * blocks_per_step + bl
        vt = jnp.concatenate([v_pages[bl * _PAGES_PER_BLOCK + p][0]
                              for p in range(_PAGES_PER_BLOCK)], axis=1)
        st = st_scr[bl]
        m = jnp.max(st, axis=0, keepdims=True)
        p = jnp.exp(st - m)
        m_scr[pl.ds(jg, 1), :] = m
        l_scr[pl.ds(jg, 1), :] = jnp.sum(p, axis=0, keepdims=True)
        acc_scr[jg] = jnp.dot(vt.astype(_BF16), p.astype(_BF16), preferred_element_type=_F32)

    @pl.when(c == pl.num_programs(1) - 1)
    def _():
        sel = _topk_block_mask(gate_scr[...], nb_past, nb_past, 128) > 0.5
        m_blk = m_scr[...]
        s_own = jnp.dot(knew_ref[0], qbdt * _SCALE, preferred_element_type=_F32) + bownt_ref[...]
        m_tot = jnp.maximum(jnp.max(s_own, axis=0, keepdims=True),
                            jnp.max(jnp.where(sel, m_blk, _NEG), axis=0, keepdims=True))
        w = jnp.where(sel, jnp.exp(m_blk - m_tot), 0.0)
        p_own = jnp.exp(s_own - m_tot)
        l_tot = jnp.sum(p_own, axis=0, keepdims=True) + jnp.sum(w * l_scr[...], axis=0, keepdims=True)
        p_own = jnp.concatenate([p_own, jnp.zeros((128 - p_own.shape[0], 128), _F32)], axis=0)
        accs = [jnp.dot(lanes_t(vnew_ref[0]), p_own, preferred_element_type=_F32)]
        accs += [jnp.zeros_like(accs[0]) for _ in range(3)]
        for j in range(nb_past):
            accs[j % 4] = accs[j % 4] + w[j:j + 1, :] * acc_scr[j]
        out_t = ((accs[0] + accs[1]) + (accs[2] + accs[3])) / l_tot
        folded = lax.dot_general(rsum_ref[...], out_t * hmaskt_ref[...], (((1,), (1,)), ((), ())),
                                 preferred_element_type=_F32, precision=lax.Precision.HIGHEST)
        o_ref[0] = folded[:tdec]


def _moba_sample_call(page_flat, qbd, knew, vnew, bnear, bfar, bownt, hmaskt, rsum,
                      cache_kt, cache_vt, *, layer, n_pool, n_pages, tdec):
    nbatch, rows, _ = qbd.shape
    pages = _PAGES_PER_STEP
    nb_past = n_pages // _PAGES_PER_BLOCK
    assert nb_past % 8 == 0 and rows <= 128
    base = layer * n_pool

    def page_spec(p):
        return pl.BlockSpec((1, B_WIDTH, CHUNK),
                            lambda bb, cc, pt, p=p: (base + pt[bb * n_pages + cc * pages + p], 0, 0))

    per_batch = lambda arr: pl.BlockSpec((1,) + arr.shape[1:], lambda bb, cc, pt: (bb, 0, 0))
    const = lambda arr: pl.BlockSpec(arr.shape, lambda bb, cc, pt: (0,) * arr.ndim)
    grid_spec = pltpu.PrefetchScalarGridSpec(
        num_scalar_prefetch=1,
        grid=(nbatch, n_pages // pages),
        in_specs=[per_batch(qbd), per_batch(knew), per_batch(vnew),
                  const(bnear), const(bfar), const(bownt), const(hmaskt), const(rsum)]
                 + [page_spec(p) for p in range(pages)] * 2,
        out_specs=pl.BlockSpec((1, tdec, B_WIDTH), lambda bb, cc, pt: (bb, 0, 0)),
        scratch_shapes=[pltpu.VMEM((nb_past, 128), _F32),
                        pltpu.VMEM((nb_past, 128), _F32),
                        pltpu.VMEM((nb_past, 128), _F32),
                        pltpu.VMEM((nb_past, B_WIDTH, 128), _F32),
                        pltpu.VMEM((pages // _PAGES_PER_BLOCK, MOBA_BLOCK, 128), _F32),
                        pltpu.VMEM((B_WIDTH, 128), _F32)])
    return pl.pallas_call(
        functools.partial(_moba_sample_kernel, nb_past=nb_past),
        grid_spec=grid_spec,
        out_shape=jax.ShapeDtypeStruct((nbatch, tdec, B_WIDTH), _F32),
        compiler_params=pltpu.CompilerParams(dimension_semantics=("arbitrary", "arbitrary"),
                                             vmem_limit_bytes=_VMEM_LIMIT),
        name="moba_sample",
    )(page_flat, qbd, knew, vnew, bnear, bfar, bownt, hmaskt, rsum,
      *([cache_kt] * pages), *([cache_vt] * pages))


def _post_kernel(x_ref, ya_ref, yb_ref, sga_ref, sgb_ref, wpa_ref, wpb_ref, wo_ref, g2_ref,
                 wup_ref, cw_ref, cb_ref, wdn_ref, gf_ref, *rest, tm, tiles_per_seq, per_row_state,
                 final_norm):
    if per_row_state:
        s1_ref, s2_ref, tpos_ref = rest[:3]
        rest = rest[3:]
    else:
        st_ref = rest[0]
        rest = rest[1:]
    xo_ref, a_ref = rest[:2]
    rest = rest[2:]
    if final_norm:
        y_ref = rest[0]
        rest = rest[1:]
    if not per_row_state:
        carry_scr = rest[0]

    ma = jnp.dot(ya_ref[...], wpa_ref[...], preferred_element_type=_F32)
    mb = jnp.dot(yb_ref[...], wpb_ref[...], preferred_element_type=_F32)
    merged = sga_ref[...].astype(_F32) * ma + sgb_ref[...].astype(_F32) * mb
    x1 = x_ref[...] + jnp.dot(merged.astype(_BF16), wo_ref[...], preferred_element_type=_F32)

    xn = _rms(x1, g2_ref[...]).astype(_BF16)
    if not per_row_state:
        @pl.when(pl.program_id(0) % tiles_per_seq == 0)
        def _():
            carry_scr[...] = st_ref[0]

    x2 = x1
    for lo, hi in zip(_FFN_SPLITS[:-1], _FFN_SPLITS[1:]):
        cols = slice(lo, hi)
        width = hi - lo
        a = jnp.dot(xn, wup_ref[:, cols], preferred_element_type=_F32)
        gate = jnp.dot(xn, wup_ref[:, D_FF + lo:D_FF + hi], preferred_element_type=_F32)
        prev1 = pltpu.roll(a, 1, 0)
        prev2 = pltpu.roll(a, 2, 0)
        if per_row_state:
            tpos = tpos_ref[:, cols]
            prev1 = jnp.where(tpos >= 1, prev1, s1_ref[:, cols])
            prev2 = jnp.where(tpos >= 2, prev2, s2_ref[:, cols])
            a_ref[:, cols] = a
        else:
            row = lax.broadcasted_iota(jnp.int32, (8, width), 0)
            c0 = carry_scr[0:1, cols]
            c1 = carry_scr[1:2, cols]
            top1 = jnp.where(row == 0, c1, prev1[:8])
            top2 = jnp.where(row == 0, c0, jnp.where(row == 1, c1, prev2[:8]))
            prev1 = jnp.concatenate([top1, prev1[8:]], axis=0)
            prev2 = jnp.concatenate([top2, prev2[8:]], axis=0)
            tail = a[tm - (CONV_W - 1):, :]
            carry_scr[:, cols] = tail
            a_ref[0, :, cols] = tail
        conv = (cb_ref[:, cols] + prev2 * cw_ref[0:1, cols] + prev1 * cw_ref[1:2, cols]
                + a * cw_ref[2:3, cols])
        act = (_gelu(conv) * gate).astype(_BF16)
        x2 = x2 + jnp.dot(act, wdn_ref[cols, :], preferred_element_type=_F32)
    xo_ref[...] = x2
    if final_norm:
        y_ref[...] = _rms(x2, gf_ref[...])


def _post_call(x2, ya, yb, sga, sgb, wpa, wpb, wo, g2, wup, cw, cb, wdn, gf, state_args, *,
               layer, tm, tiles_per_seq, per_row_state, final_norm):
    n = x2.shape[0]
    row = lambda width: pl.BlockSpec((tm, width), lambda i: (i, 0))
    in_specs = [row(D_MODEL), row(A_WIDTH), row(B_WIDTH), row(D_MODEL), row(D_MODEL),
                _layer_spec((A_WIDTH, D_MODEL), layer), _layer_spec((B_WIDTH, D_MODEL), layer),
                _layer_spec((D_MODEL, D_MODEL), layer), _const_spec((1, D_MODEL)),
                _layer_spec((D_MODEL, 2 * D_FF), layer), _const_spec((CONV_W, D_FF)),
                _const_spec((1, D_FF)), _layer_spec((D_FF, D_MODEL), layer),
                _const_spec((1, D_MODEL))]
    out_shape = [jax.ShapeDtypeStruct((n, D_MODEL), _F32)]
    out_specs = [row(D_MODEL)]
    scratch = []
    if per_row_state:
        in_specs += [row(D_FF), row(D_FF), row(D_FF)]
        out_shape.append(jax.ShapeDtypeStruct((n, D_FF), _F32))
        out_specs.append(row(D_FF))
    else:
        nseq = n // (tm * tiles_per_seq)
        in_specs.append(pl.BlockSpec((1, CONV_W - 1, D_FF), lambda i: (i // tiles_per_seq, 0, 0)))
        out_shape.append(jax.ShapeDtypeStruct((nseq, CONV_W - 1, D_FF), _F32))
        out_specs.append(pl.BlockSpec((1, CONV_W - 1, D_FF), lambda i: (i // tiles_per_seq, 0, 0)))
        scratch.append(pltpu.VMEM((CONV_W - 1, D_FF), _F32))
    if final_norm:
        out_shape.append(jax.ShapeDtypeStruct((n, D_MODEL), _F32))
        out_specs.append(row(D_MODEL))
    return pl.pallas_call(
        functools.partial(_post_kernel, tm=tm, tiles_per_seq=tiles_per_seq,
                          per_row_state=per_row_state, final_norm=final_norm),
        grid=(n // tm,),
        in_specs=in_specs,
        out_specs=out_specs,
        out_shape=out_shape,
        scratch_shapes=scratch,
        compiler_params=pltpu.CompilerParams(dimension_semantics=("arbitrary",),
                                             vmem_limit_bytes=_VMEM_LIMIT),
        name="merge_convffn",
    )(x2, ya, yb, sga, sgb, wpa, wpb, wo, g2, wup, cw, cb, wdn, gf, *state_args)


def _t5_bucket(rel):
    max_exact = N_BUCKETS // 2
    relf = jnp.maximum(rel, 1).astype(_F32)
    large = max_exact + (jnp.log(relf / max_exact) / math.log(MAX_DISTANCE / max_exact)
                         * (N_BUCKETS - max_exact)).astype(jnp.int32)
    large = jnp.minimum(large, N_BUCKETS - 1)
    return jnp.where(rel < max_exact, rel, large)


def _far_bucket_is_last(min_rel):
    max_exact = N_BUCKETS // 2
    v = max_exact + int(math.log(min_rel / max_exact) / math.log(MAX_DISTANCE / max_exact)
                        * (N_BUCKETS - max_exact) - 1e-6)
    return v >= N_BUCKETS - 1


def _toeplitz(diag, n):
    width = 2 * n
    flat = jnp.tile(diag, (1,) * (diag.ndim - 1) + (n,))
    skew = flat[..., :n * (width - 1)].reshape(diag.shape[:-1] + (n, width - 1))
    return skew[..., n - 1:2 * n - 1]


def _prompt_bias_tables(rel_bias):
    assert _far_bucket_is_last(MOBA_BLOCK + 1)
    bt = rel_bias.T.astype(_F32)
    bt = (bt - bt[:, N_BUCKETS - 1:]) * _LOG2E
    rel = jnp.arange(2 * MOBA_BLOCK, dtype=jnp.int32) - (MOBA_BLOCK - 1)
    own = jnp.where(rel >= 0, bt[:, _t5_bucket(jnp.maximum(rel, 0))], _NEG)
    prev = bt[:, _t5_bucket(rel + MOBA_BLOCK)]
    own, prev = _toeplitz(own, MOBA_BLOCK), _toeplitz(prev, MOBA_BLOCK)
    tables = [None] * _N_TABLES
    tables[_TAB_OWN], tables[_TAB_PREV] = own, prev
    tables[_TAB_FAR], tables[_TAB_DEAD] = jnp.zeros_like(own), jnp.full_like(own, _NEG)
    return jnp.stack(tables, axis=1)


def _sample_bias_tables(rel_bias, tdec, tpad):
    assert _far_bucket_is_last(MOBA_BLOCK + 1)
    bt = rel_bias.T.astype(_F32)
    tok = jnp.arange(tdec, dtype=jnp.int32)[:, None]
    key = jnp.arange(MOBA_BLOCK, dtype=jnp.int32)[None, :]
    near = bt[:, _t5_bucket(MOBA_BLOCK + tok - key)]
    far = jnp.broadcast_to(bt[:, N_BUCKETS - 1][:, None, None], (B_HEADS, tdec, 128))
    new = jnp.arange(tpad, dtype=jnp.int32)[None, :]
    own = jnp.where((tok - new >= 0) & (new < tdec), bt[:, _t5_bucket(jnp.maximum(tok - new, 0))],
                    _NEG)
    flat = lambda a: a.reshape(B_HEADS * tdec, a.shape[-1])
    return flat(near), flat(far), flat(own)


def kernel(x_prompt, x_sample, cache_k, cache_v, state_conv, page_table, w_in, ln_v_g, ln_v_b, w_s,
           b_s, w_pa, w_pb, w_o, norm1_g, norm2_g, w_up, conv_w, conv_b, w_down, rel_bias, norm_f):
    nbp_, seq, _ = x_prompt.shape
    nbs, tdec, _ = x_sample.shape
    depth, n_pool, page_size = cache_k.shape[:3]
    n_pages = page_table.shape[1]
    ns = nbs * tdec
    np_ = nbp_ * seq
    assert page_size == CHUNK and n_pages % _PAGES_PER_STEP == 0 and seq % MOBA_BLOCK == 0
    assert _PAGES_PER_STEP % _PAGES_PER_BLOCK == 0 and seq % _POST_TILE == 0
    assert CONV_W - 1 <= tdec <= CHUNK and ns % CHUNK == 0 and CHUNK % tdec == 0
    tpad = -(-tdec // 8) * 8
    tm_p = _PROJ_TILE
    nb = seq // MOBA_BLOCK
    nb_past = n_pages // _PAGES_PER_BLOCK
    rows = B_HEADS * tdec

    xp = x_prompt.reshape(np_, D_MODEL)
    xs = x_sample.reshape(ns, D_MODEL)
    cache_kt = cache_k.transpose(0, 1, 3, 4, 2).reshape(depth * n_pool, B_WIDTH, page_size)
    cache_vt = cache_v.transpose(0, 1, 3, 4, 2).reshape(depth * n_pool, B_WIDTH, page_size)
    page_flat = page_table.reshape(-1).astype(jnp.int32)

    btab = _prompt_bias_tables(rel_bias)
    bnear_s, bfar_s, bown_s = _sample_bias_tables(rel_bias, tdec, tpad)
    row_head = jnp.arange(rows, dtype=jnp.int32)[:, None] // tdec
    hmask = (jnp.arange(B_WIDTH, dtype=jnp.int32)[None, :] // B_HEAD_DIM == row_head).astype(_F32)
    rsum = (jnp.arange(rows, dtype=jnp.int32)[None, :] % tdec
            == jnp.arange(tdec, dtype=jnp.int32)[:, None]).astype(_F32)
    lane_pad = lambda a: jnp.pad(a, ((0, 0), (0, 128 - rows)))
    bownt_s, hmaskt = lane_pad(bown_s.T), lane_pad(hmask.T)
    rsum_p = lane_pad(jnp.pad(rsum, ((0, tpad - tdec), (0, 0))))
    tpos = jnp.broadcast_to((jnp.arange(ns, dtype=jnp.int32) % tdec)[:, None], (ns, D_FF))
    tril = jnp.tril(jnp.ones((CHUNK, CHUNK), bool))
    idx = jnp.arange(CHUNK, dtype=jnp.int32)
    same_chunk = tril & (idx[:, None] // tdec == idx[None, :] // tdec)
    zero_state = jnp.zeros((nbp_, CONV_W - 1, D_FF), _F32)
    w_in_b, wpa, wpb, wo = (w.astype(_BF16) for w in (w_in, w_pa, w_pb, w_o))
    wup, wdn = w_up.astype(_BF16), w_down.astype(_BF16)
    gf = norm_f[None, :]

    ks_l, vs_l, as_l, cp_l, cs_l = [], [], [], [], []
    yp = ys = kv_nat = None
    for l in range(depth):
        last = l == depth - 1
        g1 = norm1_g[l][None, :]
        lng, lnb = ln_v_g[l][None, :], ln_v_b[l][None, :]
        g2 = norm2_g[l][None, :]
        cw, cb = conv_w[l], conv_b[l][None, :]
        wmix_p = jnp.where(tril[None], w_s[l], 0.0).astype(_BF16)
        bmix_p = jnp.broadcast_to(b_s[l][:, :, None], (A_GROUPS, CHUNK, CHUNK))
        reps = CHUNK // tdec
        wmix_s = jnp.where(same_chunk[None], jnp.tile(w_s[l][:, :tdec, :tdec], (1, reps, reps)),
                           0.0).astype(_BF16)
        bmix_s = jnp.broadcast_to(jnp.tile(b_s[l][:, :tdec], (1, reps))[:, :, None],
                                  (A_GROUPS, CHUNK, CHUNK))

        (qp, kbp, vtp, kmeanp, knat, vnat, yap, sgap, sgbp) = _proj_call(
            xp, g1, w_in_b, lng, lnb, wmix_p, bmix_p, tm=tm_p, prompt=True, layer=l, depth=depth,
            seq=seq, kv_prev=kv_nat)
        kv_nat = (knat, vnat)
        ybp = _moba_prompt_call(qp.reshape(nbp_, seq, B_WIDTH), kbp.reshape(nbp_, seq, -1),
                                vtp, kmeanp.reshape(nbp_, nb, B_WIDTH), btab)
        outs = _post_call(xp, yap, ybp.reshape(np_, B_WIDTH), sgap, sgbp, wpa, wpb, wo, g2,
                          wup, cw, cb, wdn, gf, (zero_state,), layer=l, tm=_POST_TILE,
                          tiles_per_seq=seq // _POST_TILE, per_row_state=False, final_norm=last)
        xp, cp = outs[0], outs[1]
        if last:
            yp = outs[2]

        qs, ks, vs, yas, sgas, sgbs, vas = _proj_call(xs, g1, w_in_b, lng, lnb, wmix_s, bmix_s,
                                                      tm=CHUNK, prompt=False, layer=l)
        ks4 = ks.reshape(nbs, tdec, B_HEADS, B_HEAD_DIM)
        vs4 = vs.reshape(nbs, tdec, B_HEADS, B_HEAD_DIM)
        qbd = (qs.reshape(nbs, 1, tdec, B_WIDTH) * hmask.reshape(B_HEADS, tdec, B_WIDTH)[None]
               ).reshape(nbs, rows, B_WIDTH)
        pad = ((0, 0), (0, tpad - tdec), (0, 0))
        knew = jnp.pad(ks.reshape(nbs, tdec, B_WIDTH), pad)
        vnew = jnp.pad(vs.reshape(nbs, tdec, B_WIDTH), pad)
        ybs = _moba_sample_call(page_flat, qbd, knew, vnew, bnear_s, bfar_s, bownt_s, hmaskt,
                                rsum_p, cache_kt, cache_vt, layer=l, n_pool=n_pool, n_pages=n_pages,
                                tdec=tdec)
        ybs = ybs.reshape(ns, B_WIDTH)
        st = state_conv[l]
        s1 = jnp.broadcast_to(st[:, 1:2, :], (nbs, tdec, D_FF)).reshape(ns, D_FF)
        s2 = jnp.concatenate([st, jnp.zeros((nbs, tdec - (CONV_W - 1), D_FF), _F32)],
                             axis=1).reshape(ns, D_FF)
        outs = _post_call(xs, yas, ybs.astype(_BF16), sgas, sgbs, wpa, wpb, wo,
                          g2, wup, cw, cb, wdn, gf, (s1, s2, tpos), layer=l, tm=CHUNK,
                          tiles_per_seq=1, per_row_state=True, final_norm=last)
        xs, a_s = outs[0], outs[1]
        if last:
            ys = outs[2]

        ks_l.append(ks4)
        vs_l.append(vs4)
        as_l.append(vas.reshape(nbs, tdec, A_WIDTH))
        cp_l.append(cp)
        a_ext = jnp.concatenate([st, a_s.reshape(nbs, tdec, D_FF)], axis=1)
        cs_l.append(a_ext[:, -(CONV_W - 1):])

    def kv_out(t):
        return t.reshape(depth, nbp_, B_HEADS, B_HEAD_DIM, seq).transpose(0, 1, 4, 2, 3)

    return (yp.reshape(nbp_, seq, D_MODEL), ys.reshape(nbs, tdec, D_MODEL),
            kv_out(kv_nat[0]), kv_out(kv_nat[1]),
            jnp.stack(ks_l, 0), jnp.stack(vs_l, 0),
            jnp.stack(as_l, 0), jnp.stack(cp_l, 0), jnp.stack(cs_l, 0))
```

```python
import functools
import math

import jax
import jax.numpy as jnp
from jax import lax
from jax.experimental import pallas as pl
from jax.experimental.pallas import tpu as pltpu

D_MODEL = 1024
CHUNK = 128
A_WIDTH = D_MODEL // 2
A_GROUPS = 4
A_GROUP_DIM = A_WIDTH // A_GROUPS
B_HEADS = 8
B_HEAD_DIM = 64
B_WIDTH = B_HEADS * B_HEAD_DIM
MOBA_BLOCK = 256
MOBA_TOPK = 3
N_BUCKETS = 32
MAX_DISTANCE = 128
D_FF = 2816
CONV_W = 3
EPS = 1e-6
N_IN = 2 * A_WIDTH + 3 * B_WIDTH + 2 * D_MODEL

_C_U, _C_VA, _C_Q, _C_K, _C_V, _C_GA, _C_GB = 0, 512, 1024, 1536, 2048, 2560, 3584

_NEG = -1e30
_HEADS_PER_PAIR = 2
_PAIRS_PER_STEP = 2
_BLOCKS_PER_ITER = 4
_TAB_OWN, _TAB_PREV, _TAB_FAR, _TAB_DEAD, _N_TABLES = 0, 1, 2, 3, 4
_PAIR_LANES = _HEADS_PER_PAIR * B_HEAD_DIM
_K_LANES = 2 * _PAIR_LANES
_V_ROWS = B_HEAD_DIM + 16
_LOG2E = math.log2(math.e)
_PAGES_PER_STEP = 16
_PAGES_PER_BLOCK = MOBA_BLOCK // CHUNK
_PROJ_TILE = 512
_POST_TILE = 512
_FFN_SPLITS = (0, 1536, D_FF)
_VMEM_LIMIT = 48 * 1024 * 1024
_SCALE = 1.0 / math.sqrt(B_HEAD_DIM)

_F32 = jnp.float32
_BF16 = jnp.bfloat16


def _gelu(x):
    k = -2.0 * 0.7978845608028654 * _LOG2E
    return x / (1.0 + jnp.exp2(x * (k + (k * 0.044715) * (x * x))))


def _sigmoid(x):
    return 1.0 / (1.0 + jnp.exp2(x * -_LOG2E))


def _rms(x, g):
    return x * lax.rsqrt(jnp.mean(x * x, axis=-1, keepdims=True) + EPS) * g


def _const_spec(shape):
    zeros = (0,) * len(shape)
    return pl.BlockSpec(shape, lambda *_: zeros, pipeline_mode=pl.Buffered(1))


def _layer_spec(shape, layer):
    index = (layer,) + (0,) * len(shape)
    return pl.BlockSpec((None,) + shape, lambda *_: index, pipeline_mode=pl.Buffered(1))


def _proj_kernel(x_ref, g1_ref, w_ref, lng_ref, lnb_ref, wmix_ref, bmix_ref, *rest, tm, prompt,
                 nb=None):
    if prompt:
        (q_ref, kb_ref, vt_ref, kmean_ref, knat_ref, vnat_ref, ya_ref, sga_ref, sgb_ref) = rest[-9:]
    else:
        (q_ref, k_ref, v_ref, ya_ref, sga_ref, sgb_ref, va_ref) = rest
    xn = _rms(x_ref[...], g1_ref[...]).astype(_BF16)

    def proj(lo, width):
        return jnp.dot(xn, w_ref[:, lo:lo + width], preferred_element_type=_F32)

    u = _gelu(proj(_C_U, A_WIDTH))
    va = _gelu(proj(_C_VA, A_WIDTH))
    mu = jnp.mean(va, axis=-1, keepdims=True)
    vc = va - mu
    var = jnp.mean(vc * vc, axis=-1, keepdims=True)
    va = vc * lax.rsqrt(var + EPS) * lng_ref[...] + lnb_ref[...]
    if not prompt:
        va_ref[...] = va
    for c in range(tm // CHUNK):
        rows = slice(c * CHUNK, (c + 1) * CHUNK)
        for g in range(A_GROUPS):
            cols = slice(g * A_GROUP_DIM, (g + 1) * A_GROUP_DIM)
            mixed = jnp.dot(wmix_ref[g], va[rows, cols].astype(_BF16),
                            preferred_element_type=_F32) + bmix_ref[g]
            ya_ref[rows, cols] = (u[rows, cols] * mixed).astype(_BF16)

    q_ref[...] = proj(_C_Q, B_WIDTH)
    k = proj(_C_K, B_WIDTH)
    v = proj(_C_V, B_WIDTH)
    if prompt:
        blocks = tm // MOBA_BLOCK
        vt = v.T
        knat_ref[0] = k.T
        vnat_ref[0] = vt
        for r in range(blocks):
            kmean_ref[0, r:r + 1, :] = jnp.mean(k[r * MOBA_BLOCK:(r + 1) * MOBA_BLOCK], axis=0,
                                                keepdims=True)
        kb = k.astype(_BF16)
        first_blk = (pl.program_id(0) * blocks) % nb
        shape = (tm, _K_LANES - _PAIR_LANES)
        blk_of_row = first_blk + lax.broadcasted_iota(jnp.int32, shape, 0) // MOBA_BLOCK
        onehot = jnp.where(lax.broadcasted_iota(jnp.int32, shape, 1) == blk_of_row, 1.0, 0.0
                           ).astype(_BF16)
        for hp in range(B_HEADS // _HEADS_PER_PAIR):
            kb_ref[:, hp * _K_LANES:hp * _K_LANES + _PAIR_LANES] = (
                kb[:, hp * _PAIR_LANES:(hp + 1) * _PAIR_LANES])
            kb_ref[:, hp * _K_LANES + _PAIR_LANES:(hp + 1) * _K_LANES] = onehot
        vtb = vt.astype(_BF16)
        row = lax.broadcasted_iota(jnp.int32, (_V_ROWS - B_HEAD_DIM, MOBA_BLOCK), 0)
        ones_row = jnp.where(row == 0, 1.0, 0.0).astype(_BF16)
        for r in range(blocks):
            keys = slice(r * MOBA_BLOCK, (r + 1) * MOBA_BLOCK)
            for h in range(B_HEADS):
                vt_ref[0, r, h * _V_ROWS:h * _V_ROWS + B_HEAD_DIM, :] = (
                    vtb[h * B_HEAD_DIM:(h + 1) * B_HEAD_DIM, keys])
                vt_ref[0, r, h * _V_ROWS + B_HEAD_DIM:(h + 1) * _V_ROWS, :] = ones_row
    else:
        k_ref[...] = k
        v_ref[...] = v
    sga_ref[...] = _sigmoid(proj(_C_GA, D_MODEL)).astype(_BF16)
    sgb_ref[...] = _sigmoid(proj(_C_GB, D_MODEL)).astype(_BF16)


def _proj_call(x2, g1, w_in, lng, lnb, wmix, bmix, *, tm, prompt, layer=0, depth=1, seq=None,
               kv_prev=None):
    n = x2.shape[0]
    nt = n // tm
    row = lambda width: pl.BlockSpec((tm, width), lambda i: (i, 0))
    in_specs = [row(D_MODEL), _const_spec((1, D_MODEL)), _layer_spec((D_MODEL, N_IN), layer),
                _const_spec((1, A_WIDTH)), _const_spec((1, A_WIDTH)),
                _const_spec((A_GROUPS, CHUNK, CHUNK)), _const_spec((A_GROUPS, CHUNK, CHUNK))]
    args = [x2, g1, w_in, lng, lnb, wmix, bmix]
    aliases = {}
    tail_shape = [jax.ShapeDtypeStruct((n, A_WIDTH), _BF16),
                  jax.ShapeDtypeStruct((n, D_MODEL), _BF16),
                  jax.ShapeDtypeStruct((n, D_MODEL), _BF16)]
    tail_specs = [row(A_WIDTH), row(D_MODEL), row(D_MODEL)]
    nb = None
    if prompt:
        nb = seq // MOBA_BLOCK
        blocks = tm // MOBA_BLOCK
        tps = seq // tm
        assert tm % MOBA_BLOCK == 0 and seq % tm == 0 and nb <= _K_LANES - _PAIR_LANES
        nseq = n // seq
        k_cols = B_HEADS // _HEADS_PER_PAIR * _K_LANES
        nat = jax.ShapeDtypeStruct((depth * nseq, B_WIDTH, seq), _F32)
        nat_spec = pl.BlockSpec((1, B_WIDTH, tm), lambda i: (layer * nseq + i // tps, 0, i % tps))
        out_shape = [jax.ShapeDtypeStruct((n, B_WIDTH), _F32),
                     jax.ShapeDtypeStruct((n, k_cols), _BF16),
                     jax.ShapeDtypeStruct((nseq, nb, B_HEADS * _V_ROWS, MOBA_BLOCK), _BF16),
                     jax.ShapeDtypeStruct((nt, blocks, B_WIDTH), _F32), nat, nat] + tail_shape
        out_specs = [row(B_WIDTH), row(k_cols),
                     pl.BlockSpec((1, blocks, B_HEADS * _V_ROWS, MOBA_BLOCK),
                                  lambda i: (i // tps, i % tps, 0, 0)),
                     pl.BlockSpec((1, blocks, B_WIDTH), lambda i: (i, 0, 0)), nat_spec, nat_spec
                     ] + tail_specs
        if kv_prev is not None:
            in_specs += [pl.BlockSpec(memory_space=pl.ANY)] * 2
            args += list(kv_prev)
            aliases = {7: 4, 8: 5}
    else:
        out_shape = [jax.ShapeDtypeStruct((n, B_WIDTH), _F32)] * 3 + tail_shape + [
            jax.ShapeDtypeStruct((n, A_WIDTH), _F32)]
        out_specs = [row(B_WIDTH)] * 3 + tail_specs + [row(A_WIDTH)]
    return pl.pallas_call(
        functools.partial(_proj_kernel, tm=tm, prompt=prompt, nb=nb),
        grid=(nt,),
        in_specs=in_specs,
        out_specs=out_specs,
        out_shape=out_shape,
        input_output_aliases=aliases,
        compiler_params=pltpu.CompilerParams(dimension_semantics=("arbitrary",),
                                             vmem_limit_bytes=_VMEM_LIMIT),
        name="proj_mixer_a",
    )(*args)


def _topk_block_mask(gate, own, nb, width):
    blk = lax.broadcasted_iota(jnp.int32, (nb, width), 0)
    gm = jnp.where(blk < own, gate, -jnp.inf)
    rank = jnp.zeros((nb, width), _F32)
    for jp in range(nb):
        r = gm[jp:jp + 1, :]
        beats = jnp.where(r > gm, 1.0, jnp.where(r == gm, jnp.where(blk > jp, 1.0, 0.0), 0.0))
        rank = rank + beats
    past_sel = jnp.where(blk < own, jnp.where(rank < MOBA_TOPK, 1.0, 0.0), 0.0)
    return jnp.where(blk == own, 1.0, past_sel)


def _moba_prompt_kernel(q_ref, k_ref, vt_ref, kmean_ref, btab_ref, o_ref, s_scr, *, nb):
    i = pl.program_id(2)
    blk = MOBA_BLOCK
    heads = range(_PAIRS_PER_STEP * _HEADS_PER_PAIR)

    qt = q_ref[0].T
    feat = lax.broadcasted_iota(jnp.int32, (_PAIR_LANES, blk), 0)
    qz = []
    for hh in heads:
        pair, sub = divmod(hh, _HEADS_PER_PAIR)
        in_head = (feat >= sub * B_HEAD_DIM) & (feat < (sub + 1) * B_HEAD_DIM)
        qh = jnp.where(in_head, qt[pair * _PAIR_LANES:(pair + 1) * _PAIR_LANES], 0.0)
        gate = jnp.dot(kmean_ref[0, :, pair * _PAIR_LANES:(pair + 1) * _PAIR_LANES], qh,
                       preferred_element_type=_F32, precision=lax.Precision.HIGHEST)
        mask_rows = jnp.where(_topk_block_mask(gate, i, nb, blk) > 0.5, 0.0, _NEG)
        pad = jnp.zeros((_K_LANES - _PAIR_LANES - nb, blk), _F32)
        qz.append(jnp.concatenate([qh * (_SCALE * _LOG2E), mask_rows, pad], axis=0).astype(_BF16))

    n_iter = lax.div(i + _BLOCKS_PER_ITER, _BLOCKS_PER_ITER)

    def slots(it):
        js = [it * _BLOCKS_PER_ITER + u for u in range(_BLOCKS_PER_ITER)]
        tabs = [jnp.where(j == i, _TAB_OWN, jnp.where(j == i - 1, _TAB_PREV,
                                                      jnp.where(j > i, _TAB_DEAD, _TAB_FAR)))
                for j in js]
        return js, [jnp.minimum(j, i) for j in js], tabs

    def logits_pass(it, ms, group):
        js, jcs, tabs = slots(it)
        out = []
        for hh, m in zip(group, ms):
            pair = hh // _HEADS_PER_PAIR
            for j, jc, tab in zip(js, jcs, tabs):
                kj = k_ref[0, pl.ds(pl.multiple_of(jc * blk, blk), blk),
                           pair * _K_LANES:(pair + 1) * _K_LANES]
                s = jnp.dot(kj, qz[hh], preferred_element_type=_F32) + btab_ref[hh, tab]
                s_scr[hh, j] = s
                m = jnp.maximum(m, jnp.max(s, axis=0, keepdims=True))
            out.append(m)
        return tuple(out)

    def value_pass(it, accs, group, ms):
        js, jcs, _ = slots(it)
        out = []
        for hh, acc, m in zip(group, accs, ms):
            for j, jc in zip(js, jcs):
                p = jnp.exp2(s_scr[hh, j] - m).astype(_BF16)
                acc = acc + jnp.dot(vt_ref[0, jc, hh * _V_ROWS:(hh + 1) * _V_ROWS, :], p,
                                    preferred_element_type=_F32)
            out.append(acc)
        return tuple(out)

    group = tuple(heads)
    ms = lax.fori_loop(0, n_iter, lambda it, ms: logits_pass(it, ms, group),
                       tuple(jnp.full((1, blk), _NEG, _F32) for _ in group))
    accs = lax.fori_loop(0, n_iter, lambda it, accs: value_pass(it, accs, group, ms),
                         tuple(jnp.zeros((_V_ROWS, blk), _F32) for _ in group))
    outs = [acc[:B_HEAD_DIM] / acc[B_HEAD_DIM:B_HEAD_DIM + 1] for acc in accs]
    o_ref[0] = jnp.concatenate(outs, axis=0).T.astype(_BF16)


def _moba_prompt_call(q3, kb3, vt4, kmean3, btab):
    b, t, _ = q3.shape
    nb = t // MOBA_BLOCK
    g = _PAIRS_PER_STEP
    heads = g * _HEADS_PER_PAIR
    return pl.pallas_call(
        functools.partial(_moba_prompt_kernel, nb=nb),
        grid=(b, B_HEADS // heads, nb),
        in_specs=[pl.BlockSpec((1, MOBA_BLOCK, g * _PAIR_LANES), lambda bb, hp, i: (bb, i, hp)),
                  pl.BlockSpec((1, t, g * _K_LANES), lambda bb, hp, i: (bb, 0, hp)),
                  pl.BlockSpec((1, nb, heads * _V_ROWS, MOBA_BLOCK),
                               lambda bb, hp, i: (bb, 0, hp, 0)),
                  pl.BlockSpec((1, nb, g * _PAIR_LANES), lambda bb, hp, i: (bb, 0, hp)),
                  pl.BlockSpec((heads, _N_TABLES, MOBA_BLOCK, MOBA_BLOCK),
                               lambda bb, hp, i: (hp, 0, 0, 0), pipeline_mode=pl.Buffered(1))],
        out_specs=pl.BlockSpec((1, MOBA_BLOCK, g * _PAIR_LANES), lambda bb, hp, i: (bb, i, hp)),
        out_shape=jax.ShapeDtypeStruct((b, t, B_WIDTH), _BF16),
        scratch_shapes=[pltpu.VMEM((heads, nb + _BLOCKS_PER_ITER - 1, MOBA_BLOCK, MOBA_BLOCK), _F32)],
        compiler_params=pltpu.CompilerParams(
            dimension_semantics=("arbitrary", "arbitrary", "arbitrary"),
            vmem_limit_bytes=_VMEM_LIMIT),
        name="moba_prompt",
    )(q3, kb3, vt4, kmean3, btab)


def _moba_sample_kernel(pt_ref, qbd_ref, knew_ref, vnew_ref, bnear_ref, bfar_ref,
                        bownt_ref, hmaskt_ref, rsum_ref, ck_hbm, cv_hbm, o_ref,
                        gate_scr, m_scr, l_scr, acc_scr, st_scr, qbdt_scr, kbuf, vbuf, sems,
                        *, nb_past, page_base):
    pages = _PAGES_PER_STEP
    c = pl.program_id(1)
    step = pl.program_id(0) * pl.num_programs(1) + c
    last_step = pl.num_programs(0) * pl.num_programs(1) - 1
    slot = step % 2
    rows = qbd_ref.shape[1]
    tdec = o_ref.shape[1]
    blocks_per_step = pages // _PAGES_PER_BLOCK

    def page_copy(kind, of_step, into_slot, p):
        src, buf = ((ck_hbm, kbuf), (cv_hbm, vbuf))[kind]
        page = page_base + pt_ref[of_step * pages + p]
        return pltpu.make_async_copy(src.at[page], buf.at[into_slot, p], sems.at[into_slot, kind])

    def block_pages(bl):
        return range(bl * _PAGES_PER_BLOCK, (bl + 1) * _PAGES_PER_BLOCK)

    @pl.when(step == 0)
    def _():
        for kind in range(2):
            for p in range(pages):
                page_copy(kind, 0, 0, p).start()

    for kind in range(2):
        for p in range(pages):
            page_copy(kind, step, slot, p).wait()
    next_step = jnp.minimum(step + 1, last_step)

    def lanes_t(x):
        return jnp.concatenate([x, jnp.zeros((128 - x.shape[0], x.shape[1]), x.dtype)], axis=0).T

    qbd = qbd_ref[0]
    qz = (qbd * _SCALE).astype(_BF16)

    @pl.when(c == 0)
    def _():
        qbdt_scr[...] = lanes_t(qbd)

    qbdt = qbdt_scr[...]
    row_pad = jnp.zeros((128 - rows, MOBA_BLOCK), _F32)

    for bl in range(blocks_per_step):
        jg = c * blocks_per_step + bl
        kts = [kbuf[slot, p] for p in block_pages(bl)]
        kt = jnp.concatenate(kts, axis=1)
        kmean = jnp.sum(functools.reduce(lambda a, b: a + b, kts), axis=-1,
                        keepdims=True) * (1.0 / MOBA_BLOCK)
        gate_scr[pl.ds(jg, 1), :] = jnp.sum(qbdt * kmean, axis=0, keepdims=True)
        s = jnp.dot(qz, kt.astype(_BF16), preferred_element_type=_F32)
        s = s + jnp.where(jg == nb_past - 1, bnear_ref[...], bfar_ref[:, 0:1])
        st_scr[bl] = jnp.concatenate([s, row_pad], axis=0).T
        for pg in block_pages(bl):
            page_copy(0, next_step, 1 - slot, pg).start()
    for bl in range(blocks_per_step):
        jg = c * blocks_per_step + bl
        vt = jnp.concatenate([vbuf[slot, p] for p in block_pages(bl)], axis=1)
        st = st_scr[bl]
        m = jnp.max(st, axis=0, keepdims=True)
        p = jnp.exp(st - m)
        m_scr[pl.ds(jg, 1), :] = m
        l_scr[pl.ds(jg, 1), :] = jnp.sum(p, axis=0, keepdims=True)
        acc_scr[jg] = jnp.dot(vt.astype(_BF16), p.astype(_BF16), preferred_element_type=_F32)
        for pg in block_pages(bl):
            page_copy(1, next_step, 1 - slot, pg).start()

    @pl.when(step == last_step)
    def _():
        for kind in range(2):
            for p in range(pages):
                page_copy(kind, last_step, 1 - slot, p).wait()

    @pl.when(c == pl.num_programs(1) - 1)
    def _():
        sel = _topk_block_mask(gate_scr[...], nb_past, nb_past, 128) > 0.5
        m_blk = m_scr[...]
        s_own = jnp.dot(knew_ref[0], qbdt * _SCALE, preferred_element_type=_F32) + bownt_ref[...]
        m_tot = jnp.maximum(jnp.max(s_own, axis=0, keepdims=True),
                            jnp.max(jnp.where(sel, m_blk, _NEG), axis=0, keepdims=True))
        w = jnp.where(sel, jnp.exp(m_blk - m_tot), 0.0)
        p_own = jnp.exp(s_own - m_tot)
        l_tot = jnp.sum(p_own, axis=0, keepdims=True) + jnp.sum(w * l_scr[...], axis=0, keepdims=True)
        p_own = jnp.concatenate([p_own, jnp.zeros((128 - p_own.shape[0], 128), _F32)], axis=0)
        accs = [jnp.dot(lanes_t(vnew_ref[0]), p_own, preferred_element_type=_F32)]
        accs += [jnp.zeros_like(accs[0]) for _ in range(3)]
        for j in range(nb_past):
            accs[j % 4] = accs[j % 4] + w[j:j + 1, :] * acc_scr[j]
        out_t = ((accs[0] + accs[1]) + (accs[2] + accs[3])) / l_tot
        folded = lax.dot_general(rsum_ref[...], out_t * hmaskt_ref[...], (((1,), (1,)), ((), ())),
                                 preferred_element_type=_F32, precision=lax.Precision.HIGHEST)
        o_ref[0] = folded[:tdec]


def _moba_sample_call(page_flat, qbd, knew, vnew, bnear, bfar, bownt, hmaskt, rsum,
                      cache_kt, cache_vt, *, layer, n_pool, n_pages, tdec):
    nbatch, rows, _ = qbd.shape
    pages = _PAGES_PER_STEP
    nb_past = n_pages // _PAGES_PER_BLOCK
    assert nb_past % 8 == 0 and rows <= 128

    per_batch = lambda arr: pl.BlockSpec((1,) + arr.shape[1:], lambda bb, cc, pt: (bb, 0, 0))
    const = lambda arr: pl.BlockSpec(arr.shape, lambda bb, cc, pt: (0,) * arr.ndim)
    page_buf = pltpu.VMEM((2, pages, B_WIDTH, CHUNK), _F32)
    grid_spec = pltpu.PrefetchScalarGridSpec(
        num_scalar_prefetch=1,
        grid=(nbatch, n_pages // pages),
        in_specs=[per_batch(qbd), per_batch(knew), per_batch(vnew),
                  const(bnear), const(bfar), const(bownt), const(hmaskt), const(rsum),
                  pl.BlockSpec(memory_space=pl.ANY), pl.BlockSpec(memory_space=pl.ANY)],
        out_specs=pl.BlockSpec((1, tdec, B_WIDTH), lambda bb, cc, pt: (bb, 0, 0)),
        scratch_shapes=[pltpu.VMEM((nb_past, 128), _F32),
                        pltpu.VMEM((nb_past, 128), _F32),
                        pltpu.VMEM((nb_past, 128), _F32),
                        pltpu.VMEM((nb_past, B_WIDTH, 128), _F32),
                        pltpu.VMEM((pages // _PAGES_PER_BLOCK, MOBA_BLOCK, 128), _F32),
                        pltpu.VMEM((B_WIDTH, 128), _F32),
                        page_buf, page_buf,
                        pltpu.SemaphoreType.DMA((2, 2))])
    return pl.pallas_call(
        functools.partial(_moba_sample_kernel, nb_past=nb_past, page_base=layer * n_pool),
        grid_spec=grid_spec,
        out_shape=jax.ShapeDtypeStruct((nbatch, tdec, B_WIDTH), _F32),
        compiler_params=pltpu.CompilerParams(dimension_semantics=("arbitrary", "arbitrary"),
                                             vmem_limit_bytes=_VMEM_LIMIT),
        name="moba_sample",
    )(page_flat, qbd, knew, vnew, bnear, bfar, bownt, hmaskt, rsum, cache_kt, cache_vt)


def _post_kernel(x_ref, ya_ref, yb_ref, sga_ref, sgb_ref, wpa_ref, wpb_ref, wo_ref, g2_ref,
                 wup_ref, cw_ref, cb_ref, wdn_ref, gf_ref, *rest, tm, tiles_per_seq, per_row_state,
                 final_norm):
    if per_row_state:
        s1_ref, s2_ref, tpos_ref = rest[:3]
        rest = rest[3:]
    else:
        st_ref = rest[0]
        rest = rest[1:]
    xo_ref, a_ref = rest[:2]
    rest = rest[2:]
    if final_norm:
        y_ref = rest[0]
        rest = rest[1:]
    if not per_row_state:
        carry_scr = rest[0]

    ma = jnp.dot(ya_ref[...], wpa_ref[...], preferred_element_type=_F32)
    mb = jnp.dot(yb_ref[...], wpb_ref[...], preferred_element_type=_F32)
    merged = sga_ref[...].astype(_F32) * ma + sgb_ref[...].astype(_F32) * mb
    x1 = x_ref[...] + jnp.dot(merged.astype(_BF16), wo_ref[...], preferred_element_type=_F32)

    xn = _rms(x1, g2_ref[...]).astype(_BF16)
    if not per_row_state:
        @pl.when(pl.program_id(0) % tiles_per_seq == 0)
        def _():
            carry_scr[...] = st_ref[0]

    x2 = x1
    for lo, hi in zip(_FFN_SPLITS[:-1], _FFN_SPLITS[1:]):
        cols = slice(lo, hi)
        width = hi - lo
        a = jnp.dot(xn, wup_ref[:, cols], preferred_element_type=_F32)
        gate = jnp.dot(xn, wup_ref[:, D_FF + lo:D_FF + hi], preferred_element_type=_F32)
        prev1 = pltpu.roll(a, 1, 0)
        prev2 = pltpu.roll(a, 2, 0)
        if per_row_state:
            tpos = tpos_ref[:, cols]
            prev1 = jnp.where(tpos >= 1, prev1, s1_ref[:, cols])
            prev2 = jnp.where(tpos >= 2, prev2, s2_ref[:, cols])
            a_ref[:, cols] = a
        else:
            row = lax.broadcasted_iota(jnp.int32, (8, width), 0)
            c0 = carry_scr[0:1, cols]
            c1 = carry_scr[1:2, cols]
            top1 = jnp.where(row == 0, c1, prev1[:8])
            top2 = jnp.where(row == 0, c0, jnp.where(row == 1, c1, prev2[:8]))
            prev1 = jnp.concatenate([top1, prev1[8:]], axis=0)
            prev2 = jnp.concatenate([top2, prev2[8:]], axis=0)
            tail = a[tm - (CONV_W - 1):, :]
            carry_scr[:, cols] = tail
            a_ref[0, :, cols] = tail
        conv = (cb_ref[:, cols] + prev2 * cw_ref[0:1, cols] + prev1 * cw_ref[1:2, cols]
                + a * cw_ref[2:3, cols])
        act = (_gelu(conv) * gate).astype(_BF16)
        x2 = x2 + jnp.dot(act, wdn_ref[cols, :], preferred_element_type=_F32)
    xo_ref[...] = x2
    if final_norm:
        y_ref[...] = _rms(x2, gf_ref[...])


def _post_call(x2, ya, yb, sga, sgb, wpa, wpb, wo, g2, wup, cw, cb, wdn, gf, state_args, *,
               layer, tm, tiles_per_seq, per_row_state, final_norm):
    n = x2.shape[0]
    row = lambda width: pl.BlockSpec((tm, width), lambda i: (i, 0))
    in_specs = [row(D_MODEL), row(A_WIDTH), row(B_WIDTH), row(D_MODEL), row(D_MODEL),
                _layer_spec((A_WIDTH, D_MODEL), layer), _layer_spec((B_WIDTH, D_MODEL), layer),
                _layer_spec((D_MODEL, D_MODEL), layer), _const_spec((1, D_MODEL)),
                _layer_spec((D_MODEL, 2 * D_FF), layer), _const_spec((CONV_W, D_FF)),
                _const_spec((1, D_FF)), _layer_spec((D_FF, D_MODEL), layer),
                _const_spec((1, D_MODEL))]
    out_shape = [jax.ShapeDtypeStruct((n, D_MODEL), _F32)]
    out_specs = [row(D_MODEL)]
    scratch = []
    if per_row_state:
        in_specs += [row(D_FF), row(D_FF), row(D_FF)]
        out_shape.append(jax.ShapeDtypeStruct((n, D_FF), _F32))
        out_specs.append(row(D_FF))
    else:
        nseq = n // (tm * tiles_per_seq)
        in_specs.append(pl.BlockSpec((1, CONV_W - 1, D_FF), lambda i: (i // tiles_per_seq, 0, 0)))
        out_shape.append(jax.ShapeDtypeStruct((nseq, CONV_W - 1, D_FF), _F32))
        out_specs.append(pl.BlockSpec((1, CONV_W - 1, D_FF), lambda i: (i // tiles_per_seq, 0, 0)))
        scratch.append(pltpu.VMEM((CONV_W - 1, D_FF), _F32))
    if final_norm:
        out_shape.append(jax.ShapeDtypeStruct((n, D_MODEL), _F32))
        out_specs.append(row(D_MODEL))
    return pl.pallas_call(
        functools.partial(_post_kernel, tm=tm, tiles_per_seq=tiles_per_seq,
                          per_row_state=per_row_state, final_norm=final_norm),
        grid=(n // tm,),
        in_specs=in_specs,
        out_specs=out_specs,
        out_shape=out_shape,
        scratch_shapes=scratch,
        compiler_params=pltpu.CompilerParams(dimension_semantics=("arbitrary",),
                                             vmem_limit_bytes=_VMEM_LIMIT),
        name="merge_convffn",
    )(x2, ya, yb, sga, sgb, wpa, wpb, wo, g2, wup, cw, cb, wdn, gf, *state_args)


def _t5_bucket(rel):
    max_exact = N_BUCKETS // 2
    relf = jnp.maximum(rel, 1).astype(_F32)
    large = max_exact + (jnp.log(relf / max_exact) / math.log(MAX_DISTANCE / max_exact)
                         * (N_BUCKETS - max_exact)).astype(jnp.int32)
    large = jnp.minimum(large, N_BUCKETS - 1)
    return jnp.where(rel < max_exact, rel, large)


def _far_bucket_is_last(min_rel):
    max_exact = N_BUCKETS // 2
    v = max_exact + int(math.log(min_rel / max_exact) / math.log(MAX_DISTANCE / max_exact)
                        * (N_BUCKETS - max_exact) - 1e-6)
    return v >= N_BUCKETS - 1


def _toeplitz(diag, n):
    width = 2 * n
    flat = jnp.tile(diag, (1,) * (diag.ndim - 1) + (n,))
    skew = flat[..., :n * (width - 1)].reshape(diag.shape[:-1] + (n, width - 1))
    return skew[..., n - 1:2 * n - 1]


def _prompt_bias_tables(rel_bias):
    assert _far_bucket_is_last(MOBA_BLOCK + 1)
    bt = rel_bias.T.astype(_F32)
    bt = (bt - bt[:, N_BUCKETS - 1:]) * _LOG2E
    rel = jnp.arange(2 * MOBA_BLOCK, dtype=jnp.int32) - (MOBA_BLOCK - 1)
    own = jnp.where(rel >= 0, bt[:, _t5_bucket(jnp.maximum(rel, 0))], _NEG)
    prev = bt[:, _t5_bucket(rel + MOBA_BLOCK)]
    own, prev = _toeplitz(own, MOBA_BLOCK), _toeplitz(prev, MOBA_BLOCK)
    tables = [None] * _N_TABLES
    tables[_TAB_OWN], tables[_TAB_PREV] = own, prev
    tables[_TAB_FAR], tables[_TAB_DEAD] = jnp.zeros_like(own), jnp.full_like(own, _NEG)
    return jnp.stack(tables, axis=1)


def _sample_bias_tables(rel_bias, tdec, tpad):
    assert _far_bucket_is_last(MOBA_BLOCK + 1)
    bt = rel_bias.T.astype(_F32)
    tok = jnp.arange(tdec, dtype=jnp.int32)[:, None]
    key = jnp.arange(MOBA_BLOCK, dtype=jnp.int32)[None, :]
    near = bt[:, _t5_bucket(MOBA_BLOCK + tok - key)]
    far = jnp.broadcast_to(bt[:, N_BUCKETS - 1][:, None, None], (B_HEADS, tdec, 128))
    new = jnp.arange(tpad, dtype=jnp.int32)[None, :]
    own = jnp.where((tok - new >= 0) & (new < tdec), bt[:, _t5_bucket(jnp.maximum(tok - new, 0))],
                    _NEG)
    flat = lambda a: a.reshape(B_HEADS * tdec, a.shape[-1])
    return flat(near), flat(far), flat(own)


def kernel(x_prompt, x_sample, cache_k, cache_v, state_conv, page_table, w_in, ln_v_g, ln_v_b, w_s,
           b_s, w_pa, w_pb, w_o, norm1_g, norm2_g, w_up, conv_w, conv_b, w_down, rel_bias, norm_f):
    nbp_, seq, _ = x_prompt.shape
    nbs, tdec, _ = x_sample.shape
    depth, n_pool, page_size = cache_k.shape[:3]
    n_pages = page_table.shape[1]
    ns = nbs * tdec
    np_ = nbp_ * seq
    assert page_size == CHUNK and n_pages % _PAGES_PER_STEP == 0 and seq % MOBA_BLOCK == 0
    assert _PAGES_PER_STEP % _PAGES_PER_BLOCK == 0 and seq % _POST_TILE == 0
    assert CONV_W - 1 <= tdec <= CHUNK and ns % CHUNK == 0 and CHUNK % tdec == 0
    tpad = -(-tdec // 8) * 8
    tm_p = _PROJ_TILE
    nb = seq // MOBA_BLOCK
    nb_past = n_pages // _PAGES_PER_BLOCK
    rows = B_HEADS * tdec

    xp = x_prompt.reshape(np_, D_MODEL)
    xs = x_sample.reshape(ns, D_MODEL)
    cache_kt = cache_k.transpose(0, 1, 3, 4, 2).reshape(depth * n_pool, B_WIDTH, page_size)
    cache_vt = cache_v.transpose(0, 1, 3, 4, 2).reshape(depth * n_pool, B_WIDTH, page_size)
    page_flat = page_table.reshape(-1).astype(jnp.int32)

    btab = _prompt_bias_tables(rel_bias)
    bnear_s, bfar_s, bown_s = _sample_bias_tables(rel_bias, tdec, tpad)
    row_head = jnp.arange(rows, dtype=jnp.int32)[:, None] // tdec
    hmask = (jnp.arange(B_WIDTH, dtype=jnp.int32)[None, :] // B_HEAD_DIM == row_head).astype(_F32)
    rsum = (jnp.arange(rows, dtype=jnp.int32)[None, :] % tdec
            == jnp.arange(tdec, dtype=jnp.int32)[:, None]).astype(_F32)
    lane_pad = lambda a: jnp.pad(a, ((0, 0), (0, 128 - rows)))
    bownt_s, hmaskt = lane_pad(bown_s.T), lane_pad(hmask.T)
    rsum_p = lane_pad(jnp.pad(rsum, ((0, tpad - tdec), (0, 0))))
    tpos = jnp.broadcast_to((jnp.arange(ns, dtype=jnp.int32) % tdec)[:, None], (ns, D_FF))
    tril = jnp.tril(jnp.ones((CHUNK, CHUNK), bool))
    idx = jnp.arange(CHUNK, dtype=jnp.int32)
    same_chunk = tril & (idx[:, None] // tdec == idx[None, :] // tdec)
    zero_state = jnp.zeros((nbp_, CONV_W - 1, D_FF), _F32)
    w_in_b, wpa, wpb, wo = (w.astype(_BF16) for w in (w_in, w_pa, w_pb, w_o))
    wup, wdn = w_up.astype(_BF16), w_down.astype(_BF16)
    gf = norm_f[None, :]

    ks_l, vs_l, as_l, cp_l, cs_l = [], [], [], [], []
    yp = ys = kv_nat = None
    for l in range(depth):
        last = l == depth - 1
        g1 = norm1_g[l][None, :]
        lng, lnb = ln_v_g[l][None, :], ln_v_b[l][None, :]
        g2 = norm2_g[l][None, :]
        cw, cb = conv_w[l], conv_b[l][None, :]
        wmix_p = jnp.where(tril[None], w_s[l], 0.0).astype(_BF16)
        bmix_p = jnp.broadcast_to(b_s[l][:, :, None], (A_GROUPS, CHUNK, CHUNK))
        reps = CHUNK // tdec
        wmix_s = jnp.where(same_chunk[None], jnp.tile(w_s[l][:, :tdec, :tdec], (1, reps, reps)),
                           0.0).astype(_BF16)
        bmix_s = jnp.broadcast_to(jnp.tile(b_s[l][:, :tdec], (1, reps))[:, :, None],
                                  (A_GROUPS, CHUNK, CHUNK))

        (qp, kbp, vtp, kmeanp, knat, vnat, yap, sgap, sgbp) = _proj_call(
            xp, g1, w_in_b, lng, lnb, wmix_p, bmix_p, tm=tm_p, prompt=True, layer=l, depth=depth,
            seq=seq, kv_prev=kv_nat)
        kv_nat = (knat, vnat)
        ybp = _moba_prompt_call(qp.reshape(nbp_, seq, B_WIDTH), kbp.reshape(nbp_, seq, -1),
                                vtp, kmeanp.reshape(nbp_, nb, B_WIDTH), btab)
        outs = _post_call(xp, yap, ybp.reshape(np_, B_WIDTH), sgap, sgbp, wpa, wpb, wo, g2,
                          wup, cw, cb, wdn, gf, (zero_state,), layer=l, tm=_POST_TILE,
                          tiles_per_seq=seq // _POST_TILE, per_row_state=False, final_norm=last)
        xp, cp = outs[0], outs[1]
        if last:
            yp = outs[2]

        qs, ks, vs, yas, sgas, sgbs, vas = _proj_call(xs, g1, w_in_b, lng, lnb, wmix_s, bmix_s,
                                                      tm=CHUNK, prompt=False, layer=l)
        ks4 = ks.reshape(nbs, tdec, B_HEADS, B_HEAD_DIM)
        vs4 = vs.reshape(nbs, tdec, B_HEADS, B_HEAD_DIM)
        qbd = (qs.reshape(nbs, 1, tdec, B_WIDTH) * hmask.reshape(B_HEADS, tdec, B_WIDTH)[None]
               ).reshape(nbs, rows, B_WIDTH)
        pad = ((0, 0), (0, tpad - tdec), (0, 0))
        knew = jnp.pad(ks.reshape(nbs, tdec, B_WIDTH), pad)
        vnew = jnp.pad(vs.reshape(nbs, tdec, B_WIDTH), pad)
        ybs = _moba_sample_call(page_flat, qbd, knew, vnew, bnear_s, bfar_s, bownt_s, hmaskt,
                                rsum_p, cache_kt, cache_vt, layer=l, n_pool=n_pool, n_pages=n_pages,
                                tdec=tdec)
        ybs = ybs.reshape(ns, B_WIDTH)
        st = state_conv[l]
        s1 = jnp.broadcast_to(st[:, 1:2, :], (nbs, tdec, D_FF)).reshape(ns, D_FF)
        s2 = jnp.concatenate([st, jnp.zeros((nbs, tdec - (CONV_W - 1), D_FF), _F32)],
                             axis=1).reshape(ns, D_FF)
        outs = _post_call(xs, yas, ybs.astype(_BF16), sgas, sgbs, wpa, wpb, wo,
                          g2, wup, cw, cb, wdn, gf, (s1, s2, tpos), layer=l, tm=CHUNK,
                          tiles_per_seq=1, per_row_state=True, final_norm=last)
        xs, a_s = outs[0], outs[1]
        if last:
            ys = outs[2]

        ks_l.append(ks4)
        vs_l.append(vs4)
        as_l.append(vas.reshape(nbs, tdec, A_WIDTH))
        cp_l.append(cp)
        a_ext = jnp.concatenate([st, a_s.reshape(nbs, tdec, D_FF)], axis=1)
        cs_l.append(a_ext[:, -(CONV_W - 1):])

    def kv_out(t):
        return t.reshape(depth, nbp_, B_HEADS, B_HEAD_DIM, seq).transpose(0, 1, 4, 2, 3)

    return (yp.reshape(nbp_, seq, D_MODEL), ys.reshape(nbs, tdec, D_MODEL),
            kv_out(kv_nat[0]), kv_out(kv_nat[1]),
            jnp.stack(ks_l, 0), jnp.stack(vs_l, 0),
            jnp.stack(as_l, 0), jnp.stack(cp_l, 0), jnp.stack(cs_l, 0))
```

```python
import functools
import math

import jax
import jax.numpy as jnp
from jax import lax
from jax.experimental import pallas as pl
from jax.experimental.pallas import tpu as pltpu

D_MODEL = 1024
CHUNK = 128
A_WIDTH = D_MODEL // 2
A_GROUPS = 4
A_GROUP_DIM = A_WIDTH // A_GROUPS
B_HEADS = 8
B_HEAD_DIM = 64
B_WIDTH = B_HEADS * B_HEAD_DIM
MOBA_BLOCK = 256
MOBA_TOPK = 3
N_BUCKETS = 32
MAX_DISTANCE = 128
D_FF = 2816
CONV_W = 3
EPS = 1e-6
N_IN = 2 * A_WIDTH + 3 * B_WIDTH + 2 * D_MODEL

_C_U, _C_VA, _C_Q, _C_K, _C_V, _C_GA, _C_GB = 0, 512, 1024, 1536, 2048, 2560, 3584

_NEG = -1e30
_HEADS_PER_PAIR = 2
_PAIRS_PER_STEP = 2
_BLOCKS_PER_ITER = 4
_TAB_OWN, _TAB_PREV, _TAB_FAR, _TAB_DEAD, _N_TABLES = 0, 1, 2, 3, 4
_PAIR_LANES = _HEADS_PER_PAIR * B_HEAD_DIM
_K_LANES = 2 * _PAIR_LANES
_V_ROWS = B_HEAD_DIM + 16
_LOG2E = math.log2(math.e)
_PAGES_PER_STEP = 16
_PAGES_PER_BLOCK = MOBA_BLOCK // CHUNK
_PAGE_SLOTS = 3
_PROJ_TILE = 512
_POST_TILE = 512
_FFN_SPLITS = (0, 1536, D_FF)
_VMEM_LIMIT = 48 * 1024 * 1024
_SCALE = 1.0 / math.sqrt(B_HEAD_DIM)

_F32 = jnp.float32
_BF16 = jnp.bfloat16


def _gelu(x):
    k = -2.0 * 0.7978845608028654 * _LOG2E
    return x / (1.0 + jnp.exp2(x * (k + (k * 0.044715) * (x * x))))


def _sigmoid(x):
    return 1.0 / (1.0 + jnp.exp2(x * -_LOG2E))


def _rms(x, g):
    return x * lax.rsqrt(jnp.mean(x * x, axis=-1, keepdims=True) + EPS) * g


def _const_spec(shape):
    zeros = (0,) * len(shape)
    return pl.BlockSpec(shape, lambda *_: zeros, pipeline_mode=pl.Buffered(1))


def _layer_spec(shape, layer):
    index = (layer,) + (0,) * len(shape)
    return pl.BlockSpec((None,) + shape, lambda *_: index, pipeline_mode=pl.Buffered(1))


def _proj_kernel(x_ref, g1_ref, w_ref, lng_ref, lnb_ref, wmix_ref, bmix_ref, *rest, tm, prompt,
                 nb=None):
    if prompt:
        (q_ref, kb_ref, vt_ref, kmean_ref, knat_ref, vnat_ref, ya_ref, sga_ref, sgb_ref) = rest[-9:]
    else:
        (q_ref, k_ref, v_ref, ya_ref, sga_ref, sgb_ref, va_ref) = rest
    xn = _rms(x_ref[...], g1_ref[...]).astype(_BF16)

    def proj(lo, width):
        return jnp.dot(xn, w_ref[:, lo:lo + width], preferred_element_type=_F32)

    u = _gelu(proj(_C_U, A_WIDTH))
    va = _gelu(proj(_C_VA, A_WIDTH))
    mu = jnp.mean(va, axis=-1, keepdims=True)
    vc = va - mu
    var = jnp.mean(vc * vc, axis=-1, keepdims=True)
    va = vc * lax.rsqrt(var + EPS) * lng_ref[...] + lnb_ref[...]
    if not prompt:
        va_ref[...] = va
    for c in range(tm // CHUNK):
        rows = slice(c * CHUNK, (c + 1) * CHUNK)
        for g in range(A_GROUPS):
            cols = slice(g * A_GROUP_DIM, (g + 1) * A_GROUP_DIM)
            mixed = jnp.dot(wmix_ref[g], va[rows, cols].astype(_BF16),
                            preferred_element_type=_F32) + bmix_ref[g]
            ya_ref[rows, cols] = (u[rows, cols] * mixed).astype(_BF16)

    q_ref[...] = proj(_C_Q, B_WIDTH)
    k = proj(_C_K, B_WIDTH)
    v = proj(_C_V, B_WIDTH)
    if prompt:
        blocks = tm // MOBA_BLOCK
        vt = v.T
        knat_ref[0] = k.T
        vnat_ref[0] = vt
        for r in range(blocks):
            kmean_ref[0, r:r + 1, :] = jnp.mean(k[r * MOBA_BLOCK:(r + 1) * MOBA_BLOCK], axis=0,
                                                keepdims=True)
        kb = k.astype(_BF16)
        first_blk = (pl.program_id(0) * blocks) % nb
        shape = (tm, _K_LANES - _PAIR_LANES)
        blk_of_row = first_blk + lax.broadcasted_iota(jnp.int32, shape, 0) // MOBA_BLOCK
        onehot = jnp.where(lax.broadcasted_iota(jnp.int32, shape, 1) == blk_of_row, 1.0, 0.0
                           ).astype(_BF16)
        for hp in range(B_HEADS // _HEADS_PER_PAIR):
            kb_ref[:, hp * _K_LANES:hp * _K_LANES + _PAIR_LANES] = (
                kb[:, hp * _PAIR_LANES:(hp + 1) * _PAIR_LANES])
            kb_ref[:, hp * _K_LANES + _PAIR_LANES:(hp + 1) * _K_LANES] = onehot
        vtb = vt.astype(_BF16)
        row = lax.broadcasted_iota(jnp.int32, (_V_ROWS - B_HEAD_DIM, MOBA_BLOCK), 0)
        ones_row = jnp.where(row == 0, 1.0, 0.0).astype(_BF16)
        for r in range(blocks):
            keys = slice(r * MOBA_BLOCK, (r + 1) * MOBA_BLOCK)
            for h in range(B_HEADS):
                vt_ref[0, r, h * _V_ROWS:h * _V_ROWS + B_HEAD_DIM, :] = (
                    vtb[h * B_HEAD_DIM:(h + 1) * B_HEAD_DIM, keys])
                vt_ref[0, r, h * _V_ROWS + B_HEAD_DIM:(h + 1) * _V_ROWS, :] = ones_row
    else:
        k_ref[...] = k
        v_ref[...] = v
    sga_ref[...] = _sigmoid(proj(_C_GA, D_MODEL)).astype(_BF16)
    sgb_ref[...] = _sigmoid(proj(_C_GB, D_MODEL)).astype(_BF16)


def _proj_call(x2, g1, w_in, lng, lnb, wmix, bmix, *, tm, prompt, layer=0, depth=1, seq=None,
               kv_prev=None):
    n = x2.shape[0]
    nt = n // tm
    row = lambda width: pl.BlockSpec((tm, width), lambda i: (i, 0))
    in_specs = [row(D_MODEL), _const_spec((1, D_MODEL)), _layer_spec((D_MODEL, N_IN), layer),
                _const_spec((1, A_WIDTH)), _const_spec((1, A_WIDTH)),
                _const_spec((A_GROUPS, CHUNK, CHUNK)), _const_spec((A_GROUPS, CHUNK, CHUNK))]
    args = [x2, g1, w_in, lng, lnb, wmix, bmix]
    aliases = {}
    tail_shape = [jax.ShapeDtypeStruct((n, A_WIDTH), _BF16),
                  jax.ShapeDtypeStruct((n, D_MODEL), _BF16),
                  jax.ShapeDtypeStruct((n, D_MODEL), _BF16)]
    tail_specs = [row(A_WIDTH), row(D_MODEL), row(D_MODEL)]
    nb = None
    if prompt:
        nb = seq // MOBA_BLOCK
        blocks = tm // MOBA_BLOCK
        tps = seq // tm
        assert tm % MOBA_BLOCK == 0 and seq % tm == 0 and nb <= _K_LANES - _PAIR_LANES
        nseq = n // seq
        k_cols = B_HEADS // _HEADS_PER_PAIR * _K_LANES
        nat = jax.ShapeDtypeStruct((depth * nseq, B_WIDTH, seq), _F32)
        nat_spec = pl.BlockSpec((1, B_WIDTH, tm), lambda i: (layer * nseq + i // tps, 0, i % tps))
        out_shape = [jax.ShapeDtypeStruct((n, B_WIDTH), _F32),
                     jax.ShapeDtypeStruct((n, k_cols), _BF16),
                     jax.ShapeDtypeStruct((nseq, nb, B_HEADS * _V_ROWS, MOBA_BLOCK), _BF16),
                     jax.ShapeDtypeStruct((nt, blocks, B_WIDTH), _F32), nat, nat] + tail_shape
        out_specs = [row(B_WIDTH), row(k_cols),
                     pl.BlockSpec((1, blocks, B_HEADS * _V_ROWS, MOBA_BLOCK),
                                  lambda i: (i // tps, i % tps, 0, 0)),
                     pl.BlockSpec((1, blocks, B_WIDTH), lambda i: (i, 0, 0)), nat_spec, nat_spec
                     ] + tail_specs
        if kv_prev is not None:
            in_specs += [pl.BlockSpec(memory_space=pl.ANY)] * 2
            args += list(kv_prev)
            aliases = {7: 4, 8: 5}
    else:
        out_shape = [jax.ShapeDtypeStruct((n, B_WIDTH), _F32)] * 3 + tail_shape + [
            jax.ShapeDtypeStruct((n, A_WIDTH), _F32)]
        out_specs = [row(B_WIDTH)] * 3 + tail_specs + [row(A_WIDTH)]
    return pl.pallas_call(
        functools.partial(_proj_kernel, tm=tm, prompt=prompt, nb=nb),
        grid=(nt,),
        in_specs=in_specs,
        out_specs=out_specs,
        out_shape=out_shape,
        input_output_aliases=aliases,
        compiler_params=pltpu.CompilerParams(dimension_semantics=("arbitrary",),
                                             vmem_limit_bytes=_VMEM_LIMIT),
        name="proj_mixer_a",
    )(*args)


def _topk_block_mask(gate, own, nb, width):
    blk = lax.broadcasted_iota(jnp.int32, (nb, width), 0)
    gm = jnp.where(blk < own, gate, -jnp.inf)
    rank = jnp.zeros((nb, width), _F32)
    for jp in range(nb):
        r = gm[jp:jp + 1, :]
        beats = jnp.where(r > gm, 1.0, jnp.where(r == gm, jnp.where(blk > jp, 1.0, 0.0), 0.0))
        rank = rank + beats
    past_sel = jnp.where(blk < own, jnp.where(rank < MOBA_TOPK, 1.0, 0.0), 0.0)
    return jnp.where(blk == own, 1.0, past_sel)


def _moba_prompt_kernel(q_ref, k_ref, vt_ref, kmean_ref, btab_ref, o_ref, s_scr, *, nb):
    i = pl.program_id(2)
    blk = MOBA_BLOCK
    heads = range(_PAIRS_PER_STEP * _HEADS_PER_PAIR)

    qt = q_ref[0].T
    feat = lax.broadcasted_iota(jnp.int32, (_PAIR_LANES, blk), 0)
    qz = []
    for hh in heads:
        pair, sub = divmod(hh, _HEADS_PER_PAIR)
        in_head = (feat >= sub * B_HEAD_DIM) & (feat < (sub + 1) * B_HEAD_DIM)
        qh = jnp.where(in_head, qt[pair * _PAIR_LANES:(pair + 1) * _PAIR_LANES], 0.0)
        gate = jnp.dot(kmean_ref[0, :, pair * _PAIR_LANES:(pair + 1) * _PAIR_LANES], qh,
                       preferred_element_type=_F32, precision=lax.Precision.HIGHEST)
        mask_rows = jnp.where(_topk_block_mask(gate, i, nb, blk) > 0.5, 0.0, _NEG)
        pad = jnp.zeros((_K_LANES - _PAIR_LANES - nb, blk), _F32)
        qz.append(jnp.concatenate([qh * (_SCALE * _LOG2E), mask_rows, pad], axis=0).astype(_BF16))

    n_iter = lax.div(i + _BLOCKS_PER_ITER, _BLOCKS_PER_ITER)

    def slots(it):
        js = [it * _BLOCKS_PER_ITER + u for u in range(_BLOCKS_PER_ITER)]
        tabs = [jnp.where(j == i, _TAB_OWN, jnp.where(j == i - 1, _TAB_PREV,
                                                      jnp.where(j > i, _TAB_DEAD, _TAB_FAR)))
                for j in js]
        return js, [jnp.minimum(j, i) for j in js], tabs

    def logits_pass(it, ms, group):
        js, jcs, tabs = slots(it)
        out = []
        for hh, m in zip(group, ms):
            pair = hh // _HEADS_PER_PAIR
            for j, jc, tab in zip(js, jcs, tabs):
                kj = k_ref[0, pl.ds(pl.multiple_of(jc * blk, blk), blk),
                           pair * _K_LANES:(pair + 1) * _K_LANES]
                s = jnp.dot(kj, qz[hh], preferred_element_type=_F32) + btab_ref[hh, tab]
                s_scr[hh, j] = s
                m = jnp.maximum(m, jnp.max(s, axis=0, keepdims=True))
            out.append(m)
        return tuple(out)

    def value_pass(it, accs, group, ms):
        js, jcs, _ = slots(it)
        out = []
        for hh, acc, m in zip(group, accs, ms):
            for j, jc in zip(js, jcs):
                p = jnp.exp2(s_scr[hh, j] - m).astype(_BF16)
                acc = acc + jnp.dot(vt_ref[0, jc, hh * _V_ROWS:(hh + 1) * _V_ROWS, :], p,
                                    preferred_element_type=_F32)
            out.append(acc)
        return tuple(out)

    group = tuple(heads)
    ms = lax.fori_loop(0, n_iter, lambda it, ms: logits_pass(it, ms, group),
                       tuple(jnp.full((1, blk), _NEG, _F32) for _ in group))
    accs = lax.fori_loop(0, n_iter, lambda it, accs: value_pass(it, accs, group, ms),
                         tuple(jnp.zeros((_V_ROWS, blk), _F32) for _ in group))
    outs = [acc[:B_HEAD_DIM] / acc[B_HEAD_DIM:B_HEAD_DIM + 1] for acc in accs]
    o_ref[0] = jnp.concatenate(outs, axis=0).T.astype(_BF16)


def _moba_prompt_call(q3, kb3, vt4, kmean3, btab):
    b, t, _ = q3.shape
    nb = t // MOBA_BLOCK
    g = _PAIRS_PER_STEP
    heads = g * _HEADS_PER_PAIR
    return pl.pallas_call(
        functools.partial(_moba_prompt_kernel, nb=nb),
        grid=(b, B_HEADS // heads, nb),
        in_specs=[pl.BlockSpec((1, MOBA_BLOCK, g * _PAIR_LANES), lambda bb, hp, i: (bb, i, hp)),
                  pl.BlockSpec((1, t, g * _K_LANES), lambda bb, hp, i: (bb, 0, hp)),
                  pl.BlockSpec((1, nb, heads * _V_ROWS, MOBA_BLOCK),
                               lambda bb, hp, i: (bb, 0, hp, 0)),
                  pl.BlockSpec((1, nb, g * _PAIR_LANES), lambda bb, hp, i: (bb, 0, hp)),
                  pl.BlockSpec((heads, _N_TABLES, MOBA_BLOCK, MOBA_BLOCK),
                               lambda bb, hp, i: (hp, 0, 0, 0), pipeline_mode=pl.Buffered(1))],
        out_specs=pl.BlockSpec((1, MOBA_BLOCK, g * _PAIR_LANES), lambda bb, hp, i: (bb, i, hp)),
        out_shape=jax.ShapeDtypeStruct((b, t, B_WIDTH), _BF16),
        scratch_shapes=[pltpu.VMEM((heads, nb + _BLOCKS_PER_ITER - 1, MOBA_BLOCK, MOBA_BLOCK), _F32)],
        compiler_params=pltpu.CompilerParams(
            dimension_semantics=("arbitrary", "arbitrary", "arbitrary"),
            vmem_limit_bytes=_VMEM_LIMIT),
        name="moba_prompt",
    )(q3, kb3, vt4, kmean3, btab)


def _moba_sample_kernel(pt_ref, qbd_ref, knew_ref, vnew_ref, bnear_ref, bfar_ref,
                        bownt_ref, hmaskt_ref, rsum_ref, ck_hbm, cv_hbm, o_ref,
                        gate_scr, m_scr, l_scr, acc_scr, st_scr, qbdt_scr, kbuf, vbuf, sems,
                        *, nb_past, page_base):
    pages = _PAGES_PER_STEP
    c = pl.program_id(1)
    step = pl.program_id(0) * pl.num_programs(1) + c
    last_step = pl.num_programs(0) * pl.num_programs(1) - 1
    rows = qbd_ref.shape[1]
    tdec = o_ref.shape[1]
    blocks_per_step = pages // _PAGES_PER_BLOCK

    ahead = _PAGE_SLOTS - 1

    def slot_of(s):
        return lax.rem(s, _PAGE_SLOTS)

    def page_copy(kind, of_step, p):
        src_step = jnp.minimum(of_step, last_step)
        src, buf = ((ck_hbm, kbuf), (cv_hbm, vbuf))[kind]
        page = page_base + pt_ref[src_step * pages + p]
        slot = slot_of(of_step)
        return pltpu.make_async_copy(src.at[page], buf.at[slot, p], sems.at[slot, kind])

    def block_pages(bl):
        return range(bl * _PAGES_PER_BLOCK, (bl + 1) * _PAGES_PER_BLOCK)

    @pl.when(step == 0)
    def _():
        for s in range(ahead):
            for kind in range(2):
                for pg in range(pages):
                    page_copy(kind, s, pg).start()

    for kind in range(2):
        for pg in range(pages):
            page_copy(kind, step, pg).wait()
    slot = slot_of(step)

    def lanes_t(x):
        return jnp.concatenate([x, jnp.zeros((128 - x.shape[0], x.shape[1]), x.dtype)], axis=0).T

    qbd = qbd_ref[0]
    qz = (qbd * _SCALE).astype(_BF16)

    @pl.when(c == 0)
    def _():
        qbdt_scr[...] = lanes_t(qbd)

    qbdt = qbdt_scr[...]
    row_pad = jnp.zeros((128 - rows, MOBA_BLOCK), _F32)

    for bl in range(blocks_per_step):
        jg = c * blocks_per_step + bl
        kts = [kbuf[slot, pg] for pg in block_pages(bl)]
        kt = jnp.concatenate(kts, axis=1)
        kmean = jnp.sum(functools.reduce(lambda a, b: a + b, kts), axis=-1,
                        keepdims=True) * (1.0 / MOBA_BLOCK)
        gate_scr[pl.ds(jg, 1), :] = jnp.sum(qbdt * kmean, axis=0, keepdims=True)
        s = jnp.dot(qz, kt.astype(_BF16), preferred_element_type=_F32)
        s = s + jnp.where(jg == nb_past - 1, bnear_ref[...], bfar_ref[:, 0:1])
        st_scr[bl] = jnp.concatenate([s, row_pad], axis=0).T
        for pg in block_pages(bl):
            page_copy(0, step + ahead, pg).start()
    for bl in range(blocks_per_step):
        jg = c * blocks_per_step + bl
        vt = jnp.concatenate([vbuf[slot, pg] for pg in block_pages(bl)], axis=1)
        st = st_scr[bl]
        m = jnp.max(st, axis=0, keepdims=True)
        p = jnp.exp(st - m)
        m_scr[pl.ds(jg, 1), :] = m
        l_scr[pl.ds(jg, 1), :] = jnp.sum(p, axis=0, keepdims=True)
        acc_scr[jg] = jnp.dot(vt.astype(_BF16), p.astype(_BF16), preferred_element_type=_F32)
        for pg in block_pages(bl):
            page_copy(1, step + ahead, pg).start()

    @pl.when(step == last_step)
    def _():
        for s in range(1, ahead + 1):
            for kind in range(2):
                for pg in range(pages):
                    page_copy(kind, last_step + s, pg).wait()

    @pl.when(c == pl.num_programs(1) - 1)
    def _():
        sel = _topk_block_mask(gate_scr[...], nb_past, nb_past, 128) > 0.5
        m_blk = m_scr[...]
        s_own = jnp.dot(knew_ref[0], qbdt * _SCALE, preferred_element_type=_F32) + bownt_ref[...]
        m_tot = jnp.maximum(jnp.max(s_own, axis=0, keepdims=True),
                            jnp.max(jnp.where(sel, m_blk, _NEG), axis=0, keepdims=True))
        w = jnp.where(sel, jnp.exp(m_blk - m_tot), 0.0)
        p_own = jnp.exp(s_own - m_tot)
        l_tot = jnp.sum(p_own, axis=0, keepdims=True) + jnp.sum(w * l_scr[...], axis=0, keepdims=True)
        p_own = jnp.concatenate([p_own, jnp.zeros((128 - p_own.shape[0], 128), _F32)], axis=0)
        accs = [jnp.dot(lanes_t(vnew_ref[0]), p_own, preferred_element_type=_F32)]
        accs += [jnp.zeros_like(accs[0]) for _ in range(3)]
        for j in range(nb_past):
            accs[j % 4] = accs[j % 4] + w[j:j + 1, :] * acc_scr[j]
        out_t = ((accs[0] + accs[1]) + (accs[2] + accs[3])) / l_tot
        folded = lax.dot_general(rsum_ref[...], out_t * hmaskt_ref[...], (((1,), (1,)), ((), ())),
                                 preferred_element_type=_F32, precision=lax.Precision.HIGHEST)
        o_ref[0] = folded[:tdec]


def _moba_sample_call(page_flat, qbd, knew, vnew, bnear, bfar, bownt, hmaskt, rsum,
                      cache_kt, cache_vt, *, layer, n_pool, n_pages, tdec):
    nbatch, rows, _ = qbd.shape
    pages = _PAGES_PER_STEP
    nb_past = n_pages // _PAGES_PER_BLOCK
    assert nb_past % 8 == 0 and rows <= 128
    assert nbatch * (n_pages // pages) >= _PAGE_SLOTS

    per_batch = lambda arr: pl.BlockSpec((1,) + arr.shape[1:], lambda bb, cc, pt: (bb, 0, 0))
    const = lambda arr: pl.BlockSpec(arr.shape, lambda bb, cc, pt: (0,) * arr.ndim)
    page_ring = pltpu.VMEM((_PAGE_SLOTS, pages, B_WIDTH, CHUNK), _F32)
    grid_spec = pltpu.PrefetchScalarGridSpec(
        num_scalar_prefetch=1,
        grid=(nbatch, n_pages // pages),
        in_specs=[per_batch(qbd), per_batch(knew), per_batch(vnew),
                  const(bnear), const(bfar), const(bownt), const(hmaskt), const(rsum),
                  pl.BlockSpec(memory_space=pl.ANY), pl.BlockSpec(memory_space=pl.ANY)],
        out_specs=pl.BlockSpec((1, tdec, B_WIDTH), lambda bb, cc, pt: (bb, 0, 0)),
        scratch_shapes=[pltpu.VMEM((nb_past, 128), _F32),
                        pltpu.VMEM((nb_past, 128), _F32),
                        pltpu.VMEM((nb_past, 128), _F32),
                        pltpu.VMEM((nb_past, B_WIDTH, 128), _F32),
                        pltpu.VMEM((pages // _PAGES_PER_BLOCK, MOBA_BLOCK, 128), _F32),
                        pltpu.VMEM((B_WIDTH, 128), _F32),
                        page_ring, page_ring,
                        pltpu.SemaphoreType.DMA((_PAGE_SLOTS, 2))])
    return pl.pallas_call(
        functools.partial(_moba_sample_kernel, nb_past=nb_past, page_base=layer * n_pool),
        grid_spec=grid_spec,
        out_shape=jax.ShapeDtypeStruct((nbatch, tdec, B_WIDTH), _F32),
        compiler_params=pltpu.CompilerParams(dimension_semantics=("arbitrary", "arbitrary"),
                                             vmem_limit_bytes=_VMEM_LIMIT),
        name="moba_sample",
    )(page_flat, qbd, knew, vnew, bnear, bfar, bownt, hmaskt, rsum, cache_kt, cache_vt)


def _post_kernel(x_ref, ya_ref, yb_ref, sga_ref, sgb_ref, wpa_ref, wpb_ref, wo_ref, g2_ref,
                 wup_ref, cw_ref, cb_ref, wdn_ref, gf_ref, *rest, tm, tiles_per_seq, per_row_state,
                 final_norm):
    if per_row_state:
        s1_ref, s2_ref, tpos_ref = rest[:3]
        rest = rest[3:]
    else:
        st_ref = rest[0]
        rest = rest[1:]
    xo_ref, a_ref = rest[:2]
    rest = rest[2:]
    if final_norm:
        y_ref = rest[0]
        rest = rest[1:]
    if not per_row_state:
        carry_scr = rest[0]

    ma = jnp.dot(ya_ref[...], wpa_ref[...], preferred_element_type=_F32)
    mb = jnp.dot(yb_ref[...], wpb_ref[...], preferred_element_type=_F32)
    merged = sga_ref[...].astype(_F32) * ma + sgb_ref[...].astype(_F32) * mb
    x1 = x_ref[...] + jnp.dot(merged.astype(_BF16), wo_ref[...], preferred_element_type=_F32)

    xn = _rms(x1, g2_ref[...]).astype(_BF16)
    if not per_row_state:
        @pl.when(pl.program_id(0) % tiles_per_seq == 0)
        def _():
            carry_scr[...] = st_ref[0]

    x2 = x1
    for lo, hi in zip(_FFN_SPLITS[:-1], _FFN_SPLITS[1:]):
        cols = slice(lo, hi)
        width = hi - lo
        a = jnp.dot(xn, wup_ref[:, cols], preferred_element_type=_F32)
        gate = jnp.dot(xn, wup_ref[:, D_FF + lo:D_FF + hi], preferred_element_type=_F32)
        prev1 = pltpu.roll(a, 1, 0)
        prev2 = pltpu.roll(a, 2, 0)
        if per_row_state:
            tpos = tpos_ref[:, cols]
            prev1 = jnp.where(tpos >= 1, prev1, s1_ref[:, cols])
            prev2 = jnp.where(tpos >= 2, prev2, s2_ref[:, cols])
            a_ref[:, cols] = a
        else:
            row = lax.broadcasted_iota(jnp.int32, (8, width), 0)
            c0 = carry_scr[0:1, cols]
            c1 = carry_scr[1:2, cols]
            top1 = jnp.where(row == 0, c1, prev1[:8])
            top2 = jnp.where(row == 0, c0, jnp.where(row == 1, c1, prev2[:8]))
            prev1 = jnp.concatenate([top1, prev1[8:]], axis=0)
            prev2 = jnp.concatenate([top2, prev2[8:]], axis=0)
            tail = a[tm - (CONV_W - 1):, :]
            carry_scr[:, cols] = tail
            a_ref[0, :, cols] = tail
        conv = (cb_ref[:, cols] + prev2 * cw_ref[0:1, cols] + prev1 * cw_ref[1:2, cols]
                + a * cw_ref[2:3, cols])
        act = (_gelu(conv) * gate).astype(_BF16)
        x2 = x2 + jnp.dot(act, wdn_ref[cols, :], preferred_element_type=_F32)
    xo_ref[...] = x2
    if final_norm:
        y_ref[...] = _rms(x2, gf_ref[...])


def _post_call(x2, ya, yb, sga, sgb, wpa, wpb, wo, g2, wup, cw, cb, wdn, gf, state_args, *,
               layer, tm, tiles_per_seq, per_row_state, final_norm):
    n = x2.shape[0]
    row = lambda width: pl.BlockSpec((tm, width), lambda i: (i, 0))
    in_specs = [row(D_MODEL), row(A_WIDTH), row(B_WIDTH), row(D_MODEL), row(D_MODEL),
                _layer_spec((A_WIDTH, D_MODEL), layer), _layer_spec((B_WIDTH, D_MODEL), layer),
                _layer_spec((D_MODEL, D_MODEL), layer), _const_spec((1, D_MODEL)),
                _layer_spec((D_MODEL, 2 * D_FF), layer), _const_spec((CONV_W, D_FF)),
                _const_spec((1, D_FF)), _layer_spec((D_FF, D_MODEL), layer),
                _const_spec((1, D_MODEL))]
    out_shape = [jax.ShapeDtypeStruct((n, D_MODEL), _F32)]
    out_specs = [row(D_MODEL)]
    scratch = []
    if per_row_state:
        in_specs += [row(D_FF), row(D_FF), row(D_FF)]
        out_shape.append(jax.ShapeDtypeStruct((n, D_FF), _F32))
        out_specs.append(row(D_FF))
    else:
        nseq = n // (tm * tiles_per_seq)
        in_specs.append(pl.BlockSpec((1, CONV_W - 1, D_FF), lambda i: (i // tiles_per_seq, 0, 0)))
        out_shape.append(jax.ShapeDtypeStruct((nseq, CONV_W - 1, D_FF), _F32))
        out_specs.append(pl.BlockSpec((1, CONV_W - 1, D_FF), lambda i: (i // tiles_per_seq, 0, 0)))
        scratch.append(pltpu.VMEM((CONV_W - 1, D_FF), _F32))
    if final_norm:
        out_shape.append(jax.ShapeDtypeStruct((n, D_MODEL), _F32))
        out_specs.append(row(D_MODEL))
    return pl.pallas_call(
        functools.partial(_post_kernel, tm=tm, tiles_per_seq=tiles_per_seq,
                          per_row_state=per_row_state, final_norm=final_norm),
        grid=(n // tm,),
        in_specs=in_specs,
        out_specs=out_specs,
        out_shape=out_shape,
        scratch_shapes=scratch,
        compiler_params=pltpu.CompilerParams(dimension_semantics=("arbitrary",),
                                             vmem_limit_bytes=_VMEM_LIMIT),
        name="merge_convffn",
    )(x2, ya, yb, sga, sgb, wpa, wpb, wo, g2, wup, cw, cb, wdn, gf, *state_args)


def _t5_bucket(rel):
    max_exact = N_BUCKETS // 2
    relf = jnp.maximum(rel, 1).astype(_F32)
    large = max_exact + (jnp.log(relf / max_exact) / math.log(MAX_DISTANCE / max_exact)
                         * (N_BUCKETS - max_exact)).astype(jnp.int32)
    large = jnp.minimum(large, N_BUCKETS - 1)
    return jnp.where(rel < max_exact, rel, large)


def _far_bucket_is_last(min_rel):
    max_exact = N_BUCKETS // 2
    v = max_exact + int(math.log(min_rel / max_exact) / math.log(MAX_DISTANCE / max_exact)
                        * (N_BUCKETS - max_exact) - 1e-6)
    return v >= N_BUCKETS - 1


def _toeplitz(diag, n):
    width = 2 * n
    flat = jnp.tile(diag, (1,) * (diag.ndim - 1) + (n,))
    skew = flat[..., :n * (width - 1)].reshape(diag.shape[:-1] + (n, width - 1))
    return skew[..., n - 1:2 * n - 1]


def _prompt_bias_tables(rel_bias):
    assert _far_bucket_is_last(MOBA_BLOCK + 1)
    bt = rel_bias.T.astype(_F32)
    bt = (bt - bt[:, N_BUCKETS - 1:]) * _LOG2E
    rel = jnp.arange(2 * MOBA_BLOCK, dtype=jnp.int32) - (MOBA_BLOCK - 1)
    own = jnp.where(rel >= 0, bt[:, _t5_bucket(jnp.maximum(rel, 0))], _NEG)
    prev = bt[:, _t5_bucket(rel + MOBA_BLOCK)]
    own, prev = _toeplitz(own, MOBA_BLOCK), _toeplitz(prev, MOBA_BLOCK)
    tables = [None] * _N_TABLES
    tables[_TAB_OWN], tables[_TAB_PREV] = own, prev
    tables[_TAB_FAR], tables[_TAB_DEAD] = jnp.zeros_like(own), jnp.full_like(own, _NEG)
    return jnp.stack(tables, axis=1)


def _sample_bias_tables(rel_bias, tdec, tpad):
    assert _far_bucket_is_last(MOBA_BLOCK + 1)
    bt = rel_bias.T.astype(_F32)
    tok = jnp.arange(tdec, dtype=jnp.int32)[:, None]
    key = jnp.arange(MOBA_BLOCK, dtype=jnp.int32)[None, :]
    near = bt[:, _t5_bucket(MOBA_BLOCK + tok - key)]
    far = jnp.broadcast_to(bt[:, N_BUCKETS - 1][:, None, None], (B_HEADS, tdec, 128))
    new = jnp.arange(tpad, dtype=jnp.int32)[None, :]
    own = jnp.where((tok - new >= 0) & (new < tdec), bt[:, _t5_bucket(jnp.maximum(tok - new, 0))],
                    _NEG)
    flat = lambda a: a.reshape(B_HEADS * tdec, a.shape[-1])
    return flat(near), flat(far), flat(own)


def kernel(x_prompt, x_sample, cache_k, cache_v, state_conv, page_table, w_in, ln_v_g, ln_v_b, w_s,
           b_s, w_pa, w_pb, w_o, norm1_g, norm2_g, w_up, conv_w, conv_b, w_down, rel_bias, norm_f):
    nbp_, seq, _ = x_prompt.shape
    nbs, tdec, _ = x_sample.shape
    depth, n_pool, page_size = cache_k.shape[:3]
    n_pages = page_table.shape[1]
    ns = nbs * tdec
    np_ = nbp_ * seq
    assert page_size == CHUNK and n_pages % _PAGES_PER_STEP == 0 and seq % MOBA_BLOCK == 0
    assert _PAGES_PER_STEP % _PAGES_PER_BLOCK == 0 and seq % _POST_TILE == 0
    assert CONV_W - 1 <= tdec <= CHUNK and ns % CHUNK == 0 and CHUNK % tdec == 0
    tpad = -(-tdec // 8) * 8
    tm_p = _PROJ_TILE
    nb = seq // MOBA_BLOCK
    nb_past = n_pages // _PAGES_PER_BLOCK
    rows = B_HEADS * tdec

    xp = x_prompt.reshape(np_, D_MODEL)
    xs = x_sample.reshape(ns, D_MODEL)
    cache_kt = cache_k.transpose(0, 1, 3, 4, 2).reshape(depth * n_pool, B_WIDTH, page_size)
    cache_vt = cache_v.transpose(0, 1, 3, 4, 2).reshape(depth * n_pool, B_WIDTH, page_size)
    page_flat = page_table.reshape(-1).astype(jnp.int32)

    btab = _prompt_bias_tables(rel_bias)
    bnear_s, bfar_s, bown_s = _sample_bias_tables(rel_bias, tdec, tpad)
    row_head = jnp.arange(rows, dtype=jnp.int32)[:, None] // tdec
    hmask = (jnp.arange(B_WIDTH, dtype=jnp.int32)[None, :] // B_HEAD_DIM == row_head).astype(_F32)
    rsum = (jnp.arange(rows, dtype=jnp.int32)[None, :] % tdec
            == jnp.arange(tdec, dtype=jnp.int32)[:, None]).astype(_F32)
    lane_pad = lambda a: jnp.pad(a, ((0, 0), (0, 128 - rows)))
    bownt_s, hmaskt = lane_pad(bown_s.T), lane_pad(hmask.T)
    rsum_p = lane_pad(jnp.pad(rsum, ((0, tpad - tdec), (0, 0))))
    tpos = jnp.broadcast_to((jnp.arange(ns, dtype=jnp.int32) % tdec)[:, None], (ns, D_FF))
    tril = jnp.tril(jnp.ones((CHUNK, CHUNK), bool))
    idx = jnp.arange(CHUNK, dtype=jnp.int32)
    same_chunk = tril & (idx[:, None] // tdec == idx[None, :] // tdec)
    zero_state = jnp.zeros((nbp_, CONV_W - 1, D_FF), _F32)
    w_in_b, wpa, wpb, wo = (w.astype(_BF16) for w in (w_in, w_pa, w_pb, w_o))
    wup, wdn = w_up.astype(_BF16), w_down.astype(_BF16)
    gf = norm_f[None, :]

    ks_l, vs_l, as_l, cp_l, cs_l = [], [], [], [], []
    yp = ys = kv_nat = None
    for l in range(depth):
        last = l == depth - 1
        g1 = norm1_g[l][None, :]
        lng, lnb = ln_v_g[l][None, :], ln_v_b[l][None, :]
        g2 = norm2_g[l][None, :]
        cw, cb = conv_w[l], conv_b[l][None, :]
        wmix_p = jnp.where(tril[None], w_s[l], 0.0).astype(_BF16)
        bmix_p = jnp.broadcast_to(b_s[l][:, :, None], (A_GROUPS, CHUNK, CHUNK))
        reps = CHUNK // tdec
        wmix_s = jnp.where(same_chunk[None], jnp.tile(w_s[l][:, :tdec, :tdec], (1, reps, reps)),
                           0.0).astype(_BF16)
        bmix_s = jnp.broadcast_to(jnp.tile(b_s[l][:, :tdec], (1, reps))[:, :, None],
                                  (A_GROUPS, CHUNK, CHUNK))

        (qp, kbp, vtp, kmeanp, knat, vnat, yap, sgap, sgbp) = _proj_call(
            xp, g1, w_in_b, lng, lnb, wmix_p, bmix_p, tm=tm_p, prompt=True, layer=l, depth=depth,
            seq=seq, kv_prev=kv_nat)
        kv_nat = (knat, vnat)
        ybp = _moba_prompt_call(qp.reshape(nbp_, seq, B_WIDTH), kbp.reshape(nbp_, seq, -1),
                                vtp, kmeanp.reshape(nbp_, nb, B_WIDTH), btab)
        outs = _post_call(xp, yap, ybp.reshape(np_, B_WIDTH), sgap, sgbp, wpa, wpb, wo, g2,
                          wup, cw, cb, wdn, gf, (zero_state,), layer=l, tm=_POST_TILE,
                          tiles_per_seq=seq // _POST_TILE, per_row_state=False, final_norm=last)
        xp, cp = outs[0], outs[1]
        if last:
            yp = outs[2]

        qs, ks, vs, yas, sgas, sgbs, vas = _proj_call(xs, g1, w_in_b, lng, lnb, wmix_s, bmix_s,
                                                      tm=CHUNK, prompt=False, layer=l)
        ks4 = ks.reshape(nbs, tdec, B_HEADS, B_HEAD_DIM)
        vs4 = vs.reshape(nbs, tdec, B_HEADS, B_HEAD_DIM)
        qbd = (qs.reshape(nbs, 1, tdec, B_WIDTH) * hmask.reshape(B_HEADS, tdec, B_WIDTH)[None]
               ).reshape(nbs, rows, B_WIDTH)
        pad = ((0, 0), (0, tpad - tdec), (0, 0))
        knew = jnp.pad(ks.reshape(nbs, tdec, B_WIDTH), pad)
        vnew = jnp.pad(vs.reshape(nbs, tdec, B_WIDTH), pad)
        ybs = _moba_sample_call(page_flat, qbd, knew, vnew, bnear_s, bfar_s, bownt_s, hmaskt,
                                rsum_p, cache_kt, cache_vt, layer=l, n_pool=n_pool, n_pages=n_pages,
                                tdec=tdec)
        ybs = ybs.reshape(ns, B_WIDTH)
        st = state_conv[l]
        s1 = jnp.broadcast_to(st[:, 1:2, :], (nbs, tdec, D_FF)).reshape(ns, D_FF)
        s2 = jnp.concatenate([st, jnp.zeros((nbs, tdec - (CONV_W - 1), D_FF), _F32)],
                             axis=1).reshape(ns, D_FF)
        outs = _post_call(xs, yas, ybs.astype(_BF16), sgas, sgbs, wpa, wpb, wo,
                          g2, wup, cw, cb, wdn, gf, (s1, s2, tpos), layer=l, tm=CHUNK,
                          tiles_per_seq=1, per_row_state=True, final_norm=last)
        xs, a_s = outs[0], outs[1]
        if last:
            ys = outs[2]

        ks_l.append(ks4)
        vs_l.append(vs4)
        as_l.append(vas.reshape(nbs, tdec, A_WIDTH))
        cp_l.append(cp)
        a_ext = jnp.concatenate([st, a_s.reshape(nbs, tdec, D_FF)], axis=1)
        cs_l.append(a_ext[:, -(CONV_W - 1):])

    def kv_out(t):
        return t.reshape(depth, nbp_, B_HEADS, B_HEAD_DIM, seq).transpose(0, 1, 4, 2, 3)

    return (yp.reshape(nbp_, seq, D_MODEL), ys.reshape(nbs, tdec, D_MODEL),
            kv_out(kv_nat[0]), kv_out(kv_nat[1]),
            jnp.stack(ks_l, 0), jnp.stack(vs_l, 0),
            jnp.stack(as_l, 0), jnp.stack(cp_l, 0), jnp.stack(cs_l, 0))
```

```python
import functools
import math

import jax
import jax.numpy as jnp
from jax import lax
from jax.experimental import pallas as pl
from jax.experimental.pallas import tpu as pltpu

D_MODEL = 1024
CHUNK = 128
A_WIDTH = D_MODEL // 2
A_GROUPS = 4
A_GROUP_DIM = A_WIDTH // A_GROUPS
B_HEADS = 8
B_HEAD_DIM = 64
B_WIDTH = B_HEADS * B_HEAD_DIM
MOBA_BLOCK = 256
MOBA_TOPK = 3
N_BUCKETS = 32
MAX_DISTANCE = 128
D_FF = 2816
CONV_W = 3
EPS = 1e-6
N_IN = 2 * A_WIDTH + 3 * B_WIDTH + 2 * D_MODEL

_C_U, _C_VA, _C_Q, _C_K, _C_V, _C_GA, _C_GB = 0, 512, 1024, 1536, 2048, 2560, 3584

_NEG = -1e30
_HEADS_PER_PAIR = 2
_PAIRS_PER_STEP = 2
_BLOCKS_PER_ITER = 4
_TAB_OWN, _TAB_PREV, _TAB_FAR, _TAB_DEAD = 0, 1, 2, 3
_HEAD_TABLES = 2
_PAIR_LANES = _HEADS_PER_PAIR * B_HEAD_DIM
_K_LANES = 2 * _PAIR_LANES
_V_ROWS = B_HEAD_DIM + 16
_LOG2E = math.log2(math.e)
_PAGES_PER_STEP = 16
_PAGES_PER_BLOCK = MOBA_BLOCK // CHUNK
_PAGE_SLOTS = 3
_PROJ_TILE = 512
_POST_TILE = 512
_FFN_SPLITS = (0, 1536, D_FF)
_VMEM_LIMIT = 48 * 1024 * 1024
_SCALE = 1.0 / math.sqrt(B_HEAD_DIM)

_F32 = jnp.float32
_BF16 = jnp.bfloat16


def _gelu(x):
    k = -2.0 * 0.7978845608028654 * _LOG2E
    return x / (1.0 + jnp.exp2(x * (k + (k * 0.044715) * (x * x))))


def _sigmoid(x):
    return 1.0 / (1.0 + jnp.exp2(x * -_LOG2E))


def _rms(x, g):
    return x * lax.rsqrt(jnp.mean(x * x, axis=-1, keepdims=True) + EPS) * g


def _const_spec(shape):
    zeros = (0,) * len(shape)
    return pl.BlockSpec(shape, lambda *_: zeros, pipeline_mode=pl.Buffered(1))


def _layer_spec(shape, layer):
    index = (layer,) + (0,) * len(shape)
    return pl.BlockSpec((None,) + shape, lambda *_: index, pipeline_mode=pl.Buffered(1))


def _proj_kernel(x_ref, g1_ref, w_ref, lng_ref, lnb_ref, wmix_ref, bmix_ref, *rest, tm, prompt,
                 nb=None):
    if prompt:
        (q_ref, kb_ref, vt_ref, kmean_ref, knat_ref, vnat_ref, ya_ref, sga_ref, sgb_ref) = rest[-9:]
    else:
        (q_ref, k_ref, v_ref, ya_ref, sga_ref, sgb_ref, va_ref) = rest
    xn = _rms(x_ref[...], g1_ref[...]).astype(_BF16)

    def proj(lo, width):
        return jnp.dot(xn, w_ref[:, lo:lo + width], preferred_element_type=_F32)

    u = _gelu(proj(_C_U, A_WIDTH))
    va = _gelu(proj(_C_VA, A_WIDTH))
    mu = jnp.mean(va, axis=-1, keepdims=True)
    vc = va - mu
    var = jnp.mean(vc * vc, axis=-1, keepdims=True)
    va = vc * lax.rsqrt(var + EPS) * lng_ref[...] + lnb_ref[...]
    if not prompt:
        va_ref[...] = va
    for c in range(tm // CHUNK):
        rows = slice(c * CHUNK, (c + 1) * CHUNK)
        for g in range(A_GROUPS):
            cols = slice(g * A_GROUP_DIM, (g + 1) * A_GROUP_DIM)
            mixed = jnp.dot(wmix_ref[g], va[rows, cols].astype(_BF16),
                            preferred_element_type=_F32) + bmix_ref[g]
            ya_ref[rows, cols] = (u[rows, cols] * mixed).astype(_BF16)

    q_ref[...] = proj(_C_Q, B_WIDTH)
    k = proj(_C_K, B_WIDTH)
    v = proj(_C_V, B_WIDTH)
    if prompt:
        blocks = tm // MOBA_BLOCK
        vt = v.T
        knat_ref[0] = k.T
        vnat_ref[0] = vt
        for r in range(blocks):
            kmean_ref[0, r:r + 1, :] = jnp.mean(k[r * MOBA_BLOCK:(r + 1) * MOBA_BLOCK], axis=0,
                                                keepdims=True)
        kb = k.astype(_BF16)
        first_blk = (pl.program_id(0) * blocks) % nb
        shape = (tm, _K_LANES - _PAIR_LANES)
        blk_of_row = first_blk + lax.broadcasted_iota(jnp.int32, shape, 0) // MOBA_BLOCK
        onehot = jnp.where(lax.broadcasted_iota(jnp.int32, shape, 1) == blk_of_row, 1.0, 0.0
                           ).astype(_BF16)
        for hp in range(B_HEADS // _HEADS_PER_PAIR):
            kb_ref[:, hp * _K_LANES:hp * _K_LANES + _PAIR_LANES] = (
                kb[:, hp * _PAIR_LANES:(hp + 1) * _PAIR_LANES])
            kb_ref[:, hp * _K_LANES + _PAIR_LANES:(hp + 1) * _K_LANES] = onehot
        vtb = vt.astype(_BF16)
        row = lax.broadcasted_iota(jnp.int32, (_V_ROWS - B_HEAD_DIM, MOBA_BLOCK), 0)
        ones_row = jnp.where(row == 0, 1.0, 0.0).astype(_BF16)
        for r in range(blocks):
            keys = slice(r * MOBA_BLOCK, (r + 1) * MOBA_BLOCK)
            for h in range(B_HEADS):
                vt_ref[0, r, h * _V_ROWS:h * _V_ROWS + B_HEAD_DIM, :] = (
                    vtb[h * B_HEAD_DIM:(h + 1) * B_HEAD_DIM, keys])
                vt_ref[0, r, h * _V_ROWS + B_HEAD_DIM:(h + 1) * _V_ROWS, :] = ones_row
    else:
        k_ref[...] = k
        v_ref[...] = v
    sga_ref[...] = _sigmoid(proj(_C_GA, D_MODEL)).astype(_BF16)
    sgb_ref[...] = _sigmoid(proj(_C_GB, D_MODEL)).astype(_BF16)


def _proj_call(x2, g1, w_in, lng, lnb, wmix, bmix, *, tm, prompt, layer=0, depth=1, seq=None,
               kv_prev=None):
    n = x2.shape[0]
    nt = n // tm
    row = lambda width: pl.BlockSpec((tm, width), lambda i: (i, 0))
    in_specs = [row(D_MODEL), _const_spec((1, D_MODEL)), _layer_spec((D_MODEL, N_IN), layer),
                _const_spec((1, A_WIDTH)), _const_spec((1, A_WIDTH)),
                _layer_spec((A_GROUPS, CHUNK, CHUNK), layer),
                _layer_spec((A_GROUPS, CHUNK, CHUNK), layer)]
    args = [x2, g1, w_in, lng, lnb, wmix, bmix]
    aliases = {}
    tail_shape = [jax.ShapeDtypeStruct((n, A_WIDTH), _BF16),
                  jax.ShapeDtypeStruct((n, D_MODEL), _BF16),
                  jax.ShapeDtypeStruct((n, D_MODEL), _BF16)]
    tail_specs = [row(A_WIDTH), row(D_MODEL), row(D_MODEL)]
    nb = None
    if prompt:
        nb = seq // MOBA_BLOCK
        blocks = tm // MOBA_BLOCK
        tps = seq // tm
        assert tm % MOBA_BLOCK == 0 and seq % tm == 0 and nb <= _K_LANES - _PAIR_LANES
        nseq = n // seq
        k_cols = B_HEADS // _HEADS_PER_PAIR * _K_LANES
        nat = jax.ShapeDtypeStruct((depth * nseq, B_WIDTH, seq), _F32)
        nat_spec = pl.BlockSpec((1, B_WIDTH, tm), lambda i: (layer * nseq + i // tps, 0, i % tps))
        out_shape = [jax.ShapeDtypeStruct((n, B_WIDTH), _F32),
                     jax.ShapeDtypeStruct((n, k_cols), _BF16),
                     jax.ShapeDtypeStruct((nseq, nb, B_HEADS * _V_ROWS, MOBA_BLOCK), _BF16),
                     jax.ShapeDtypeStruct((nt, blocks, B_WIDTH), _F32), nat, nat] + tail_shape
        out_specs = [row(B_WIDTH), row(k_cols),
                     pl.BlockSpec((1, blocks, B_HEADS * _V_ROWS, MOBA_BLOCK),
                                  lambda i: (i // tps, i % tps, 0, 0)),
                     pl.BlockSpec((1, blocks, B_WIDTH), lambda i: (i, 0, 0)), nat_spec, nat_spec
                     ] + tail_specs
        if kv_prev is not None:
            in_specs += [pl.BlockSpec(memory_space=pl.ANY)] * 2
            args += list(kv_prev)
            aliases = {7: 4, 8: 5}
    else:
        out_shape = [jax.ShapeDtypeStruct((n, B_WIDTH), _F32)] * 3 + tail_shape + [
            jax.ShapeDtypeStruct((n, A_WIDTH), _F32)]
        out_specs = [row(B_WIDTH)] * 3 + tail_specs + [row(A_WIDTH)]
    return pl.pallas_call(
        functools.partial(_proj_kernel, tm=tm, prompt=prompt, nb=nb),
        grid=(nt,),
        in_specs=in_specs,
        out_specs=out_specs,
        out_shape=out_shape,
        input_output_aliases=aliases,
        compiler_params=pltpu.CompilerParams(dimension_semantics=("arbitrary",),
                                             vmem_limit_bytes=_VMEM_LIMIT),
        name="proj_mixer_a",
    )(*args)


def _topk_block_mask(gate, own, nb, width):
    blk = lax.broadcasted_iota(jnp.int32, (nb, width), 0)
    gm = jnp.where(blk < own, gate, -jnp.inf)
    rank = jnp.zeros((nb, width), _F32)
    for jp in range(nb):
        r = gm[jp:jp + 1, :]
        beats = jnp.where(r > gm, 1.0, jnp.where(r == gm, jnp.where(blk > jp, 1.0, 0.0), 0.0))
        rank = rank + beats
    past_sel = jnp.where(blk < own, jnp.where(rank < MOBA_TOPK, 1.0, 0.0), 0.0)
    return jnp.where(blk == own, 1.0, past_sel)


def _moba_prompt_kernel(q_ref, k_ref, vt_ref, kmean_ref, btab_ref, o_ref, s_scr, *, nb):
    i = pl.program_id(2)
    blk = MOBA_BLOCK
    heads = range(_PAIRS_PER_STEP * _HEADS_PER_PAIR)

    qt = q_ref[0].T
    feat = lax.broadcasted_iota(jnp.int32, (_PAIR_LANES, blk), 0)
    qz = []
    for hh in heads:
        pair, sub = divmod(hh, _HEADS_PER_PAIR)
        in_head = (feat >= sub * B_HEAD_DIM) & (feat < (sub + 1) * B_HEAD_DIM)
        qh = jnp.where(in_head, qt[pair * _PAIR_LANES:(pair + 1) * _PAIR_LANES], 0.0)
        gate = jnp.dot(kmean_ref[0, :, pair * _PAIR_LANES:(pair + 1) * _PAIR_LANES], qh,
                       preferred_element_type=_F32, precision=lax.Precision.HIGHEST)
        mask_rows = jnp.where(_topk_block_mask(gate, i, nb, blk) > 0.5, 0.0, _NEG)
        pad = jnp.zeros((_K_LANES - _PAIR_LANES - nb, blk), _F32)
        qz.append(jnp.concatenate([qh * (_SCALE * _LOG2E), mask_rows, pad], axis=0).astype(_BF16))

    n_iter = lax.div(i + _BLOCKS_PER_ITER, _BLOCKS_PER_ITER)

    first_head = pl.program_id(1) * len(heads)

    def slots(it):
        js = [it * _BLOCKS_PER_ITER + u for u in range(_BLOCKS_PER_ITER)]
        kinds = [jnp.where(j == i, _TAB_OWN, jnp.where(j == i - 1, _TAB_PREV,
                                                       jnp.where(j > i, _TAB_DEAD, _TAB_FAR)))
                 for j in js]
        return js, [jnp.minimum(j, i) for j in js], kinds

    def table_row(hh, kind):
        per_head = (first_head + hh) * _HEAD_TABLES + kind
        return jnp.where(kind < _HEAD_TABLES, per_head, B_HEADS * _HEAD_TABLES + kind - _HEAD_TABLES)

    def logits_pass(it, ms, group):
        js, jcs, tabs = slots(it)
        out = []
        for hh, m in zip(group, ms):
            pair = hh // _HEADS_PER_PAIR
            for j, jc, tab in zip(js, jcs, tabs):
                kj = k_ref[0, pl.ds(pl.multiple_of(jc * blk, blk), blk),
                           pair * _K_LANES:(pair + 1) * _K_LANES]
                s = jnp.dot(kj, qz[hh], preferred_element_type=_F32) + btab_ref[table_row(hh, tab)]
                s_scr[hh, j] = s
                m = jnp.maximum(m, jnp.max(s, axis=0, keepdims=True))
            out.append(m)
        return tuple(out)

    def value_pass(it, accs, group, ms):
        js, jcs, _ = slots(it)
        out = []
        for hh, acc, m in zip(group, accs, ms):
            for j, jc in zip(js, jcs):
                p = jnp.exp2(s_scr[hh, j] - m).astype(_BF16)
                acc = acc + jnp.dot(vt_ref[0, jc, hh * _V_ROWS:(hh + 1) * _V_ROWS, :], p,
                                    preferred_element_type=_F32)
            out.append(acc)
        return tuple(out)

    group = tuple(heads)
    ms = lax.fori_loop(0, n_iter, lambda it, ms: logits_pass(it, ms, group),
                       tuple(jnp.full((1, blk), _NEG, _F32) for _ in group))
    accs = lax.fori_loop(0, n_iter, lambda it, accs: value_pass(it, accs, group, ms),
                         tuple(jnp.zeros((_V_ROWS, blk), _F32) for _ in group))
    outs = [acc[:B_HEAD_DIM] / acc[B_HEAD_DIM:B_HEAD_DIM + 1] for acc in accs]
    o_ref[0] = jnp.concatenate(outs, axis=0).T.astype(_BF16)


def _moba_prompt_call(q3, kb3, vt4, kmean3, btab):
    b, t, _ = q3.shape
    nb = t // MOBA_BLOCK
    g = _PAIRS_PER_STEP
    heads = g * _HEADS_PER_PAIR
    return pl.pallas_call(
        functools.partial(_moba_prompt_kernel, nb=nb),
        grid=(b, B_HEADS // heads, nb),
        in_specs=[pl.BlockSpec((1, MOBA_BLOCK, g * _PAIR_LANES), lambda bb, hp, i: (bb, i, hp)),
                  pl.BlockSpec((1, t, g * _K_LANES), lambda bb, hp, i: (bb, 0, hp)),
                  pl.BlockSpec((1, nb, heads * _V_ROWS, MOBA_BLOCK),
                               lambda bb, hp, i: (bb, 0, hp, 0)),
                  pl.BlockSpec((1, nb, g * _PAIR_LANES), lambda bb, hp, i: (bb, 0, hp)),
                  _const_spec(btab.shape)],
        out_specs=pl.BlockSpec((1, MOBA_BLOCK, g * _PAIR_LANES), lambda bb, hp, i: (bb, i, hp)),
        out_shape=jax.ShapeDtypeStruct((b, t, B_WIDTH), _BF16),
        scratch_shapes=[pltpu.VMEM((heads, nb + _BLOCKS_PER_ITER - 1, MOBA_BLOCK, MOBA_BLOCK), _F32)],
        compiler_params=pltpu.CompilerParams(
            dimension_semantics=("arbitrary", "arbitrary", "arbitrary"),
            vmem_limit_bytes=_VMEM_LIMIT),
        name="moba_prompt",
    )(q3, kb3, vt4, kmean3, btab)


def _moba_sample_kernel(pt_ref, qbd_ref, knew_ref, vnew_ref, bnear_ref, bfar_ref,
                        bownt_ref, hmaskt_ref, rsum_ref, ck_hbm, cv_hbm, o_ref,
                        gate_scr, m_scr, l_scr, acc_scr, st_scr, qbdt_scr, kbuf, vbuf, sems,
                        *, nb_past, page_base):
    pages = _PAGES_PER_STEP
    c = pl.program_id(1)
    step = pl.program_id(0) * pl.num_programs(1) + c
    last_step = pl.num_programs(0) * pl.num_programs(1) - 1
    rows = qbd_ref.shape[1]
    tdec = o_ref.shape[1]
    blocks_per_step = pages // _PAGES_PER_BLOCK

    ahead = _PAGE_SLOTS - 1

    def slot_of(s):
        return lax.rem(s, _PAGE_SLOTS)

    def page_copy(kind, of_step, p):
        src_step = jnp.minimum(of_step, last_step)
        src, buf = ((ck_hbm, kbuf), (cv_hbm, vbuf))[kind]
        page = page_base + pt_ref[src_step * pages + p]
        slot = slot_of(of_step)
        return pltpu.make_async_copy(src.at[page], buf.at[slot, p], sems.at[slot, kind])

    def block_pages(bl):
        return range(bl * _PAGES_PER_BLOCK, (bl + 1) * _PAGES_PER_BLOCK)

    @pl.when(step == 0)
    def _():
        for s in range(ahead):
            for kind in range(2):
                for pg in range(pages):
                    page_copy(kind, s, pg).start()

    for kind in range(2):
        for pg in range(pages):
            page_copy(kind, step, pg).wait()
    slot = slot_of(step)

    def lanes_t(x):
        return jnp.concatenate([x, jnp.zeros((128 - x.shape[0], x.shape[1]), x.dtype)], axis=0).T

    qbd = qbd_ref[0]
    qz = (qbd * _SCALE).astype(_BF16)

    @pl.when(c == 0)
    def _():
        qbdt_scr[...] = lanes_t(qbd)

    qbdt = qbdt_scr[...]
    row_pad = jnp.zeros((128 - rows, MOBA_BLOCK), _F32)

    for bl in range(blocks_per_step):
        jg = c * blocks_per_step + bl
        kts = [kbuf[slot, pg] for pg in block_pages(bl)]
        kt = jnp.concatenate(kts, axis=1)
        kmean = jnp.sum(functools.reduce(lambda a, b: a + b, kts), axis=-1,
                        keepdims=True) * (1.0 / MOBA_BLOCK)
        gate_scr[pl.ds(jg, 1), :] = jnp.sum(qbdt * kmean, axis=0, keepdims=True)
        s = jnp.dot(qz, kt.astype(_BF16), preferred_element_type=_F32)
        s = s + jnp.where(jg == nb_past - 1, bnear_ref[...], bfar_ref[:, 0:1])
        st_scr[bl] = jnp.concatenate([s, row_pad], axis=0).T
        for pg in block_pages(bl):
            page_copy(0, step + ahead, pg).start()
    for bl in range(blocks_per_step):
        jg = c * blocks_per_step + bl
        vt = jnp.concatenate([vbuf[slot, pg] for pg in block_pages(bl)], axis=1)
        st = st_scr[bl]
        m = jnp.max(st, axis=0, keepdims=True)
        p = jnp.exp(st - m)
        m_scr[pl.ds(jg, 1), :] = m
        l_scr[pl.ds(jg, 1), :] = jnp.sum(p, axis=0, keepdims=True)
        acc_scr[jg] = jnp.dot(vt.astype(_BF16), p.astype(_BF16), preferred_element_type=_F32)
        for pg in block_pages(bl):
            page_copy(1, step + ahead, pg).start()

    @pl.when(step == last_step)
    def _():
        for s in range(1, ahead + 1):
            for kind in range(2):
                for pg in range(pages):
                    page_copy(kind, last_step + s, pg).wait()

    @pl.when(c == pl.num_programs(1) - 1)
    def _():
        sel = _topk_block_mask(gate_scr[...], nb_past, nb_past, 128) > 0.5
        m_blk = m_scr[...]
        s_own = jnp.dot(knew_ref[0], qbdt * _SCALE, preferred_element_type=_F32) + bownt_ref[...]
        m_tot = jnp.maximum(jnp.max(s_own, axis=0, keepdims=True),
                            jnp.max(jnp.where(sel, m_blk, _NEG), axis=0, keepdims=True))
        w = jnp.where(sel, jnp.exp(m_blk - m_tot), 0.0)
        p_own = jnp.exp(s_own - m_tot)
        l_tot = jnp.sum(p_own, axis=0, keepdims=True) + jnp.sum(w * l_scr[...], axis=0, keepdims=True)
        p_own = jnp.concatenate([p_own, jnp.zeros((128 - p_own.shape[0], 128), _F32)], axis=0)
        accs = [jnp.dot(lanes_t(vnew_ref[0]), p_own, preferred_element_type=_F32)]
        accs += [jnp.zeros_like(accs[0]) for _ in range(3)]
        for j in range(nb_past):
            accs[j % 4] = accs[j % 4] + w[j:j + 1, :] * acc_scr[j]
        out_t = ((accs[0] + accs[1]) + (accs[2] + accs[3])) / l_tot
        folded = lax.dot_general(rsum_ref[...], out_t * hmaskt_ref[...], (((1,), (1,)), ((), ())),
                                 preferred_element_type=_F32, precision=lax.Precision.HIGHEST)
        o_ref[0] = folded[:tdec]


def _moba_sample_call(page_flat, qbd, knew, vnew, bnear, bfar, bownt, hmaskt, rsum,
                      cache_kt, cache_vt, *, layer, n_pool, n_pages, tdec):
    nbatch, rows, _ = qbd.shape
    pages = _PAGES_PER_STEP
    nb_past = n_pages // _PAGES_PER_BLOCK
    assert nb_past % 8 == 0 and rows <= 128
    assert nbatch * (n_pages // pages) >= _PAGE_SLOTS

    per_batch = lambda arr: pl.BlockSpec((1,) + arr.shape[1:], lambda bb, cc, pt: (bb, 0, 0))
    const = lambda arr: pl.BlockSpec(arr.shape, lambda bb, cc, pt: (0,) * arr.ndim)
    page_ring = pltpu.VMEM((_PAGE_SLOTS, pages, B_WIDTH, CHUNK), _F32)
    grid_spec = pltpu.PrefetchScalarGridSpec(
        num_scalar_prefetch=1,
        grid=(nbatch, n_pages // pages),
        in_specs=[per_batch(qbd), per_batch(knew), per_batch(vnew),
                  const(bnear), const(bfar), const(bownt), const(hmaskt), const(rsum),
                  pl.BlockSpec(memory_space=pl.ANY), pl.BlockSpec(memory_space=pl.ANY)],
        out_specs=pl.BlockSpec((1, tdec, B_WIDTH), lambda bb, cc, pt: (bb, 0, 0)),
        scratch_shapes=[pltpu.VMEM((nb_past, 128), _F32),
                        pltpu.VMEM((nb_past, 128), _F32),
                        pltpu.VMEM((nb_past, 128), _F32),
                        pltpu.VMEM((nb_past, B_WIDTH, 128), _F32),
                        pltpu.VMEM((pages // _PAGES_PER_BLOCK, MOBA_BLOCK, 128), _F32),
                        pltpu.VMEM((B_WIDTH, 128), _F32),
                        page_ring, page_ring,
                        pltpu.SemaphoreType.DMA((_PAGE_SLOTS, 2))])
    return pl.pallas_call(
        functools.partial(_moba_sample_kernel, nb_past=nb_past, page_base=layer * n_pool),
        grid_spec=grid_spec,
        out_shape=jax.ShapeDtypeStruct((nbatch, tdec, B_WIDTH), _F32),
        compiler_params=pltpu.CompilerParams(dimension_semantics=("arbitrary", "arbitrary"),
                                             vmem_limit_bytes=_VMEM_LIMIT),
        name="moba_sample",
    )(page_flat, qbd, knew, vnew, bnear, bfar, bownt, hmaskt, rsum, cache_kt, cache_vt)


def _post_kernel(x_ref, ya_ref, yb_ref, sga_ref, sgb_ref, wpa_ref, wpb_ref, wo_ref, g2_ref,
                 wup_ref, cw_ref, cb_ref, wdn_ref, gf_ref, *rest, tm, tiles_per_seq, per_row_state,
                 final_norm):
    if per_row_state:
        s1_ref, s2_ref, tpos_ref = rest[:3]
        rest = rest[3:]
    else:
        st_ref = rest[0]
        rest = rest[1:]
    xo_ref, a_ref = rest[:2]
    rest = rest[2:]
    if final_norm:
        y_ref = rest[0]
        rest = rest[1:]
    if not per_row_state:
        carry_scr = rest[0]

    ma = jnp.dot(ya_ref[...], wpa_ref[...], preferred_element_type=_F32)
    mb = jnp.dot(yb_ref[...], wpb_ref[...], preferred_element_type=_F32)
    merged = sga_ref[...].astype(_F32) * ma + sgb_ref[...].astype(_F32) * mb
    x1 = x_ref[...] + jnp.dot(merged.astype(_BF16), wo_ref[...], preferred_element_type=_F32)

    xn = _rms(x1, g2_ref[...]).astype(_BF16)
    if not per_row_state:
        @pl.when(pl.program_id(0) % tiles_per_seq == 0)
        def _():
            carry_scr[...] = st_ref[0]

    x2 = x1
    for lo, hi in zip(_FFN_SPLITS[:-1], _FFN_SPLITS[1:]):
        cols = slice(lo, hi)
        width = hi - lo
        a = jnp.dot(xn, wup_ref[:, cols], preferred_element_type=_F32)
        gate = jnp.dot(xn, wup_ref[:, D_FF + lo:D_FF + hi], preferred_element_type=_F32)
        prev1 = pltpu.roll(a, 1, 0)
        prev2 = pltpu.roll(a, 2, 0)
        if per_row_state:
            tpos = tpos_ref[:, cols]
            prev1 = jnp.where(tpos >= 1, prev1, s1_ref[:, cols])
            prev2 = jnp.where(tpos >= 2, prev2, s2_ref[:, cols])
            a_ref[:, cols] = a
        else:
            row = lax.broadcasted_iota(jnp.int32, (8, width), 0)
            c0 = carry_scr[0:1, cols]
            c1 = carry_scr[1:2, cols]
            top1 = jnp.where(row == 0, c1, prev1[:8])
            top2 = jnp.where(row == 0, c0, jnp.where(row == 1, c1, prev2[:8]))
            prev1 = jnp.concatenate([top1, prev1[8:]], axis=0)
            prev2 = jnp.concatenate([top2, prev2[8:]], axis=0)
            tail = a[tm - (CONV_W - 1):, :]
            carry_scr[:, cols] = tail
            a_ref[0, :, cols] = tail
        conv = (cb_ref[:, cols] + prev2 * cw_ref[0:1, cols] + prev1 * cw_ref[1:2, cols]
                + a * cw_ref[2:3, cols])
        act = (_gelu(conv) * gate).astype(_BF16)
        x2 = x2 + jnp.dot(act, wdn_ref[cols, :], preferred_element_type=_F32)
    xo_ref[...] = x2
    if final_norm:
        y_ref[...] = _rms(x2, gf_ref[...])


def _post_call(x2, ya, yb, sga, sgb, wpa, wpb, wo, g2, wup, cw, cb, wdn, gf, state_args, *,
               layer, tm, tiles_per_seq, per_row_state, final_norm):
    n = x2.shape[0]
    row = lambda width: pl.BlockSpec((tm, width), lambda i: (i, 0))
    in_specs = [row(D_MODEL), row(A_WIDTH), row(B_WIDTH), row(D_MODEL), row(D_MODEL),
                _layer_spec((A_WIDTH, D_MODEL), layer), _layer_spec((B_WIDTH, D_MODEL), layer),
                _layer_spec((D_MODEL, D_MODEL), layer), _const_spec((1, D_MODEL)),
                _layer_spec((D_MODEL, 2 * D_FF), layer), _const_spec((CONV_W, D_FF)),
                _const_spec((1, D_FF)), _layer_spec((D_FF, D_MODEL), layer),
                _const_spec((1, D_MODEL))]
    out_shape = [jax.ShapeDtypeStruct((n, D_MODEL), _F32)]
    out_specs = [row(D_MODEL)]
    scratch = []
    if per_row_state:
        in_specs += [row(D_FF), row(D_FF), row(D_FF)]
        out_shape.append(jax.ShapeDtypeStruct((n, D_FF), _F32))
        out_specs.append(row(D_FF))
    else:
        nseq = n // (tm * tiles_per_seq)
        in_specs.append(pl.BlockSpec((1, CONV_W - 1, D_FF), lambda i: (i // tiles_per_seq, 0, 0)))
        out_shape.append(jax.ShapeDtypeStruct((nseq, CONV_W - 1, D_FF), _F32))
        out_specs.append(pl.BlockSpec((1, CONV_W - 1, D_FF), lambda i: (i // tiles_per_seq, 0, 0)))
        scratch.append(pltpu.VMEM((CONV_W - 1, D_FF), _F32))
    if final_norm:
        out_shape.append(jax.ShapeDtypeStruct((n, D_MODEL), _F32))
        out_specs.append(row(D_MODEL))
    return pl.pallas_call(
        functools.partial(_post_kernel, tm=tm, tiles_per_seq=tiles_per_seq,
                          per_row_state=per_row_state, final_norm=final_norm),
        grid=(n // tm,),
        in_specs=in_specs,
        out_specs=out_specs,
        out_shape=out_shape,
        scratch_shapes=scratch,
        compiler_params=pltpu.CompilerParams(dimension_semantics=("arbitrary",),
                                             vmem_limit_bytes=_VMEM_LIMIT),
        name="merge_convffn",
    )(x2, ya, yb, sga, sgb, wpa, wpb, wo, g2, wup, cw, cb, wdn, gf, *state_args)


def _t5_bucket(rel):
    max_exact = N_BUCKETS // 2
    relf = jnp.maximum(rel, 1).astype(_F32)
    large = max_exact + (jnp.log(relf / max_exact) / math.log(MAX_DISTANCE / max_exact)
                         * (N_BUCKETS - max_exact)).astype(jnp.int32)
    large = jnp.minimum(large, N_BUCKETS - 1)
    return jnp.where(rel < max_exact, rel, large)


def _far_bucket_is_last(min_rel):
    max_exact = N_BUCKETS // 2
    v = max_exact + int(math.log(min_rel / max_exact) / math.log(MAX_DISTANCE / max_exact)
                        * (N_BUCKETS - max_exact) - 1e-6)
    return v >= N_BUCKETS - 1


def _toeplitz(diag, n):
    width = 2 * n
    flat = jnp.tile(diag, (1,) * (diag.ndim - 1) + (n,))
    skew = flat[..., :n * (width - 1)].reshape(diag.shape[:-1] + (n, width - 1))
    return skew[..., n - 1:2 * n - 1]


def _prompt_bias_tables(rel_bias):
    assert _far_bucket_is_last(MOBA_BLOCK + 1)
    assert (_TAB_OWN, _TAB_PREV, _TAB_FAR, _TAB_DEAD) == (0, 1, 2, 3) and _HEAD_TABLES == 2
    bt = rel_bias.T.astype(_F32)
    bt = (bt - bt[:, N_BUCKETS - 1:]) * _LOG2E
    rel = jnp.arange(2 * MOBA_BLOCK, dtype=jnp.int32) - (MOBA_BLOCK - 1)
    own = jnp.where(rel >= 0, bt[:, _t5_bucket(jnp.maximum(rel, 0))], _NEG)
    prev = bt[:, _t5_bucket(rel + MOBA_BLOCK)]
    per_head = jnp.stack([_toeplitz(own, MOBA_BLOCK), _toeplitz(prev, MOBA_BLOCK)], axis=1)
    per_head = per_head.reshape(B_HEADS * _HEAD_TABLES, MOBA_BLOCK, MOBA_BLOCK)
    shared = jnp.stack([jnp.zeros((MOBA_BLOCK, MOBA_BLOCK), _F32),
                        jnp.full((MOBA_BLOCK, MOBA_BLOCK), _NEG, _F32)])
    return jnp.concatenate([per_head, shared], axis=0)


def _sample_bias_tables(rel_bias, tdec, tpad):
    assert _far_bucket_is_last(MOBA_BLOCK + 1)
    bt = rel_bias.T.astype(_F32)
    tok = jnp.arange(tdec, dtype=jnp.int32)[:, None]
    key = jnp.arange(MOBA_BLOCK, dtype=jnp.int32)[None, :]
    near = bt[:, _t5_bucket(MOBA_BLOCK + tok - key)]
    far = jnp.broadcast_to(bt[:, N_BUCKETS - 1][:, None, None], (B_HEADS, tdec, 128))
    new = jnp.arange(tpad, dtype=jnp.int32)[None, :]
    own = jnp.where((tok - new >= 0) & (new < tdec), bt[:, _t5_bucket(jnp.maximum(tok - new, 0))],
                    _NEG)
    flat = lambda a: a.reshape(B_HEADS * tdec, a.shape[-1])
    return flat(near), flat(far), flat(own)


def kernel(x_prompt, x_sample, cache_k, cache_v, state_conv, page_table, w_in, ln_v_g, ln_v_b, w_s,
           b_s, w_pa, w_pb, w_o, norm1_g, norm2_g, w_up, conv_w, conv_b, w_down, rel_bias, norm_f):
    nbp_, seq, _ = x_prompt.shape
    nbs, tdec, _ = x_sample.shape
    depth, n_pool, page_size = cache_k.shape[:3]
    n_pages = page_table.shape[1]
    ns = nbs * tdec
    np_ = nbp_ * seq
    assert page_size == CHUNK and n_pages % _PAGES_PER_STEP == 0 and seq % MOBA_BLOCK == 0
    assert _PAGES_PER_STEP % _PAGES_PER_BLOCK == 0 and seq % _POST_TILE == 0
    assert CONV_W - 1 <= tdec <= CHUNK and ns % CHUNK == 0 and CHUNK % tdec == 0
    tpad = -(-tdec // 8) * 8
    tm_p = _PROJ_TILE
    nb = seq // MOBA_BLOCK
    nb_past = n_pages // _PAGES_PER_BLOCK
    rows = B_HEADS * tdec

    xp = x_prompt.reshape(np_, D_MODEL)
    xs = x_sample.reshape(ns, D_MODEL)
    cache_kt = cache_k.transpose(0, 1, 3, 4, 2).reshape(depth * n_pool, B_WIDTH, page_size)
    cache_vt = cache_v.transpose(0, 1, 3, 4, 2).reshape(depth * n_pool, B_WIDTH, page_size)
    page_flat = page_table.reshape(-1).astype(jnp.int32)

    btab = _prompt_bias_tables(rel_bias)
    bnear_s, bfar_s, bown_s = _sample_bias_tables(rel_bias, tdec, tpad)
    row_head = jnp.arange(rows, dtype=jnp.int32)[:, None] // tdec
    hmask = (jnp.arange(B_WIDTH, dtype=jnp.int32)[None, :] // B_HEAD_DIM == row_head).astype(_F32)
    rsum = (jnp.arange(rows, dtype=jnp.int32)[None, :] % tdec
            == jnp.arange(tdec, dtype=jnp.int32)[:, None]).astype(_F32)
    lane_pad = lambda a: jnp.pad(a, ((0, 0), (0, 128 - rows)))
    bownt_s, hmaskt = lane_pad(bown_s.T), lane_pad(hmask.T)
    rsum_p = lane_pad(jnp.pad(rsum, ((0, tpad - tdec), (0, 0))))
    tpos = jnp.broadcast_to((jnp.arange(ns, dtype=jnp.int32) % tdec)[:, None], (ns, D_FF))
    tril = jnp.tril(jnp.ones((CHUNK, CHUNK), bool))
    idx = jnp.arange(CHUNK, dtype=jnp.int32)
    same_chunk = tril & (idx[:, None] // tdec == idx[None, :] // tdec)
    zero_state = jnp.zeros((nbp_, CONV_W - 1, D_FF), _F32)
    w_in_b, wpa, wpb, wo = (w.astype(_BF16) for w in (w_in, w_pa, w_pb, w_o))
    wup, wdn = w_up.astype(_BF16), w_down.astype(_BF16)
    gf = norm_f[None, :]
    mix_shape = (depth, A_GROUPS, CHUNK, CHUNK)
    reps = CHUNK // tdec
    wmix_p = jnp.where(tril, w_s, 0.0).astype(_BF16)
    bmix_p = jnp.broadcast_to(b_s[..., None], mix_shape)
    wmix_s = jnp.where(same_chunk, jnp.tile(w_s[:, :, :tdec, :tdec], (1, 1, reps, reps)),
                       0.0).astype(_BF16)
    bmix_s = jnp.broadcast_to(jnp.tile(b_s[:, :, :tdec], (1, 1, reps))[..., None], mix_shape)

    ks_l, vs_l, as_l, cp_l, cs_l = [], [], [], [], []
    yp = ys = kv_nat = None
    for l in range(depth):
        last = l == depth - 1
        g1 = norm1_g[l][None, :]
        lng, lnb = ln_v_g[l][None, :], ln_v_b[l][None, :]
        g2 = norm2_g[l][None, :]
        cw, cb = conv_w[l], conv_b[l][None, :]

        (qp, kbp, vtp, kmeanp, knat, vnat, yap, sgap, sgbp) = _proj_call(
            xp, g1, w_in_b, lng, lnb, wmix_p, bmix_p, tm=tm_p, prompt=True, layer=l, depth=depth,
            seq=seq, kv_prev=kv_nat)
        kv_nat = (knat, vnat)
        ybp = _moba_prompt_call(qp.reshape(nbp_, seq, B_WIDTH), kbp.reshape(nbp_, seq, -1),
                                vtp, kmeanp.reshape(nbp_, nb, B_WIDTH), btab)
        outs = _post_call(xp, yap, ybp.reshape(np_, B_WIDTH), sgap, sgbp, wpa, wpb, wo, g2,
                          wup, cw, cb, wdn, gf, (zero_state,), layer=l, tm=_POST_TILE,
                          tiles_per_seq=seq // _POST_TILE, per_row_state=False, final_norm=last)
        xp, cp = outs[0], outs[1]
        if last:
            yp = outs[2]

        qs, ks, vs, yas, sgas, sgbs, vas = _proj_call(xs, g1, w_in_b, lng, lnb, wmix_s, bmix_s,
                                                      tm=CHUNK, prompt=False, layer=l)
        ks4 = ks.reshape(nbs, tdec, B_HEADS, B_HEAD_DIM)
        vs4 = vs.reshape(nbs, tdec, B_HEADS, B_HEAD_DIM)
        qbd = (qs.reshape(nbs, 1, tdec, B_WIDTH) * hmask.reshape(B_HEADS, tdec, B_WIDTH)[None]
               ).reshape(nbs, rows, B_WIDTH)
        pad = ((0, 0), (0, tpad - tdec), (0, 0))
        knew = jnp.pad(ks.reshape(nbs, tdec, B_WIDTH), pad)
        vnew = jnp.pad(vs.reshape(nbs, tdec, B_WIDTH), pad)
        ybs = _moba_sample_call(page_flat, qbd, knew, vnew, bnear_s, bfar_s, bownt_s, hmaskt,
                                rsum_p, cache_kt, cache_vt, layer=l, n_pool=n_pool, n_pages=n_pages,
                                tdec=tdec)
        ybs = ybs.reshape(ns, B_WIDTH)
        st = state_conv[l]
        s1 = jnp.broadcast_to(st[:, 1:2, :], (nbs, tdec, D_FF)).reshape(ns, D_FF)
        s2 = jnp.concatenate([st, jnp.zeros((nbs, tdec - (CONV_W - 1), D_FF), _F32)],
                             axis=1).reshape(ns, D_FF)
        outs = _post_call(xs, yas, ybs.astype(_BF16), sgas, sgbs, wpa, wpb, wo,
                          g2, wup, cw, cb, wdn, gf, (s1, s2, tpos), layer=l, tm=CHUNK,
                          tiles_per_seq=1, per_row_state=True, final_norm=last)
        xs, a_s = outs[0], outs[1]
        if last:
            ys = outs[2]

        ks_l.append(ks4)
        vs_l.append(vs4)
        as_l.append(vas.reshape(nbs, tdec, A_WIDTH))
        cp_l.append(cp)
        a_ext = jnp.concatenate([st, a_s.reshape(nbs, tdec, D_FF)], axis=1)
        cs_l.append(a_ext[:, -(CONV_W - 1):])

    def kv_out(t):
        return t.reshape(depth, nbp_, B_HEADS, B_HEAD_DIM, seq).transpose(0, 1, 4, 2, 3)

    return (yp.reshape(nbp_, seq, D_MODEL), ys.reshape(nbs, tdec, D_MODEL),
            kv_out(kv_nat[0]), kv_out(kv_nat[1]),
            jnp.stack(ks_l, 0), jnp.stack(vs_l, 0),
            jnp.stack(as_l, 0), jnp.stack(cp_l, 0), jnp.stack(cs_l, 0))
```

```python
import functools
import math

import jax
import jax.numpy as jnp
from jax import lax
from jax.experimental import pallas as pl
from jax.experimental.pallas import tpu as pltpu

D_MODEL = 1024
CHUNK = 128
A_WIDTH = D_MODEL // 2
A_GROUPS = 4
A_GROUP_DIM = A_WIDTH // A_GROUPS
B_HEADS = 8
B_HEAD_DIM = 64
B_WIDTH = B_HEADS * B_HEAD_DIM
MOBA_BLOCK = 256
MOBA_TOPK = 3
N_BUCKETS = 32
MAX_DISTANCE = 128
D_FF = 2816
CONV_W = 3
EPS = 1e-6
N_IN = 2 * A_WIDTH + 3 * B_WIDTH + 2 * D_MODEL

_C_U, _C_VA, _C_Q, _C_K, _C_V, _C_GA, _C_GB = 0, 512, 1024, 1536, 2048, 2560, 3584

_NEG = -1e30
_HEADS_PER_PAIR = 2
_PAIRS_PER_STEP = 2
_BLOCKS_PER_ITER = 4
_TAIL_BLOCKS = 2
_TAB_OWN, _TAB_PREV, _TAB_FAR, _TAB_DEAD = 0, 1, 2, 3
_HEAD_TABLES = 2
_PAIR_LANES = _HEADS_PER_PAIR * B_HEAD_DIM
_K_LANES = 2 * _PAIR_LANES
_V_ROWS = B_HEAD_DIM + 16
_LOG2E = math.log2(math.e)
_PAGES_PER_STEP = 16
_PAGES_PER_BLOCK = MOBA_BLOCK // CHUNK
_PAGE_SLOTS = 3
_PROJ_TILE = 512
_POST_TILE = 512
_FFN_SPLITS = (0, 1536, D_FF)
_VMEM_LIMIT = 48 * 1024 * 1024
_SCALE = 1.0 / math.sqrt(B_HEAD_DIM)

_F32 = jnp.float32
_BF16 = jnp.bfloat16


def _gelu(x):
    k = -2.0 * 0.7978845608028654 * _LOG2E
    return x / (1.0 + jnp.exp2(x * (k + (k * 0.044715) * (x * x))))


def _sigmoid(x):
    return 1.0 / (1.0 + jnp.exp2(x * -_LOG2E))


def _rms(x, g):
    return x * lax.rsqrt(jnp.mean(x * x, axis=-1, keepdims=True) + EPS) * g


def _const_spec(shape):
    zeros = (0,) * len(shape)
    return pl.BlockSpec(shape, lambda *_: zeros, pipeline_mode=pl.Buffered(1))


def _layer_spec(shape, layer):
    index = (layer,) + (0,) * len(shape)
    return pl.BlockSpec((None,) + shape, lambda *_: index, pipeline_mode=pl.Buffered(1))


def _proj_kernel(x_ref, g1_ref, w_ref, lng_ref, lnb_ref, wmix_ref, bmix_ref, *rest, tm, prompt,
                 nb=None):
    if prompt:
        (q_ref, kb_ref, vt_ref, kmean_ref, knat_ref, vnat_ref, ya_ref, sga_ref, sgb_ref) = rest[-9:]
    else:
        (q_ref, k_ref, v_ref, ya_ref, sga_ref, sgb_ref, va_ref) = rest
    xn = _rms(x_ref[...], g1_ref[...]).astype(_BF16)

    def proj(lo, width):
        return jnp.dot(xn, w_ref[:, lo:lo + width], preferred_element_type=_F32)

    u = _gelu(proj(_C_U, A_WIDTH))
    va = _gelu(proj(_C_VA, A_WIDTH))
    mu = jnp.mean(va, axis=-1, keepdims=True)
    vc = va - mu
    var = jnp.mean(vc * vc, axis=-1, keepdims=True)
    va = vc * lax.rsqrt(var + EPS) * lng_ref[...] + lnb_ref[...]
    if not prompt:
        va_ref[...] = va
    for c in range(tm // CHUNK):
        rows = slice(c * CHUNK, (c + 1) * CHUNK)
        for g in range(A_GROUPS):
            cols = slice(g * A_GROUP_DIM, (g + 1) * A_GROUP_DIM)
            mixed = jnp.dot(wmix_ref[g], va[rows, cols].astype(_BF16),
                            preferred_element_type=_F32) + bmix_ref[g]
            ya_ref[rows, cols] = (u[rows, cols] * mixed).astype(_BF16)

    q_ref[...] = proj(_C_Q, B_WIDTH)
    k = proj(_C_K, B_WIDTH)
    v = proj(_C_V, B_WIDTH)
    if prompt:
        blocks = tm // MOBA_BLOCK
        vt = v.T
        knat_ref[0] = k.T
        vnat_ref[0] = vt
        for r in range(blocks):
            kmean_ref[0, r:r + 1, :] = jnp.mean(k[r * MOBA_BLOCK:(r + 1) * MOBA_BLOCK], axis=0,
                                                keepdims=True)
        kb = k.astype(_BF16)
        first_blk = (pl.program_id(0) * blocks) % nb
        shape = (tm, _K_LANES - _PAIR_LANES)
        blk_of_row = first_blk + lax.broadcasted_iota(jnp.int32, shape, 0) // MOBA_BLOCK
        onehot = jnp.where(lax.broadcasted_iota(jnp.int32, shape, 1) == blk_of_row, 1.0, 0.0
                           ).astype(_BF16)
        for hp in range(B_HEADS // _HEADS_PER_PAIR):
            kb_ref[:, hp * _K_LANES:hp * _K_LANES + _PAIR_LANES] = (
                kb[:, hp * _PAIR_LANES:(hp + 1) * _PAIR_LANES])
            kb_ref[:, hp * _K_LANES + _PAIR_LANES:(hp + 1) * _K_LANES] = onehot
        vtb = vt.astype(_BF16)
        row = lax.broadcasted_iota(jnp.int32, (_V_ROWS - B_HEAD_DIM, MOBA_BLOCK), 0)
        ones_row = jnp.where(row == 0, 1.0, 0.0).astype(_BF16)
        for r in range(blocks):
            keys = slice(r * MOBA_BLOCK, (r + 1) * MOBA_BLOCK)
            for h in range(B_HEADS):
                vt_ref[0, r, h * _V_ROWS:h * _V_ROWS + B_HEAD_DIM, :] = (
                    vtb[h * B_HEAD_DIM:(h + 1) * B_HEAD_DIM, keys])
                vt_ref[0, r, h * _V_ROWS + B_HEAD_DIM:(h + 1) * _V_ROWS, :] = ones_row
    else:
        k_ref[...] = k
        v_ref[...] = v
    sga_ref[...] = proj(_C_GA, D_MODEL).astype(_BF16)
    sgb_ref[...] = proj(_C_GB, D_MODEL).astype(_BF16)


def _proj_call(x2, g1, w_in, lng, lnb, wmix, bmix, *, tm, prompt, layer=0, depth=1, seq=None,
               kv_prev=None):
    n = x2.shape[0]
    nt = n // tm
    row = lambda width: pl.BlockSpec((tm, width), lambda i: (i, 0))
    in_specs = [row(D_MODEL), _const_spec((1, D_MODEL)), _layer_spec((D_MODEL, N_IN), layer),
                _const_spec((1, A_WIDTH)), _const_spec((1, A_WIDTH)),
                _layer_spec((A_GROUPS, CHUNK, CHUNK), layer),
                _layer_spec((A_GROUPS, CHUNK, CHUNK), layer)]
    args = [x2, g1, w_in, lng, lnb, wmix, bmix]
    aliases = {}
    tail_shape = [jax.ShapeDtypeStruct((n, A_WIDTH), _BF16),
                  jax.ShapeDtypeStruct((n, D_MODEL), _BF16),
                  jax.ShapeDtypeStruct((n, D_MODEL), _BF16)]
    tail_specs = [row(A_WIDTH), row(D_MODEL), row(D_MODEL)]
    nb = None
    if prompt:
        nb = seq // MOBA_BLOCK
        blocks = tm // MOBA_BLOCK
        tps = seq // tm
        assert tm % MOBA_BLOCK == 0 and seq % tm == 0 and nb <= _K_LANES - _PAIR_LANES
        nseq = n // seq
        k_cols = B_HEADS // _HEADS_PER_PAIR * _K_LANES
        nat = jax.ShapeDtypeStruct((depth * nseq, B_WIDTH, seq), _F32)
        nat_spec = pl.BlockSpec((1, B_WIDTH, tm), lambda i: (layer * nseq + i // tps, 0, i % tps))
        out_shape = [jax.ShapeDtypeStruct((n, B_WIDTH), _F32),
                     jax.ShapeDtypeStruct((n, k_cols), _BF16),
                     jax.ShapeDtypeStruct((nseq, nb, B_HEADS * _V_ROWS, MOBA_BLOCK), _BF16),
                     jax.ShapeDtypeStruct((nt, blocks, B_WIDTH), _F32), nat, nat] + tail_shape
        out_specs = [row(B_WIDTH), row(k_cols),
                     pl.BlockSpec((1, blocks, B_HEADS * _V_ROWS, MOBA_BLOCK),
                                  lambda i: (i // tps, i % tps, 0, 0)),
                     pl.BlockSpec((1, blocks, B_WIDTH), lambda i: (i, 0, 0)), nat_spec, nat_spec
                     ] + tail_specs
        if kv_prev is not None:
            in_specs += [pl.BlockSpec(memory_space=pl.ANY)] * 2
            args += list(kv_prev)
            aliases = {7: 4, 8: 5}
    else:
        out_shape = [jax.ShapeDtypeStruct((n, B_WIDTH), _F32)] * 3 + tail_shape + [
            jax.ShapeDtypeStruct((n, A_WIDTH), _F32)]
        out_specs = [row(B_WIDTH)] * 3 + tail_specs + [row(A_WIDTH)]
    return pl.pallas_call(
        functools.partial(_proj_kernel, tm=tm, prompt=prompt, nb=nb),
        grid=(nt,),
        in_specs=in_specs,
        out_specs=out_specs,
        out_shape=out_shape,
        input_output_aliases=aliases,
        compiler_params=pltpu.CompilerParams(dimension_semantics=("arbitrary",),
                                             vmem_limit_bytes=_VMEM_LIMIT),
        name="proj_mixer_a",
    )(*args)


def _topk_block_mask(gate, own, nb, width):
    blk = lax.broadcasted_iota(jnp.int32, (nb, width), 0)
    gm = jnp.where(blk < own, gate, -jnp.inf)
    rank = jnp.zeros((nb, width), _F32)
    for jp in range(nb):
        r = gm[jp:jp + 1, :]
        beats = jnp.where(r > gm, 1.0, jnp.where(r == gm, jnp.where(blk > jp, 1.0, 0.0), 0.0))
        rank = rank + beats
    past_sel = jnp.where(blk < own, jnp.where(rank < MOBA_TOPK, 1.0, 0.0), 0.0)
    return jnp.where(blk == own, 1.0, past_sel)


def _moba_prompt_kernel(q_ref, k_ref, vt_ref, kmean_ref, btab_ref, o_ref, s_scr, *, nb):
    i = pl.program_id(2)
    blk = MOBA_BLOCK
    heads = range(_PAIRS_PER_STEP * _HEADS_PER_PAIR)

    qt = q_ref[0].T
    feat = lax.broadcasted_iota(jnp.int32, (_PAIR_LANES, blk), 0)
    qz = []
    for hh in heads:
        pair, sub = divmod(hh, _HEADS_PER_PAIR)
        in_head = (feat >= sub * B_HEAD_DIM) & (feat < (sub + 1) * B_HEAD_DIM)
        qh = jnp.where(in_head, qt[pair * _PAIR_LANES:(pair + 1) * _PAIR_LANES], 0.0)
        gate = jnp.dot(kmean_ref[0, :, pair * _PAIR_LANES:(pair + 1) * _PAIR_LANES], qh,
                       preferred_element_type=_F32, precision=lax.Precision.HIGHEST)
        mask_rows = jnp.where(_topk_block_mask(gate, i, nb, blk) > 0.5, 0.0, _NEG)
        pad = jnp.zeros((_K_LANES - _PAIR_LANES - nb, blk), _F32)
        qz.append(jnp.concatenate([qh * (_SCALE * _LOG2E), mask_rows, pad], axis=0).astype(_BF16))

    n_wide = lax.div(i + 1, _BLOCKS_PER_ITER)
    tail_first = n_wide * _BLOCKS_PER_ITER
    n_tail = lax.div(i + _TAIL_BLOCKS - tail_first, _TAIL_BLOCKS)
    first_head = pl.program_id(1) * len(heads)

    def slots(first, it, width):
        js = [first + it * width + u for u in range(width)]
        kinds = [jnp.where(j == i, _TAB_OWN, jnp.where(j == i - 1, _TAB_PREV,
                                                       jnp.where(j > i, _TAB_DEAD, _TAB_FAR)))
                 for j in js]
        return js, [jnp.minimum(j, i) for j in js], kinds

    def table_row(hh, kind):
        per_head = (first_head + hh) * _HEAD_TABLES + kind
        return jnp.where(kind < _HEAD_TABLES, per_head, B_HEADS * _HEAD_TABLES + kind - _HEAD_TABLES)

    def logits_pass(it, ms, group, first, width):
        js, jcs, tabs = slots(first, it, width)
        out = []
        for hh, m in zip(group, ms):
            pair = hh // _HEADS_PER_PAIR
            for j, jc, tab in zip(js, jcs, tabs):
                kj = k_ref[0, pl.ds(pl.multiple_of(jc * blk, blk), blk),
                           pair * _K_LANES:(pair + 1) * _K_LANES]
                s = jnp.dot(kj, qz[hh], preferred_element_type=_F32) + btab_ref[table_row(hh, tab)]
                s_scr[hh, j] = s
                m = jnp.maximum(m, jnp.max(s, axis=0, keepdims=True))
            out.append(m)
        return tuple(out)

    def value_pass(it, accs, group, ms, first, width):
        js, jcs, _ = slots(first, it, width)
        out = []
        for hh, acc, m in zip(group, accs, ms):
            for j, jc in zip(js, jcs):
                p = jnp.exp2(s_scr[hh, j] - m).astype(_BF16)
                acc = acc + jnp.dot(vt_ref[0, jc, hh * _V_ROWS:(hh + 1) * _V_ROWS, :], p,
                                    preferred_element_type=_F32)
            out.append(acc)
        return tuple(out)

    group = tuple(heads)
    sweeps = ((n_wide, 0, _BLOCKS_PER_ITER), (n_tail, tail_first, _TAIL_BLOCKS))
    ms = tuple(jnp.full((1, blk), _NEG, _F32) for _ in group)
    for n, first, width in sweeps:
        ms = lax.fori_loop(0, n, lambda it, c, f=first, w=width: logits_pass(it, c, group, f, w), ms)
    accs = tuple(jnp.zeros((_V_ROWS, blk), _F32) for _ in group)
    for n, first, width in sweeps:
        accs = lax.fori_loop(
            0, n, lambda it, c, f=first, w=width: value_pass(it, c, group, ms, f, w), accs)
    outs = [acc[:B_HEAD_DIM] / acc[B_HEAD_DIM:B_HEAD_DIM + 1] for acc in accs]
    o_ref[0] = jnp.concatenate(outs, axis=0).T.astype(_BF16)


def _moba_prompt_call(q3, kb3, vt4, kmean3, btab):
    b, t, _ = q3.shape
    nb = t // MOBA_BLOCK
    g = _PAIRS_PER_STEP
    heads = g * _HEADS_PER_PAIR
    return pl.pallas_call(
        functools.partial(_moba_prompt_kernel, nb=nb),
        grid=(b, B_HEADS // heads, nb),
        in_specs=[pl.BlockSpec((1, MOBA_BLOCK, g * _PAIR_LANES), lambda bb, hp, i: (bb, i, hp)),
                  pl.BlockSpec((1, t, g * _K_LANES), lambda bb, hp, i: (bb, 0, hp)),
                  pl.BlockSpec((1, nb, heads * _V_ROWS, MOBA_BLOCK),
                               lambda bb, hp, i: (bb, 0, hp, 0)),
                  pl.BlockSpec((1, nb, g * _PAIR_LANES), lambda bb, hp, i: (bb, 0, hp)),
                  _const_spec(btab.shape)],
        out_specs=pl.BlockSpec((1, MOBA_BLOCK, g * _PAIR_LANES), lambda bb, hp, i: (bb, i, hp)),
        out_shape=jax.ShapeDtypeStruct((b, t, B_WIDTH), _BF16),
        scratch_shapes=[pltpu.VMEM((heads, nb + _TAIL_BLOCKS - 1, MOBA_BLOCK, MOBA_BLOCK), _F32)],
        compiler_params=pltpu.CompilerParams(
            dimension_semantics=("arbitrary", "arbitrary", "arbitrary"),
            vmem_limit_bytes=_VMEM_LIMIT),
        name="moba_prompt",
    )(q3, kb3, vt4, kmean3, btab)


def _moba_sample_kernel(pt_ref, qbd_ref, knew_ref, vnew_ref, bnear_ref, bfar_ref,
                        bownt_ref, hmaskt_ref, rsum_ref, ck_hbm, cv_hbm, o_ref,
                        gate_scr, m_scr, l_scr, acc_scr, st_scr, qbdt_scr, kbuf, vbuf, sems,
                        *, nb_past, page_base):
    pages = _PAGES_PER_STEP
    c = pl.program_id(1)
    step = pl.program_id(0) * pl.num_programs(1) + c
    last_step = pl.num_programs(0) * pl.num_programs(1) - 1
    rows = qbd_ref.shape[1]
    tdec = o_ref.shape[1]
    blocks_per_step = pages // _PAGES_PER_BLOCK

    ahead = _PAGE_SLOTS - 1

    def slot_of(s):
        return lax.rem(s, _PAGE_SLOTS)

    def page_copy(kind, of_step, p):
        src_step = jnp.minimum(of_step, last_step)
        src, buf = ((ck_hbm, kbuf), (cv_hbm, vbuf))[kind]
        page = page_base + pt_ref[src_step * pages + p]
        slot = slot_of(of_step)
        return pltpu.make_async_copy(src.at[page], buf.at[slot, p], sems.at[slot, kind])

    def block_pages(bl):
        return range(bl * _PAGES_PER_BLOCK, (bl + 1) * _PAGES_PER_BLOCK)

    @pl.when(step == 0)
    def _():
        for s in range(ahead):
            for kind in range(2):
                for pg in range(pages):
                    page_copy(kind, s, pg).start()

    for kind in range(2):
        for pg in range(pages):
            page_copy(kind, step, pg).wait()
    slot = slot_of(step)

    def lanes_t(x):
        return jnp.concatenate([x, jnp.zeros((128 - x.shape[0], x.shape[1]), x.dtype)], axis=0).T

    qbd = qbd_ref[0]
    qz = (qbd * _SCALE).astype(_BF16)

    @pl.when(c == 0)
    def _():
        qbdt_scr[...] = lanes_t(qbd)

    qbdt = qbdt_scr[...]
    row_pad = jnp.zeros((128 - rows, MOBA_BLOCK), _F32)

    for bl in range(blocks_per_step):
        jg = c * blocks_per_step + bl
        kts = [kbuf[slot, pg] for pg in block_pages(bl)]
        kt = jnp.concatenate(kts, axis=1)
        kmean = jnp.sum(functools.reduce(lambda a, b: a + b, kts), axis=-1,
                        keepdims=True) * (1.0 / MOBA_BLOCK)
        gate_scr[pl.ds(jg, 1), :] = jnp.sum(qbdt * kmean, axis=0, keepdims=True)
        s = jnp.dot(qz, kt.astype(_BF16), preferred_element_type=_F32)
        s = s + jnp.where(jg == nb_past - 1, bnear_ref[...], bfar_ref[:, 0:1])
        st_scr[bl] = jnp.concatenate([s, row_pad], axis=0).T
        for pg in block_pages(bl):
            page_copy(0, step + ahead, pg).start()
    for bl in range(blocks_per_step):
        jg = c * blocks_per_step + bl
        vt = jnp.concatenate([vbuf[slot, pg] for pg in block_pages(bl)], axis=1)
        st = st_scr[bl]
        m = jnp.max(st, axis=0, keepdims=True)
        p = jnp.exp(st - m)
        m_scr[pl.ds(jg, 1), :] = m
        l_scr[pl.ds(jg, 1), :] = jnp.sum(p, axis=0, keepdims=True)
        acc_scr[jg] = jnp.dot(vt.astype(_BF16), p.astype(_BF16), preferred_element_type=_F32)
        for pg in block_pages(bl):
            page_copy(1, step + ahead, pg).start()

    @pl.when(step == last_step)
    def _():
        for s in range(1, ahead + 1):
            for kind in range(2):
                for pg in range(pages):
                    page_copy(kind, last_step + s, pg).wait()

    @pl.when(c == pl.num_programs(1) - 1)
    def _():
        sel = _topk_block_mask(gate_scr[...], nb_past, nb_past, 128) > 0.5
        m_blk = m_scr[...]
        s_own = jnp.dot(knew_ref[0], qbdt * _SCALE, preferred_element_type=_F32) + bownt_ref[...]
        m_tot = jnp.maximum(jnp.max(s_own, axis=0, keepdims=True),
                            jnp.max(jnp.where(sel, m_blk, _NEG), axis=0, keepdims=True))
        w = jnp.where(sel, jnp.exp(m_blk - m_tot), 0.0)
        p_own = jnp.exp(s_own - m_tot)
        l_tot = jnp.sum(p_own, axis=0, keepdims=True) + jnp.sum(w * l_scr[...], axis=0, keepdims=True)
        p_own = jnp.concatenate([p_own, jnp.zeros((128 - p_own.shape[0], 128), _F32)], axis=0)
        accs = [jnp.dot(lanes_t(vnew_ref[0]), p_own, preferred_element_type=_F32)]
        accs += [jnp.zeros_like(accs[0]) for _ in range(3)]
        for j in range(nb_past):
            accs[j % 4] = accs[j % 4] + w[j:j + 1, :] * acc_scr[j]
        out_t = ((accs[0] + accs[1]) + (accs[2] + accs[3])) / l_tot
        folded = lax.dot_general(rsum_ref[...], out_t * hmaskt_ref[...], (((1,), (1,)), ((), ())),
                                 preferred_element_type=_F32, precision=lax.Precision.HIGHEST)
        o_ref[0] = folded[:tdec]


def _moba_sample_call(page_flat, qbd, knew, vnew, bnear, bfar, bownt, hmaskt, rsum,
                      cache_kt, cache_vt, *, layer, n_pool, n_pages, tdec):
    nbatch, rows, _ = qbd.shape
    pages = _PAGES_PER_STEP
    nb_past = n_pages // _PAGES_PER_BLOCK
    assert nb_past % 8 == 0 and rows <= 128
    assert nbatch * (n_pages // pages) >= _PAGE_SLOTS

    per_batch = lambda arr: pl.BlockSpec((1,) + arr.shape[1:], lambda bb, cc, pt: (bb, 0, 0))
    const = lambda arr: pl.BlockSpec(arr.shape, lambda bb, cc, pt: (0,) * arr.ndim)
    page_ring = pltpu.VMEM((_PAGE_SLOTS, pages, B_WIDTH, CHUNK), _F32)
    grid_spec = pltpu.PrefetchScalarGridSpec(
        num_scalar_prefetch=1,
        grid=(nbatch, n_pages // pages),
        in_specs=[per_batch(qbd), per_batch(knew), per_batch(vnew),
                  const(bnear), const(bfar), const(bownt), const(hmaskt), const(rsum),
                  pl.BlockSpec(memory_space=pl.ANY), pl.BlockSpec(memory_space=pl.ANY)],
        out_specs=pl.BlockSpec((1, tdec, B_WIDTH), lambda bb, cc, pt: (bb, 0, 0)),
        scratch_shapes=[pltpu.VMEM((nb_past, 128), _F32),
                        pltpu.VMEM((nb_past, 128), _F32),
                        pltpu.VMEM((nb_past, 128), _F32),
                        pltpu.VMEM((nb_past, B_WIDTH, 128), _F32),
                        pltpu.VMEM((pages // _PAGES_PER_BLOCK, MOBA_BLOCK, 128), _F32),
                        pltpu.VMEM((B_WIDTH, 128), _F32),
                        page_ring, page_ring,
                        pltpu.SemaphoreType.DMA((_PAGE_SLOTS, 2))])
    return pl.pallas_call(
        functools.partial(_moba_sample_kernel, nb_past=nb_past, page_base=layer * n_pool),
        grid_spec=grid_spec,
        out_shape=jax.ShapeDtypeStruct((nbatch, tdec, B_WIDTH), _F32),
        compiler_params=pltpu.CompilerParams(dimension_semantics=("arbitrary", "arbitrary"),
                                             vmem_limit_bytes=_VMEM_LIMIT),
        name="moba_sample",
    )(page_flat, qbd, knew, vnew, bnear, bfar, bownt, hmaskt, rsum, cache_kt, cache_vt)


def _post_kernel(x_ref, ya_ref, yb_ref, sga_ref, sgb_ref, wpa_ref, wpb_ref, wo_ref, g2_ref,
                 wup_ref, cw_ref, cb_ref, wdn_ref, gf_ref, *rest, tm, tiles_per_seq, per_row_state,
                 final_norm):
    if per_row_state:
        s1_ref, s2_ref, tpos_ref = rest[:3]
        rest = rest[3:]
    else:
        st_ref = rest[0]
        rest = rest[1:]
    xo_ref, a_ref = rest[:2]
    rest = rest[2:]
    if final_norm:
        y_ref = rest[0]
        rest = rest[1:]
    if not per_row_state:
        carry_scr = rest[0]

    ma = jnp.dot(ya_ref[...], wpa_ref[...], preferred_element_type=_F32)
    mb = jnp.dot(yb_ref[...], wpb_ref[...], preferred_element_type=_F32)
    merged = (_sigmoid(sga_ref[...].astype(_F32)) * ma
              + _sigmoid(sgb_ref[...].astype(_F32)) * mb)
    x1 = x_ref[...] + jnp.dot(merged.astype(_BF16), wo_ref[...], preferred_element_type=_F32)

    xn = _rms(x1, g2_ref[...]).astype(_BF16)
    if not per_row_state:
        @pl.when(pl.program_id(0) % tiles_per_seq == 0)
        def _():
            carry_scr[...] = st_ref[0]

    x2 = x1
    for lo, hi in zip(_FFN_SPLITS[:-1], _FFN_SPLITS[1:]):
        cols = slice(lo, hi)
        width = hi - lo
        a = jnp.dot(xn, wup_ref[:, cols], preferred_element_type=_F32)
        gate = jnp.dot(xn, wup_ref[:, D_FF + lo:D_FF + hi], preferred_element_type=_F32)
        prev1 = pltpu.roll(a, 1, 0)
        prev2 = pltpu.roll(a, 2, 0)
        if per_row_state:
            tpos = tpos_ref[:, cols]
            prev1 = jnp.where(tpos >= 1, prev1, s1_ref[:, cols])
            prev2 = jnp.where(tpos >= 2, prev2, s2_ref[:, cols])
            a_ref[:, cols] = a
        else:
            row = lax.broadcasted_iota(jnp.int32, (8, width), 0)
            c0 = carry_scr[0:1, cols]
            c1 = carry_scr[1:2, cols]
            top1 = jnp.where(row == 0, c1, prev1[:8])
            top2 = jnp.where(row == 0, c0, jnp.where(row == 1, c1, prev2[:8]))
            prev1 = jnp.concatenate([top1, prev1[8:]], axis=0)
            prev2 = jnp.concatenate([top2, prev2[8:]], axis=0)
            tail = a[tm - (CONV_W - 1):, :]
            carry_scr[:, cols] = tail
            a_ref[0, :, cols] = tail
        conv = (cb_ref[:, cols] + prev2 * cw_ref[0:1, cols] + prev1 * cw_ref[1:2, cols]
                + a * cw_ref[2:3, cols])
        act = (_gelu(conv) * gate).astype(_BF16)
        x2 = x2 + jnp.dot(act, wdn_ref[cols, :], preferred_element_type=_F32)
    xo_ref[...] = x2
    if final_norm:
        y_ref[...] = _rms(x2, gf_ref[...])


def _post_call(x2, ya, yb, sga, sgb, wpa, wpb, wo, g2, wup, cw, cb, wdn, gf, state_args, *,
               layer, tm, tiles_per_seq, per_row_state, final_norm):
    n = x2.shape[0]
    row = lambda width: pl.BlockSpec((tm, width), lambda i: (i, 0))
    in_specs = [row(D_MODEL), row(A_WIDTH), row(B_WIDTH), row(D_MODEL), row(D_MODEL),
                _layer_spec((A_WIDTH, D_MODEL), layer), _layer_spec((B_WIDTH, D_MODEL), layer),
                _layer_spec((D_MODEL, D_MODEL), layer), _const_spec((1, D_MODEL)),
                _layer_spec((D_MODEL, 2 * D_FF), layer), _const_spec((CONV_W, D_FF)),
                _const_spec((1, D_FF)), _layer_spec((D_FF, D_MODEL), layer),
                _const_spec((1, D_MODEL))]
    out_shape = [jax.ShapeDtypeStruct((n, D_MODEL), _F32)]
    out_specs = [row(D_MODEL)]
    scratch = []
    if per_row_state:
        in_specs += [row(D_FF), row(D_FF), row(D_FF)]
        out_shape.append(jax.ShapeDtypeStruct((n, D_FF), _F32))
        out_specs.append(row(D_FF))
    else:
        nseq = n // (tm * tiles_per_seq)
        in_specs.append(pl.BlockSpec((1, CONV_W - 1, D_FF), lambda i: (i // tiles_per_seq, 0, 0)))
        out_shape.append(jax.ShapeDtypeStruct((nseq, CONV_W - 1, D_FF), _F32))
        out_specs.append(pl.BlockSpec((1, CONV_W - 1, D_FF), lambda i: (i // tiles_per_seq, 0, 0)))
        scratch.append(pltpu.VMEM((CONV_W - 1, D_FF), _F32))
    if final_norm:
        out_shape.append(jax.ShapeDtypeStruct((n, D_MODEL), _F32))
        out_specs.append(row(D_MODEL))
    return pl.pallas_call(
        functools.partial(_post_kernel, tm=tm, tiles_per_seq=tiles_per_seq,
                          per_row_state=per_row_state, final_norm=final_norm),
        grid=(n // tm,),
        in_specs=in_specs,
        out_specs=out_specs,
        out_shape=out_shape,
        scratch_shapes=scratch,
        compiler_params=pltpu.CompilerParams(dimension_semantics=("arbitrary",),
                                             vmem_limit_bytes=_VMEM_LIMIT),
        name="merge_convffn",
    )(x2, ya, yb, sga, sgb, wpa, wpb, wo, g2, wup, cw, cb, wdn, gf, *state_args)


def _t5_bucket(rel):
    max_exact = N_BUCKETS // 2
    relf = jnp.maximum(rel, 1).astype(_F32)
    large = max_exact + (jnp.log(relf / max_exact) / math.log(MAX_DISTANCE / max_exact)
                         * (N_BUCKETS - max_exact)).astype(jnp.int32)
    large = jnp.minimum(large, N_BUCKETS - 1)
    return jnp.where(rel < max_exact, rel, large)


def _far_bucket_is_last(min_rel):
    max_exact = N_BUCKETS // 2
    v = max_exact + int(math.log(min_rel / max_exact) / math.log(MAX_DISTANCE / max_exact)
                        * (N_BUCKETS - max_exact) - 1e-6)
    return v >= N_BUCKETS - 1


def _toeplitz(diag, n):
    width = 2 * n
    flat = jnp.tile(diag, (1,) * (diag.ndim - 1) + (n,))
    skew = flat[..., :n * (width - 1)].reshape(diag.shape[:-1] + (n, width - 1))
    return skew[..., n - 1:2 * n - 1]


def _prompt_bias_tables(rel_bias):
    assert _far_bucket_is_last(MOBA_BLOCK + 1)
    assert (_TAB_OWN, _TAB_PREV, _TAB_FAR, _TAB_DEAD) == (0, 1, 2, 3) and _HEAD_TABLES == 2
    bt = rel_bias.T.astype(_F32)
    bt = (bt - bt[:, N_BUCKETS - 1:]) * _LOG2E
    rel = jnp.arange(2 * MOBA_BLOCK, dtype=jnp.int32) - (MOBA_BLOCK - 1)
    own = jnp.where(rel >= 0, bt[:, _t5_bucket(jnp.maximum(rel, 0))], _NEG)
    prev = bt[:, _t5_bucket(rel + MOBA_BLOCK)]
    per_head = jnp.stack([_toeplitz(own, MOBA_BLOCK), _toeplitz(prev, MOBA_BLOCK)], axis=1)
    per_head = per_head.reshape(B_HEADS * _HEAD_TABLES, MOBA_BLOCK, MOBA_BLOCK)
    shared = jnp.stack([jnp.zeros((MOBA_BLOCK, MOBA_BLOCK), _F32),
                        jnp.full((MOBA_BLOCK, MOBA_BLOCK), _NEG, _F32)])
    return jnp.concatenate([per_head, shared], axis=0)


def _sample_bias_tables(rel_bias, tdec, tpad):
    assert _far_bucket_is_last(MOBA_BLOCK + 1)
    bt = rel_bias.T.astype(_F32)
    tok = jnp.arange(tdec, dtype=jnp.int32)[:, None]
    key = jnp.arange(MOBA_BLOCK, dtype=jnp.int32)[None, :]
    near = bt[:, _t5_bucket(MOBA_BLOCK + tok - key)]
    far = jnp.broadcast_to(bt[:, N_BUCKETS - 1][:, None, None], (B_HEADS, tdec, 128))
    new = jnp.arange(tpad, dtype=jnp.int32)[None, :]
    own = jnp.where((tok - new >= 0) & (new < tdec), bt[:, _t5_bucket(jnp.maximum(tok - new, 0))],
                    _NEG)
    flat = lambda a: a.reshape(B_HEADS * tdec, a.shape[-1])
    return flat(near), flat(far), flat(own)


def kernel(x_prompt, x_sample, cache_k, cache_v, state_conv, page_table, w_in, ln_v_g, ln_v_b, w_s,
           b_s, w_pa, w_pb, w_o, norm1_g, norm2_g, w_up, conv_w, conv_b, w_down, rel_bias, norm_f):
    nbp_, seq, _ = x_prompt.shape
    nbs, tdec, _ = x_sample.shape
    depth, n_pool, page_size = cache_k.shape[:3]
    n_pages = page_table.shape[1]
    ns = nbs * tdec
    np_ = nbp_ * seq
    assert page_size == CHUNK and n_pages % _PAGES_PER_STEP == 0 and seq % MOBA_BLOCK == 0
    assert _PAGES_PER_STEP % _PAGES_PER_BLOCK == 0 and seq % _POST_TILE == 0
    assert CONV_W - 1 <= tdec <= CHUNK and ns % CHUNK == 0 and CHUNK % tdec == 0
    tpad = -(-tdec // 8) * 8
    tm_p = _PROJ_TILE
    nb = seq // MOBA_BLOCK
    nb_past = n_pages // _PAGES_PER_BLOCK
    rows = B_HEADS * tdec

    xp = x_prompt.reshape(np_, D_MODEL)
    xs = x_sample.reshape(ns, D_MODEL)
    cache_kt = cache_k.transpose(0, 1, 3, 4, 2).reshape(depth * n_pool, B_WIDTH, page_size)
    cache_vt = cache_v.transpose(0, 1, 3, 4, 2).reshape(depth * n_pool, B_WIDTH, page_size)
    page_flat = page_table.reshape(-1).astype(jnp.int32)

    btab = _prompt_bias_tables(rel_bias)
    bnear_s, bfar_s, bown_s = _sample_bias_tables(rel_bias, tdec, tpad)
    row_head = jnp.arange(rows, dtype=jnp.int32)[:, None] // tdec
    hmask = (jnp.arange(B_WIDTH, dtype=jnp.int32)[None, :] // B_HEAD_DIM == row_head).astype(_F32)
    rsum = (jnp.arange(rows, dtype=jnp.int32)[None, :] % tdec
            == jnp.arange(tdec, dtype=jnp.int32)[:, None]).astype(_F32)
    lane_pad = lambda a: jnp.pad(a, ((0, 0), (0, 128 - rows)))
    bownt_s, hmaskt = lane_pad(bown_s.T), lane_pad(hmask.T)
    rsum_p = lane_pad(jnp.pad(rsum, ((0, tpad - tdec), (0, 0))))
    tpos = jnp.broadcast_to((jnp.arange(ns, dtype=jnp.int32) % tdec)[:, None], (ns, D_FF))
    tril = jnp.tril(jnp.ones((CHUNK, CHUNK), bool))
    idx = jnp.arange(CHUNK, dtype=jnp.int32)
    same_chunk = tril & (idx[:, None] // tdec == idx[None, :] // tdec)
    zero_state = jnp.zeros((nbp_, CONV_W - 1, D_FF), _F32)
    w_in_b, wpa, wpb, wo = (w.astype(_BF16) for w in (w_in, w_pa, w_pb, w_o))
    wup, wdn = w_up.astype(_BF16), w_down.astype(_BF16)
    gf = norm_f[None, :]
    mix_shape = (depth, A_GROUPS, CHUNK, CHUNK)
    reps = CHUNK // tdec
    wmix_p = jnp.where(tril, w_s, 0.0).astype(_BF16)
    bmix_p = jnp.broadcast_to(b_s[..., None], mix_shape)
    wmix_s = jnp.where(same_chunk, jnp.tile(w_s[:, :, :tdec, :tdec], (1, 1, reps, reps)),
                       0.0).astype(_BF16)
    bmix_s = jnp.broadcast_to(jnp.tile(b_s[:, :, :tdec], (1, 1, reps))[..., None], mix_shape)

    ks_l, vs_l, as_l, cp_l, cs_l = [], [], [], [], []
    yp = ys = kv_nat = None
    for l in range(depth):
        last = l == depth - 1
        g1 = norm1_g[l][None, :]
        lng, lnb = ln_v_g[l][None, :], ln_v_b[l][None, :]
        g2 = norm2_g[l][None, :]
        cw, cb = conv_w[l], conv_b[l][None, :]

        (qp, kbp, vtp, kmeanp, knat, vnat, yap, sgap, sgbp) = _proj_call(
            xp, g1, w_in_b, lng, lnb, wmix_p, bmix_p, tm=tm_p, prompt=True, layer=l, depth=depth,
            seq=seq, kv_prev=kv_nat)
        kv_nat = (knat, vnat)
        ybp = _moba_prompt_call(qp.reshape(nbp_, seq, B_WIDTH), kbp.reshape(nbp_, seq, -1),
                                vtp, kmeanp.reshape(nbp_, nb, B_WIDTH), btab)
        outs = _post_call(xp, yap, ybp.reshape(np_, B_WIDTH), sgap, sgbp, wpa, wpb, wo, g2,
                          wup, cw, cb, wdn, gf, (zero_state,), layer=l, tm=_POST_TILE,
                          tiles_per_seq=seq // _POST_TILE, per_row_state=False, final_norm=last)
        xp, cp = outs[0], outs[1]
        if last:
            yp = outs[2]

        qs, ks, vs, yas, sgas, sgbs, vas = _proj_call(xs, g1, w_in_b, lng, lnb, wmix_s, bmix_s,
                                                      tm=CHUNK, prompt=False, layer=l)
        ks4 = ks.reshape(nbs, tdec, B_HEADS, B_HEAD_DIM)
        vs4 = vs.reshape(nbs, tdec, B_HEADS, B_HEAD_DIM)
        qbd = (qs.reshape(nbs, 1, tdec, B_WIDTH) * hmask.reshape(B_HEADS, tdec, B_WIDTH)[None]
               ).reshape(nbs, rows, B_WIDTH)
        pad = ((0, 0), (0, tpad - tdec), (0, 0))
        knew = jnp.pad(ks.reshape(nbs, tdec, B_WIDTH), pad)
        vnew = jnp.pad(vs.reshape(nbs, tdec, B_WIDTH), pad)
        ybs = _moba_sample_call(page_flat, qbd, knew, vnew, bnear_s, bfar_s, bownt_s, hmaskt,
                                rsum_p, cache_kt, cache_vt, layer=l, n_pool=n_pool, n_pages=n_pages,
                                tdec=tdec)
        ybs = ybs.reshape(ns, B_WIDTH)
        st = state_conv[l]
        s1 = jnp.broadcast_to(st[:, 1:2, :], (nbs, tdec, D_FF)).reshape(ns, D_FF)
        s2 = jnp.concatenate([st, jnp.zeros((nbs, tdec - (CONV_W - 1), D_FF), _F32)],
                             axis=1).reshape(ns, D_FF)
        outs = _post_call(xs, yas, ybs.astype(_BF16), sgas, sgbs, wpa, wpb, wo,
                          g2, wup, cw, cb, wdn, gf, (s1, s2, tpos), layer=l, tm=CHUNK,
                          tiles_per_seq=1, per_row_state=True, final_norm=last)
        xs, a_s = outs[0], outs[1]
        if last:
            ys = outs[2]

        ks_l.append(ks4)
        vs_l.append(vs4)
        as_l.append(vas.reshape(nbs, tdec, A_WIDTH))
        cp_l.append(cp)
        a_ext = jnp.concatenate([st, a_s.reshape(nbs, tdec, D_FF)], axis=1)
        cs_l.append(a_ext[:, -(CONV_W - 1):])

    def kv_out(t):
        return t.reshape(depth, nbp_, B_HEADS, B_HEAD_DIM, seq).transpose(0, 1, 4, 2, 3)

    return (yp.reshape(nbp_, seq, D_MODEL), ys.reshape(nbs, tdec, D_MODEL),
            kv_out(kv_nat[0]), kv_out(kv_nat[1]),
            jnp.stack(ks_l, 0), jnp.stack(vs_l, 0),
            jnp.stack(as_l, 0), jnp.stack(cp_l, 0), jnp.stack(cs_l, 0))
```

```python
import functools
import math

import jax
import jax.numpy as jnp
from jax import lax
from jax.experimental import pallas as pl
from jax.experimental.pallas import tpu as pltpu

D_MODEL = 1024
CHUNK = 128
A_WIDTH = D_MODEL // 2
A_GROUPS = 4
A_GROUP_DIM = A_WIDTH // A_GROUPS
B_HEADS = 8
B_HEAD_DIM = 64
B_WIDTH = B_HEADS * B_HEAD_DIM
MOBA_BLOCK = 256
MOBA_TOPK = 3
N_BUCKETS = 32
MAX_DISTANCE = 128
D_FF = 2816
CONV_W = 3
EPS = 1e-6
N_IN = 2 * A_WIDTH + 3 * B_WIDTH + 2 * D_MODEL

_C_U, _C_VA, _C_Q, _C_K, _C_V, _C_GA, _C_GB = 0, 512, 1024, 1536, 2048, 2560, 3584

_NEG = -1e30
_HEADS_PER_PAIR = 2
_PAIRS_PER_STEP = 2
_SWEEP_WIDTHS = (4, 2, 1)
_TAB_OWN, _TAB_PREV, _TAB_FAR, _TAB_DEAD = 0, 1, 2, 3
_HEAD_TABLES = 2
_PAIR_LANES = _HEADS_PER_PAIR * B_HEAD_DIM
_K_LANES = 2 * _PAIR_LANES
_V_ROWS = B_HEAD_DIM + 16
_LOG2E = math.log2(math.e)
_PAGES_PER_STEP = 16
_PAGES_PER_BLOCK = MOBA_BLOCK // CHUNK
_PAGE_SLOTS = 3
_PROJ_TILE = 512
_POST_TILE = 512
_FFN_SPLITS = (0, 1536, D_FF)
_VMEM_LIMIT = 48 * 1024 * 1024
_SCALE = 1.0 / math.sqrt(B_HEAD_DIM)

_F32 = jnp.float32
_BF16 = jnp.bfloat16


def _gelu(x):
    k = -2.0 * 0.7978845608028654 * _LOG2E
    return x / (1.0 + jnp.exp2(x * (k + (k * 0.044715) * (x * x))))


def _sigmoid(x):
    return 1.0 / (1.0 + jnp.exp2(x * -_LOG2E))


def _rms(x, g):
    return x * lax.rsqrt(jnp.mean(x * x, axis=-1, keepdims=True) + EPS) * g


def _const_spec(shape):
    zeros = (0,) * len(shape)
    return pl.BlockSpec(shape, lambda *_: zeros, pipeline_mode=pl.Buffered(1))


def _layer_spec(shape, layer):
    index = (layer,) + (0,) * len(shape)
    return pl.BlockSpec((None,) + shape, lambda *_: index, pipeline_mode=pl.Buffered(1))


def _proj_kernel(x_ref, g1_ref, w_ref, lng_ref, lnb_ref, wmix_ref, bmix_ref, *rest, tm, prompt,
                 nb=None):
    if prompt:
        (q_ref, kb_ref, vt_ref, kmean_ref, knat_ref, vnat_ref, ya_ref, sga_ref, sgb_ref) = rest[-9:]
    else:
        (q_ref, k_ref, v_ref, ya_ref, sga_ref, sgb_ref, va_ref) = rest
    xn = _rms(x_ref[...], g1_ref[...]).astype(_BF16)

    def proj(lo, width):
        return jnp.dot(xn, w_ref[:, lo:lo + width], preferred_element_type=_F32)

    u = _gelu(proj(_C_U, A_WIDTH))
    va = _gelu(proj(_C_VA, A_WIDTH))
    mu = jnp.mean(va, axis=-1, keepdims=True)
    vc = va - mu
    var = jnp.mean(vc * vc, axis=-1, keepdims=True)
    va = vc * lax.rsqrt(var + EPS) * lng_ref[...] + lnb_ref[...]
    if not prompt:
        va_ref[...] = va
    for c in range(tm // CHUNK):
        rows = slice(c * CHUNK, (c + 1) * CHUNK)
        for g in range(A_GROUPS):
            cols = slice(g * A_GROUP_DIM, (g + 1) * A_GROUP_DIM)
            mixed = jnp.dot(wmix_ref[g], va[rows, cols].astype(_BF16),
                            preferred_element_type=_F32) + bmix_ref[g]
            ya_ref[rows, cols] = (u[rows, cols] * mixed).astype(_BF16)

    q_ref[...] = proj(_C_Q, B_WIDTH)
    k = proj(_C_K, B_WIDTH)
    v = proj(_C_V, B_WIDTH)
    if prompt:
        blocks = tm // MOBA_BLOCK
        vt = v.T
        knat_ref[0] = k.T
        vnat_ref[0] = vt
        for r in range(blocks):
            kmean_ref[0, r:r + 1, :] = jnp.mean(k[r * MOBA_BLOCK:(r + 1) * MOBA_BLOCK], axis=0,
                                                keepdims=True)
        kb = k.astype(_BF16)
        first_blk = (pl.program_id(0) * blocks) % nb
        shape = (tm, _K_LANES - _PAIR_LANES)
        blk_of_row = first_blk + lax.broadcasted_iota(jnp.int32, shape, 0) // MOBA_BLOCK
        onehot = jnp.where(lax.broadcasted_iota(jnp.int32, shape, 1) == blk_of_row, 1.0, 0.0
                           ).astype(_BF16)
        for hp in range(B_HEADS // _HEADS_PER_PAIR):
            kb_ref[:, hp * _K_LANES:hp * _K_LANES + _PAIR_LANES] = (
                kb[:, hp * _PAIR_LANES:(hp + 1) * _PAIR_LANES])
            kb_ref[:, hp * _K_LANES + _PAIR_LANES:(hp + 1) * _K_LANES] = onehot
        vtb = vt.astype(_BF16)
        row = lax.broadcasted_iota(jnp.int32, (_V_ROWS - B_HEAD_DIM, MOBA_BLOCK), 0)
        ones_row = jnp.where(row == 0, 1.0, 0.0).astype(_BF16)
        for r in range(blocks):
            keys = slice(r * MOBA_BLOCK, (r + 1) * MOBA_BLOCK)
            for h in range(B_HEADS):
                vt_ref[0, r, h * _V_ROWS:h * _V_ROWS + B_HEAD_DIM, :] = (
                    vtb[h * B_HEAD_DIM:(h + 1) * B_HEAD_DIM, keys])
                vt_ref[0, r, h * _V_ROWS + B_HEAD_DIM:(h + 1) * _V_ROWS, :] = ones_row
    else:
        k_ref[...] = k
        v_ref[...] = v
    sga_ref[...] = proj(_C_GA, D_MODEL).astype(_BF16)
    sgb_ref[...] = proj(_C_GB, D_MODEL).astype(_BF16)


def _proj_call(x2, g1, w_in, lng, lnb, wmix, bmix, *, tm, prompt, layer=0, depth=1, seq=None,
               kv_prev=None):
    n = x2.shape[0]
    nt = n // tm
    row = lambda width: pl.BlockSpec((tm, width), lambda i: (i, 0))
    in_specs = [row(D_MODEL), _const_spec((1, D_MODEL)), _layer_spec((D_MODEL, N_IN), layer),
                _const_spec((1, A_WIDTH)), _const_spec((1, A_WIDTH)),
                _layer_spec((A_GROUPS, CHUNK, CHUNK), layer),
                _layer_spec((A_GROUPS, CHUNK, CHUNK), layer)]
    args = [x2, g1, w_in, lng, lnb, wmix, bmix]
    aliases = {}
    tail_shape = [jax.ShapeDtypeStruct((n, A_WIDTH), _BF16),
                  jax.ShapeDtypeStruct((n, D_MODEL), _BF16),
                  jax.ShapeDtypeStruct((n, D_MODEL), _BF16)]
    tail_specs = [row(A_WIDTH), row(D_MODEL), row(D_MODEL)]
    nb = None
    if prompt:
        nb = seq // MOBA_BLOCK
        blocks = tm // MOBA_BLOCK
        tps = seq // tm
        assert tm % MOBA_BLOCK == 0 and seq % tm == 0 and nb <= _K_LANES - _PAIR_LANES
        nseq = n // seq
        k_cols = B_HEADS // _HEADS_PER_PAIR * _K_LANES
        nat = jax.ShapeDtypeStruct((depth * nseq, B_WIDTH, seq), _F32)
        nat_spec = pl.BlockSpec((1, B_WIDTH, tm), lambda i: (layer * nseq + i // tps, 0, i % tps))
        out_shape = [jax.ShapeDtypeStruct((n, B_WIDTH), _F32),
                     jax.ShapeDtypeStruct((n, k_cols), _BF16),
                     jax.ShapeDtypeStruct((nseq, nb, B_HEADS * _V_ROWS, MOBA_BLOCK), _BF16),
                     jax.ShapeDtypeStruct((nt, blocks, B_WIDTH), _F32), nat, nat] + tail_shape
        out_specs = [row(B_WIDTH), row(k_cols),
                     pl.BlockSpec((1, blocks, B_HEADS * _V_ROWS, MOBA_BLOCK),
                                  lambda i: (i // tps, i % tps, 0, 0)),
                     pl.BlockSpec((1, blocks, B_WIDTH), lambda i: (i, 0, 0)), nat_spec, nat_spec
                     ] + tail_specs
        if kv_prev is not None:
            in_specs += [pl.BlockSpec(memory_space=pl.ANY)] * 2
            args += list(kv_prev)
            aliases = {7: 4, 8: 5}
    else:
        out_shape = [jax.ShapeDtypeStruct((n, B_WIDTH), _F32)] * 3 + tail_shape + [
            jax.ShapeDtypeStruct((n, A_WIDTH), _F32)]
        out_specs = [row(B_WIDTH)] * 3 + tail_specs + [row(A_WIDTH)]
    return pl.pallas_call(
        functools.partial(_proj_kernel, tm=tm, prompt=prompt, nb=nb),
        grid=(nt,),
        in_specs=in_specs,
        out_specs=out_specs,
        out_shape=out_shape,
        input_output_aliases=aliases,
        compiler_params=pltpu.CompilerParams(dimension_semantics=("arbitrary",),
                                             vmem_limit_bytes=_VMEM_LIMIT),
        name="proj_mixer_a",
    )(*args)


def _topk_block_mask(gate, own, nb, width):
    blk = lax.broadcasted_iota(jnp.int32, (nb, width), 0)
    gm = jnp.where(blk < own, gate, -jnp.inf)
    rank = jnp.zeros((nb, width), _F32)
    for jp in range(nb):
        r = gm[jp:jp + 1, :]
        beats = jnp.where(r > gm, 1.0, jnp.where(r == gm, jnp.where(blk > jp, 1.0, 0.0), 0.0))
        rank = rank + beats
    past_sel = jnp.where(blk < own, jnp.where(rank < MOBA_TOPK, 1.0, 0.0), 0.0)
    return jnp.where(blk == own, 1.0, past_sel)


def _moba_prompt_kernel(q_ref, k_ref, vt_ref, kmean_ref, btab_ref, o_ref, s_scr, *, nb):
    i = pl.program_id(2)
    blk = MOBA_BLOCK
    heads = range(_PAIRS_PER_STEP * _HEADS_PER_PAIR)

    qt = q_ref[0].T
    feat = lax.broadcasted_iota(jnp.int32, (_PAIR_LANES, blk), 0)
    qz = []
    for hh in heads:
        pair, sub = divmod(hh, _HEADS_PER_PAIR)
        in_head = (feat >= sub * B_HEAD_DIM) & (feat < (sub + 1) * B_HEAD_DIM)
        qh = jnp.where(in_head, qt[pair * _PAIR_LANES:(pair + 1) * _PAIR_LANES], 0.0)
        gate = jnp.dot(kmean_ref[0, :, pair * _PAIR_LANES:(pair + 1) * _PAIR_LANES], qh,
                       preferred_element_type=_F32, precision=lax.Precision.HIGHEST)
        mask_rows = jnp.where(_topk_block_mask(gate, i, nb, blk) > 0.5, 0.0, _NEG)
        pad = jnp.zeros((_K_LANES - _PAIR_LANES - nb, blk), _F32)
        qz.append(jnp.concatenate([qh * (_SCALE * _LOG2E), mask_rows, pad], axis=0).astype(_BF16))

    sweeps, done = [], 0
    for width in _SWEEP_WIDTHS:
        last_width = width == _SWEEP_WIDTHS[-1]
        n = lax.div(i + (width if last_width else 1) - done, width)
        sweeps.append((n, done, width))
        done = done + n * width
    first_head = pl.program_id(1) * len(heads)

    def slots(first, it, width):
        js = [first + it * width + u for u in range(width)]
        kinds = [jnp.where(j == i, _TAB_OWN, jnp.where(j == i - 1, _TAB_PREV,
                                                       jnp.where(j > i, _TAB_DEAD, _TAB_FAR)))
                 for j in js]
        return js, [jnp.minimum(j, i) for j in js], kinds

    def table_row(hh, kind):
        per_head = (first_head + hh) * _HEAD_TABLES + kind
        return jnp.where(kind < _HEAD_TABLES, per_head, B_HEADS * _HEAD_TABLES + kind - _HEAD_TABLES)

    def logits_pass(it, ms, group, first, width):
        js, jcs, tabs = slots(first, it, width)
        out = []
        for hh, m in zip(group, ms):
            pair = hh // _HEADS_PER_PAIR
            for j, jc, tab in zip(js, jcs, tabs):
                kj = k_ref[0, pl.ds(pl.multiple_of(jc * blk, blk), blk),
                           pair * _K_LANES:(pair + 1) * _K_LANES]
                s = jnp.dot(kj, qz[hh], preferred_element_type=_F32) + btab_ref[table_row(hh, tab)]
                s_scr[hh, j] = s
                m = jnp.maximum(m, jnp.max(s, axis=0, keepdims=True))
            out.append(m)
        return tuple(out)

    def value_pass(it, accs, group, ms, first, width):
        js, jcs, _ = slots(first, it, width)
        out = []
        for hh, acc, m in zip(group, accs, ms):
            for j, jc in zip(js, jcs):
                p = jnp.exp2(s_scr[hh, j] - m).astype(_BF16)
                acc = acc + jnp.dot(vt_ref[0, jc, hh * _V_ROWS:(hh + 1) * _V_ROWS, :], p,
                                    preferred_element_type=_F32)
            out.append(acc)
        return tuple(out)

    group = tuple(heads)
    ms = tuple(jnp.full((1, blk), _NEG, _F32) for _ in group)
    for n, first, width in sweeps:
        ms = lax.fori_loop(0, n, lambda it, c, f=first, w=width: logits_pass(it, c, group, f, w), ms)
    accs = tuple(jnp.zeros((_V_ROWS, blk), _F32) for _ in group)
    for n, first, width in sweeps:
        accs = lax.fori_loop(
            0, n, lambda it, c, f=first, w=width: value_pass(it, c, group, ms, f, w), accs)
    outs = [acc[:B_HEAD_DIM] / acc[B_HEAD_DIM:B_HEAD_DIM + 1] for acc in accs]
    o_ref[0] = jnp.concatenate(outs, axis=0).T.astype(_BF16)


def _moba_prompt_call(q3, kb3, vt4, kmean3, btab):
    b, t, _ = q3.shape
    nb = t // MOBA_BLOCK
    g = _PAIRS_PER_STEP
    heads = g * _HEADS_PER_PAIR
    return pl.pallas_call(
        functools.partial(_moba_prompt_kernel, nb=nb),
        grid=(b, B_HEADS // heads, nb),
        in_specs=[pl.BlockSpec((1, MOBA_BLOCK, g * _PAIR_LANES), lambda bb, hp, i: (bb, i, hp)),
                  pl.BlockSpec((1, t, g * _K_LANES), lambda bb, hp, i: (bb, 0, hp)),
                  pl.BlockSpec((1, nb, heads * _V_ROWS, MOBA_BLOCK),
                               lambda bb, hp, i: (bb, 0, hp, 0)),
                  pl.BlockSpec((1, nb, g * _PAIR_LANES), lambda bb, hp, i: (bb, 0, hp)),
                  _const_spec(btab.shape)],
        out_specs=pl.BlockSpec((1, MOBA_BLOCK, g * _PAIR_LANES), lambda bb, hp, i: (bb, i, hp)),
        out_shape=jax.ShapeDtypeStruct((b, t, B_WIDTH), _BF16),
        scratch_shapes=[pltpu.VMEM((heads, nb + _SWEEP_WIDTHS[-1] - 1, MOBA_BLOCK, MOBA_BLOCK),
                                   _F32)],
        compiler_params=pltpu.CompilerParams(
            dimension_semantics=("arbitrary", "arbitrary", "arbitrary"),
            vmem_limit_bytes=_VMEM_LIMIT),
        name="moba_prompt",
    )(q3, kb3, vt4, kmean3, btab)


def _moba_sample_kernel(pt_ref, qbd_ref, knew_ref, vnew_ref, bnear_ref, bfar_ref,
                        bownt_ref, hmaskt_ref, rsum_ref, ck_hbm, cv_hbm, o_ref,
                        gate_scr, m_scr, l_scr, acc_scr, st_scr, qbdt_scr, kbuf, vbuf, sems,
                        *, nb_past, page_base):
    pages = _PAGES_PER_STEP
    c = pl.program_id(1)
    step = pl.program_id(0) * pl.num_programs(1) + c
    last_step = pl.num_programs(0) * pl.num_programs(1) - 1
    rows = qbd_ref.shape[1]
    tdec = o_ref.shape[1]
    blocks_per_step = pages // _PAGES_PER_BLOCK

    ahead = _PAGE_SLOTS - 1

    def slot_of(s):
        return lax.rem(s, _PAGE_SLOTS)

    def page_copy(kind, of_step, p):
        src_step = jnp.minimum(of_step, last_step)
        src, buf = ((ck_hbm, kbuf), (cv_hbm, vbuf))[kind]
        page = page_base + pt_ref[src_step * pages + p]
        slot = slot_of(of_step)
        return pltpu.make_async_copy(src.at[page], buf.at[slot, p], sems.at[slot, kind])

    def block_pages(bl):
        return range(bl * _PAGES_PER_BLOCK, (bl + 1) * _PAGES_PER_BLOCK)

    @pl.when(step == 0)
    def _():
        for s in range(ahead):
            for kind in range(2):
                for pg in range(pages):
                    page_copy(kind, s, pg).start()

    for kind in range(2):
        for pg in range(pages):
            page_copy(kind, step, pg).wait()
    slot = slot_of(step)

    def lanes_t(x):
        return jnp.concatenate([x, jnp.zeros((128 - x.shape[0], x.shape[1]), x.dtype)], axis=0).T

    qbd = qbd_ref[0]
    qz = (qbd * _SCALE).astype(_BF16)

    @pl.when(c == 0)
    def _():
        qbdt_scr[...] = lanes_t(qbd)

    qbdt = qbdt_scr[...]
    row_pad = jnp.zeros((128 - rows, MOBA_BLOCK), _F32)

    for bl in range(blocks_per_step):
        jg = c * blocks_per_step + bl
        kts = [kbuf[slot, pg] for pg in block_pages(bl)]
        kt = jnp.concatenate(kts, axis=1)
        kmean = jnp.sum(functools.reduce(lambda a, b: a + b, kts), axis=-1,
                        keepdims=True) * (1.0 / MOBA_BLOCK)
        gate_scr[pl.ds(jg, 1), :] = jnp.sum(qbdt * kmean, axis=0, keepdims=True)
        s = jnp.dot(qz, kt.astype(_BF16), preferred_element_type=_F32)
        s = s + jnp.where(jg == nb_past - 1, bnear_ref[...], bfar_ref[:, 0:1])
        st_scr[bl] = jnp.concatenate([s, row_pad], axis=0).T
        for pg in block_pages(bl):
            page_copy(0, step + ahead, pg).start()
    for bl in range(blocks_per_step):
        jg = c * blocks_per_step + bl
        vt = jnp.concatenate([vbuf[slot, pg] for pg in block_pages(bl)], axis=1)
        st = st_scr[bl]
        m = jnp.max(st, axis=0, keepdims=True)
        p = jnp.exp(st - m)
        m_scr[pl.ds(jg, 1), :] = m
        l_scr[pl.ds(jg, 1), :] = jnp.sum(p, axis=0, keepdims=True)
        acc_scr[jg] = jnp.dot(vt.astype(_BF16), p.astype(_BF16), preferred_element_type=_F32)
        for pg in block_pages(bl):
            page_copy(1, step + ahead, pg).start()

    @pl.when(step == last_step)
    def _():
        for s in range(1, ahead + 1):
            for kind in range(2):
                for pg in range(pages):
                    page_copy(kind, last_step + s, pg).wait()

    @pl.when(c == pl.num_programs(1) - 1)
    def _():
        sel = _topk_block_mask(gate_scr[...], nb_past, nb_past, 128) > 0.5
        m_blk = m_scr[...]
        s_own = jnp.dot(knew_ref[0], qbdt * _SCALE, preferred_element_type=_F32) + bownt_ref[...]
        m_tot = jnp.maximum(jnp.max(s_own, axis=0, keepdims=True),
                            jnp.max(jnp.where(sel, m_blk, _NEG), axis=0, keepdims=True))
        w = jnp.where(sel, jnp.exp(m_blk - m_tot), 0.0)
        p_own = jnp.exp(s_own - m_tot)
        l_tot = jnp.sum(p_own, axis=0, keepdims=True) + jnp.sum(w * l_scr[...], axis=0, keepdims=True)
        p_own = jnp.concatenate([p_own, jnp.zeros((128 - p_own.shape[0], 128), _F32)], axis=0)
        accs = [jnp.dot(lanes_t(vnew_ref[0]), p_own, preferred_element_type=_F32)]
        accs += [jnp.zeros_like(accs[0]) for _ in range(3)]
        for j in range(nb_past):
            accs[j % 4] = accs[j % 4] + w[j:j + 1, :] * acc_scr[j]
        out_t = ((accs[0] + accs[1]) + (accs[2] + accs[3])) / l_tot
        folded = lax.dot_general(rsum_ref[...], out_t * hmaskt_ref[...], (((1,), (1,)), ((), ())),
                                 preferred_element_type=_F32, precision=lax.Precision.HIGHEST)
        o_ref[0] = folded[:tdec]


def _moba_sample_call(page_flat, qbd, knew, vnew, bnear, bfar, bownt, hmaskt, rsum,
                      cache_kt, cache_vt, *, layer, n_pool, n_pages, tdec):
    nbatch, rows, _ = qbd.shape
    pages = _PAGES_PER_STEP
    nb_past = n_pages // _PAGES_PER_BLOCK
    assert nb_past % 8 == 0 and rows <= 128
    assert nbatch * (n_pages // pages) >= _PAGE_SLOTS

    per_batch = lambda arr: pl.BlockSpec((1,) + arr.shape[1:], lambda bb, cc, pt: (bb, 0, 0))
    const = lambda arr: pl.BlockSpec(arr.shape, lambda bb, cc, pt: (0,) * arr.ndim)
    page_ring = pltpu.VMEM((_PAGE_SLOTS, pages, B_WIDTH, CHUNK), _F32)
    grid_spec = pltpu.PrefetchScalarGridSpec(
        num_scalar_prefetch=1,
        grid=(nbatch, n_pages // pages),
        in_specs=[per_batch(qbd), per_batch(knew), per_batch(vnew),
                  const(bnear), const(bfar), const(bownt), const(hmaskt), const(rsum),
                  pl.BlockSpec(memory_space=pl.ANY), pl.BlockSpec(memory_space=pl.ANY)],
        out_specs=pl.BlockSpec((1, tdec, B_WIDTH), lambda bb, cc, pt: (bb, 0, 0)),
        scratch_shapes=[pltpu.VMEM((nb_past, 128), _F32),
                        pltpu.VMEM((nb_past, 128), _F32),
                        pltpu.VMEM((nb_past, 128), _F32),
                        pltpu.VMEM((nb_past, B_WIDTH, 128), _F32),
                        pltpu.VMEM((pages // _PAGES_PER_BLOCK, MOBA_BLOCK, 128), _F32),
                        pltpu.VMEM((B_WIDTH, 128), _F32),
                        page_ring, page_ring,
                        pltpu.SemaphoreType.DMA((_PAGE_SLOTS, 2))])
    return pl.pallas_call(
        functools.partial(_moba_sample_kernel, nb_past=nb_past, page_base=layer * n_pool),
        grid_spec=grid_spec,
        out_shape=jax.ShapeDtypeStruct((nbatch, tdec, B_WIDTH), _F32),
        compiler_params=pltpu.CompilerParams(dimension_semantics=("arbitrary", "arbitrary"),
                                             vmem_limit_bytes=_VMEM_LIMIT),
        name="moba_sample",
    )(page_flat, qbd, knew, vnew, bnear, bfar, bownt, hmaskt, rsum, cache_kt, cache_vt)


def _post_kernel(x_ref, ya_ref, yb_ref, sga_ref, sgb_ref, wpa_ref, wpb_ref, wo_ref, g2_ref,
                 wup_ref, cw_ref, cb_ref, wdn_ref, gf_ref, *rest, tm, tiles_per_seq, per_row_state,
                 final_norm):
    if per_row_state:
        s1_ref, s2_ref, tpos_ref = rest[:3]
        rest = rest[3:]
    else:
        st_ref = rest[0]
        rest = rest[1:]
    xo_ref, a_ref = rest[:2]
    rest = rest[2:]
    if final_norm:
        y_ref = rest[0]
        rest = rest[1:]
    if not per_row_state:
        carry_scr = rest[0]

    ma = jnp.dot(ya_ref[...], wpa_ref[...], preferred_element_type=_F32)
    mb = jnp.dot(yb_ref[...], wpb_ref[...], preferred_element_type=_F32)
    merged = (_sigmoid(sga_ref[...].astype(_F32)) * ma
              + _sigmoid(sgb_ref[...].astype(_F32)) * mb)
    x1 = x_ref[...] + jnp.dot(merged.astype(_BF16), wo_ref[...], preferred_element_type=_F32)

    xn = _rms(x1, g2_ref[...]).astype(_BF16)
    if not per_row_state:
        @pl.when(pl.program_id(0) % tiles_per_seq == 0)
        def _():
            carry_scr[...] = st_ref[0]

    x2 = x1
    for lo, hi in zip(_FFN_SPLITS[:-1], _FFN_SPLITS[1:]):
        cols = slice(lo, hi)
        width = hi - lo
        a = jnp.dot(xn, wup_ref[:, cols], preferred_element_type=_F32)
        gate = jnp.dot(xn, wup_ref[:, D_FF + lo:D_FF + hi], preferred_element_type=_F32)
        prev1 = pltpu.roll(a, 1, 0)
        prev2 = pltpu.roll(a, 2, 0)
        if per_row_state:
            tpos = tpos_ref[:, cols]
            prev1 = jnp.where(tpos >= 1, prev1, s1_ref[:, cols])
            prev2 = jnp.where(tpos >= 2, prev2, s2_ref[:, cols])
            a_ref[:, cols] = a
        else:
            row = lax.broadcasted_iota(jnp.int32, (8, width), 0)
            c0 = carry_scr[0:1, cols]
            c1 = carry_scr[1:2, cols]
            top1 = jnp.where(row == 0, c1, prev1[:8])
            top2 = jnp.where(row == 0, c0, jnp.where(row == 1, c1, prev2[:8]))
            prev1 = jnp.concatenate([top1, prev1[8:]], axis=0)
            prev2 = jnp.concatenate([top2, prev2[8:]], axis=0)
            tail = a[tm - (CONV_W - 1):, :]
            carry_scr[:, cols] = tail
            a_ref[0, :, cols] = tail
        conv = (cb_ref[:, cols] + prev2 * cw_ref[0:1, cols] + prev1 * cw_ref[1:2, cols]
                + a * cw_ref[2:3, cols])
        act = (_gelu(conv) * gate).astype(_BF16)
        x2 = x2 + jnp.dot(act, wdn_ref[cols, :], preferred_element_type=_F32)
    xo_ref[...] = x2
    if final_norm:
        y_ref[...] = _rms(x2, gf_ref[...])


def _post_call(x2, ya, yb, sga, sgb, wpa, wpb, wo, g2, wup, cw, cb, wdn, gf, state_args, *,
               layer, tm, tiles_per_seq, per_row_state, final_norm):
    n = x2.shape[0]
    row = lambda width: pl.BlockSpec((tm, width), lambda i: (i, 0))
    in_specs = [row(D_MODEL), row(A_WIDTH), row(B_WIDTH), row(D_MODEL), row(D_MODEL),
                _layer_spec((A_WIDTH, D_MODEL), layer), _layer_spec((B_WIDTH, D_MODEL), layer),
                _layer_spec((D_MODEL, D_MODEL), layer), _const_spec((1, D_MODEL)),
                _layer_spec((D_MODEL, 2 * D_FF), layer), _const_spec((CONV_W, D_FF)),
                _const_spec((1, D_FF)), _layer_spec((D_FF, D_MODEL), layer),
                _const_spec((1, D_MODEL))]
    out_shape = [jax.ShapeDtypeStruct((n, D_MODEL), _F32)]
    out_specs = [row(D_MODEL)]
    scratch = []
    if per_row_state:
        in_specs += [row(D_FF), row(D_FF), row(D_FF)]
        out_shape.append(jax.ShapeDtypeStruct((n, D_FF), _F32))
        out_specs.append(row(D_FF))
    else:
        nseq = n // (tm * tiles_per_seq)
        in_specs.append(pl.BlockSpec((1, CONV_W - 1, D_FF), lambda i: (i // tiles_per_seq, 0, 0)))
        out_shape.append(jax.ShapeDtypeStruct((nseq, CONV_W - 1, D_FF), _F32))
        out_specs.append(pl.BlockSpec((1, CONV_W - 1, D_FF), lambda i: (i // tiles_per_seq, 0, 0)))
        scratch.append(pltpu.VMEM((CONV_W - 1, D_FF), _F32))
    if final_norm:
        out_shape.append(jax.ShapeDtypeStruct((n, D_MODEL), _F32))
        out_specs.append(row(D_MODEL))
    return pl.pallas_call(
        functools.partial(_post_kernel, tm=tm, tiles_per_seq=tiles_per_seq,
                          per_row_state=per_row_state, final_norm=final_norm),
        grid=(n // tm,),
        in_specs=in_specs,
        out_specs=out_specs,
        out_shape=out_shape,
        scratch_shapes=scratch,
        compiler_params=pltpu.CompilerParams(dimension_semantics=("arbitrary",),
                                             vmem_limit_bytes=_VMEM_LIMIT),
        name="merge_convffn",
    )(x2, ya, yb, sga, sgb, wpa, wpb, wo, g2, wup, cw, cb, wdn, gf, *state_args)


def _t5_bucket(rel):
    max_exact = N_BUCKETS // 2
    relf = jnp.maximum(rel, 1).astype(_F32)
    large = max_exact + (jnp.log(relf / max_exact) / math.log(MAX_DISTANCE / max_exact)
                         * (N_BUCKETS - max_exact)).astype(jnp.int32)
    large = jnp.minimum(large, N_BUCKETS - 1)
    return jnp.where(rel < max_exact, rel, large)


def _far_bucket_is_last(min_rel):
    max_exact = N_BUCKETS // 2
    v = max_exact + int(math.log(min_rel / max_exact) / math.log(MAX_DISTANCE / max_exact)
                        * (N_BUCKETS - max_exact) - 1e-6)
    return v >= N_BUCKETS - 1


def _toeplitz(diag, n):
    width = 2 * n
    flat = jnp.tile(diag, (1,) * (diag.ndim - 1) + (n,))
    skew = flat[..., :n * (width - 1)].reshape(diag.shape[:-1] + (n, width - 1))
    return skew[..., n - 1:2 * n - 1]


def _prompt_bias_tables(rel_bias):
    assert _far_bucket_is_last(MOBA_BLOCK + 1)
    assert (_TAB_OWN, _TAB_PREV, _TAB_FAR, _TAB_DEAD) == (0, 1, 2, 3) and _HEAD_TABLES == 2
    bt = rel_bias.T.astype(_F32)
    bt = (bt - bt[:, N_BUCKETS - 1:]) * _LOG2E
    rel = jnp.arange(2 * MOBA_BLOCK, dtype=jnp.int32) - (MOBA_BLOCK - 1)
    own = jnp.where(rel >= 0, bt[:, _t5_bucket(jnp.maximum(rel, 0))], _NEG)
    prev = bt[:, _t5_bucket(rel + MOBA_BLOCK)]
    per_head = jnp.stack([_toeplitz(own, MOBA_BLOCK), _toeplitz(prev, MOBA_BLOCK)], axis=1)
    per_head = per_head.reshape(B_HEADS * _HEAD_TABLES, MOBA_BLOCK, MOBA_BLOCK)
    shared = jnp.stack([jnp.zeros((MOBA_BLOCK, MOBA_BLOCK), _F32),
                        jnp.full((MOBA_BLOCK, MOBA_BLOCK), _NEG, _F32)])
    return jnp.concatenate([per_head, shared], axis=0)


def _sample_bias_tables(rel_bias, tdec, tpad):
    assert _far_bucket_is_last(MOBA_BLOCK + 1)
    bt = rel_bias.T.astype(_F32)
    tok = jnp.arange(tdec, dtype=jnp.int32)[:, None]
    key = jnp.arange(MOBA_BLOCK, dtype=jnp.int32)[None, :]
    near = bt[:, _t5_bucket(MOBA_BLOCK + tok - key)]
    far = jnp.broadcast_to(bt[:, N_BUCKETS - 1][:, None, None], (B_HEADS, tdec, 128))
    new = jnp.arange(tpad, dtype=jnp.int32)[None, :]
    own = jnp.where((tok - new >= 0) & (new < tdec), bt[:, _t5_bucket(jnp.maximum(tok - new, 0))],
                    _NEG)
    flat = lambda a: a.reshape(B_HEADS * tdec, a.shape[-1])
    return flat(near), flat(far), flat(own)


def kernel(x_prompt, x_sample, cache_k, cache_v, state_conv, page_table, w_in, ln_v_g, ln_v_b, w_s,
           b_s, w_pa, w_pb, w_o, norm1_g, norm2_g, w_up, conv_w, conv_b, w_down, rel_bias, norm_f):
    nbp_, seq, _ = x_prompt.shape
    nbs, tdec, _ = x_sample.shape
    depth, n_pool, page_size = cache_k.shape[:3]
    n_pages = page_table.shape[1]
    ns = nbs * tdec
    np_ = nbp_ * seq
    assert page_size == CHUNK and n_pages % _PAGES_PER_STEP == 0 and seq % MOBA_BLOCK == 0
    assert _PAGES_PER_STEP % _PAGES_PER_BLOCK == 0 and seq % _POST_TILE == 0
    assert CONV_W - 1 <= tdec <= CHUNK and ns % CHUNK == 0 and CHUNK % tdec == 0
    tpad = -(-tdec // 8) * 8
    tm_p = _PROJ_TILE
    nb = seq // MOBA_BLOCK
    nb_past = n_pages // _PAGES_PER_BLOCK
    rows = B_HEADS * tdec

    xp = x_prompt.reshape(np_, D_MODEL)
    xs = x_sample.reshape(ns, D_MODEL)
    cache_kt = cache_k.transpose(0, 1, 3, 4, 2).reshape(depth * n_pool, B_WIDTH, page_size)
    cache_vt = cache_v.transpose(0, 1, 3, 4, 2).reshape(depth * n_pool, B_WIDTH, page_size)
    page_flat = page_table.reshape(-1).astype(jnp.int32)

    btab = _prompt_bias_tables(rel_bias)
    bnear_s, bfar_s, bown_s = _sample_bias_tables(rel_bias, tdec, tpad)
    row_head = jnp.arange(rows, dtype=jnp.int32)[:, None] // tdec
    hmask = (jnp.arange(B_WIDTH, dtype=jnp.int32)[None, :] // B_HEAD_DIM == row_head).astype(_F32)
    rsum = (jnp.arange(rows, dtype=jnp.int32)[None, :] % tdec
            == jnp.arange(tdec, dtype=jnp.int32)[:, None]).astype(_F32)
    lane_pad = lambda a: jnp.pad(a, ((0, 0), (0, 128 - rows)))
    bownt_s, hmaskt = lane_pad(bown_s.T), lane_pad(hmask.T)
    rsum_p = lane_pad(jnp.pad(rsum, ((0, tpad - tdec), (0, 0))))
    tpos = jnp.broadcast_to((jnp.arange(ns, dtype=jnp.int32) % tdec)[:, None], (ns, D_FF))
    tril = jnp.tril(jnp.ones((CHUNK, CHUNK), bool))
    idx = jnp.arange(CHUNK, dtype=jnp.int32)
    same_chunk = tril & (idx[:, None] // tdec == idx[None, :] // tdec)
    zero_state = jnp.zeros((nbp_, CONV_W - 1, D_FF), _F32)
    w_in_b, wpa, wpb, wo = (w.astype(_BF16) for w in (w_in, w_pa, w_pb, w_o))
    wup, wdn = w_up.astype(_BF16), w_down.astype(_BF16)
    gf = norm_f[None, :]
    mix_shape = (depth, A_GROUPS, CHUNK, CHUNK)
    reps = CHUNK // tdec
    wmix_p = jnp.where(tril, w_s, 0.0).astype(_BF16)
    bmix_p = jnp.broadcast_to(b_s[..., None], mix_shape)
    wmix_s = jnp.zeros(mix_shape, _F32)
    for t in range(tdec):
        for s in range(t + 1):
            here = same_chunk & (idx[:, None] % tdec == t) & (idx[None, :] % tdec == s)
            wmix_s = jnp.where(here, w_s[:, :, t, s][..., None, None], wmix_s)
    wmix_s = wmix_s.astype(_BF16)
    bmix_s = jnp.broadcast_to(jnp.tile(b_s[:, :, :tdec], (1, 1, reps))[..., None], mix_shape)

    ks_l, vs_l, as_l, cp_l, cs_l = [], [], [], [], []
    yp = ys = kv_nat = None
    for l in range(depth):
        last = l == depth - 1
        g1 = norm1_g[l][None, :]
        lng, lnb = ln_v_g[l][None, :], ln_v_b[l][None, :]
        g2 = norm2_g[l][None, :]
        cw, cb = conv_w[l], conv_b[l][None, :]

        (qp, kbp, vtp, kmeanp, knat, vnat, yap, sgap, sgbp) = _proj_call(
            xp, g1, w_in_b, lng, lnb, wmix_p, bmix_p, tm=tm_p, prompt=True, layer=l, depth=depth,
            seq=seq, kv_prev=kv_nat)
        kv_nat = (knat, vnat)
        ybp = _moba_prompt_call(qp.reshape(nbp_, seq, B_WIDTH), kbp.reshape(nbp_, seq, -1),
                                vtp, kmeanp.reshape(nbp_, nb, B_WIDTH), btab)
        outs = _post_call(xp, yap, ybp.reshape(np_, B_WIDTH), sgap, sgbp, wpa, wpb, wo, g2,
                          wup, cw, cb, wdn, gf, (zero_state,), layer=l, tm=_POST_TILE,
                          tiles_per_seq=seq // _POST_TILE, per_row_state=False, final_norm=last)
        xp, cp = outs[0], outs[1]
        if last:
            yp = outs[2]

        qs, ks, vs, yas, sgas, sgbs, vas = _proj_call(xs, g1, w_in_b, lng, lnb, wmix_s, bmix_s,
                                                      tm=CHUNK, prompt=False, layer=l)
        ks4 = ks.reshape(nbs, tdec, B_HEADS, B_HEAD_DIM)
        vs4 = vs.reshape(nbs, tdec, B_HEADS, B_HEAD_DIM)
        qbd = (qs.reshape(nbs, 1, tdec, B_WIDTH) * hmask.reshape(B_HEADS, tdec, B_WIDTH)[None]
               ).reshape(nbs, rows, B_WIDTH)
        pad = ((0, 0), (0, tpad - tdec), (0, 0))
        knew = jnp.pad(ks.reshape(nbs, tdec, B_WIDTH), pad)
        vnew = jnp.pad(vs.reshape(nbs, tdec, B_WIDTH), pad)
        ybs = _moba_sample_call(page_flat, qbd, knew, vnew, bnear_s, bfar_s, bownt_s, hmaskt,
                                rsum_p, cache_kt, cache_vt, layer=l, n_pool=n_pool, n_pages=n_pages,
                                tdec=tdec)
        ybs = ybs.reshape(ns, B_WIDTH)
        st = state_conv[l]
        s1 = jnp.broadcast_to(st[:, 1:2, :], (nbs, tdec, D_FF)).reshape(ns, D_FF)
        s2 = jnp.concatenate([st, jnp.zeros((nbs, tdec - (CONV_W - 1), D_FF), _F32)],
                             axis=1).reshape(ns, D_FF)
        outs = _post_call(xs, yas, ybs.astype(_BF16), sgas, sgbs, wpa, wpb, wo,
                          g2, wup, cw, cb, wdn, gf, (s1, s2, tpos), layer=l, tm=CHUNK,
                          tiles_per_seq=1, per_row_state=True, final_norm=last)
        xs, a_s = outs[0], outs[1]
        if last:
            ys = outs[2]

        ks_l.append(ks4)
        vs_l.append(vs4)
        as_l.append(vas.reshape(nbs, tdec, A_WIDTH))
        cp_l.append(cp)
        a_ext = jnp.concatenate([st, a_s.reshape(nbs, tdec, D_FF)], axis=1)
        cs_l.append(a_ext[:, -(CONV_W - 1):])

    def kv_out(t):
        return t.reshape(depth, nbp_, B_HEADS, B_HEAD_DIM, seq).transpose(0, 1, 4, 2, 3)

    return (yp.reshape(nbp_, seq, D_MODEL), ys.reshape(nbs, tdec, D_MODEL),
            kv_out(kv_nat[0]), kv_out(kv_nat[1]),
            jnp.stack(ks_l, 0), jnp.stack(vs_l, 0),
            jnp.stack(as_l, 0), jnp.stack(cp_l, 0), jnp.stack(cs_l, 0))
```

```python
import functools
import math

import jax
import jax.numpy as jnp
from jax import lax
from jax.experimental import pallas as pl
from jax.experimental.pallas import tpu as pltpu

D_MODEL = 1024
CHUNK = 128
A_WIDTH = D_MODEL // 2
A_GROUPS = 4
A_GROUP_DIM = A_WIDTH // A_GROUPS
B_HEADS = 8
B_HEAD_DIM = 64
B_WIDTH = B_HEADS * B_HEAD_DIM
MOBA_BLOCK = 256
MOBA_TOPK = 3
N_BUCKETS = 32
MAX_DISTANCE = 128
D_FF = 2816
CONV_W = 3
EPS = 1e-6
N_IN = 2 * A_WIDTH + 3 * B_WIDTH + 2 * D_MODEL

_C_U, _C_VA, _C_Q, _C_K, _C_V, _C_GA, _C_GB = 0, 512, 1024, 1536, 2048, 2560, 3584

_NEG = -1e30
_HEADS_PER_PAIR = 2
_PAIRS_PER_STEP = 2
_SWEEP_WIDTHS = (8, 4, 2, 1)
_TAB_OWN, _TAB_PREV, _TAB_FAR, _TAB_DEAD = 0, 1, 2, 3
_HEAD_TABLES = 2
_PAIR_LANES = _HEADS_PER_PAIR * B_HEAD_DIM
_K_LANES = 2 * _PAIR_LANES
_V_ROWS = B_HEAD_DIM + 16
_LOG2E = math.log2(math.e)
_PAGES_PER_STEP = 16
_PAGES_PER_BLOCK = MOBA_BLOCK // CHUNK
_PAGE_SLOTS = 3
_PROJ_TILE = 512
_POST_TILE = 512
_FFN_SPLITS = (0, 1536, D_FF)
_VMEM_LIMIT = 48 * 1024 * 1024
_SCALE = 1.0 / math.sqrt(B_HEAD_DIM)

_F32 = jnp.float32
_BF16 = jnp.bfloat16


def _gelu(x):
    k = -2.0 * 0.7978845608028654 * _LOG2E
    return x / (1.0 + jnp.exp2(x * (k + (k * 0.044715) * (x * x))))


def _sigmoid(x):
    return 1.0 / (1.0 + jnp.exp2(x * -_LOG2E))


def _rms(x, g):
    return x * lax.rsqrt(jnp.mean(x * x, axis=-1, keepdims=True) + EPS) * g


def _const_spec(shape):
    zeros = (0,) * len(shape)
    return pl.BlockSpec(shape, lambda *_: zeros, pipeline_mode=pl.Buffered(1))


def _layer_spec(shape, layer):
    index = (layer,) + (0,) * len(shape)
    return pl.BlockSpec((None,) + shape, lambda *_: index, pipeline_mode=pl.Buffered(1))


def _proj_kernel(x_ref, g1_ref, w_ref, lng_ref, lnb_ref, wmix_ref, bmix_ref, *rest, tm, prompt,
                 nb=None):
    if prompt:
        (q_ref, kb_ref, vt_ref, kmean_ref, knat_ref, vnat_ref, ya_ref, sga_ref, sgb_ref) = rest[-9:]
    else:
        (q_ref, k_ref, v_ref, ya_ref, sga_ref, sgb_ref, va_ref) = rest
    xn = _rms(x_ref[...], g1_ref[...]).astype(_BF16)

    def proj(lo, width):
        return jnp.dot(xn, w_ref[:, lo:lo + width], preferred_element_type=_F32)

    u = _gelu(proj(_C_U, A_WIDTH))
    va = _gelu(proj(_C_VA, A_WIDTH))
    mu = jnp.mean(va, axis=-1, keepdims=True)
    vc = va - mu
    var = jnp.mean(vc * vc, axis=-1, keepdims=True)
    va = vc * lax.rsqrt(var + EPS) * lng_ref[...] + lnb_ref[...]
    if not prompt:
        va_ref[...] = va
    for c in range(tm // CHUNK):
        rows = slice(c * CHUNK, (c + 1) * CHUNK)
        for g in range(A_GROUPS):
            cols = slice(g * A_GROUP_DIM, (g + 1) * A_GROUP_DIM)
            mixed = jnp.dot(wmix_ref[g], va[rows, cols].astype(_BF16),
                            preferred_element_type=_F32) + bmix_ref[g]
            ya_ref[rows, cols] = (u[rows, cols] * mixed).astype(_BF16)

    q_ref[...] = proj(_C_Q, B_WIDTH)
    k = proj(_C_K, B_WIDTH)
    v = proj(_C_V, B_WIDTH)
    if prompt:
        blocks = tm // MOBA_BLOCK
        vt = v.T
        knat_ref[0] = k.T
        vnat_ref[0] = vt
        for r in range(blocks):
            kmean_ref[0, r:r + 1, :] = jnp.mean(k[r * MOBA_BLOCK:(r + 1) * MOBA_BLOCK], axis=0,
                                                keepdims=True)
        kb = k.astype(_BF16)
        first_blk = (pl.program_id(0) * blocks) % nb
        shape = (tm, _K_LANES - _PAIR_LANES)
        blk_of_row = first_blk + lax.broadcasted_iota(jnp.int32, shape, 0) // MOBA_BLOCK
        onehot = jnp.where(lax.broadcasted_iota(jnp.int32, shape, 1) == blk_of_row, 1.0, 0.0
                           ).astype(_BF16)
        for hp in range(B_HEADS // _HEADS_PER_PAIR):
            kb_ref[:, hp * _K_LANES:hp * _K_LANES + _PAIR_LANES] = (
                kb[:, hp * _PAIR_LANES:(hp + 1) * _PAIR_LANES])
            kb_ref[:, hp * _K_LANES + _PAIR_LANES:(hp + 1) * _K_LANES] = onehot
        vtb = vt.astype(_BF16)
        row = lax.broadcasted_iota(jnp.int32, (_V_ROWS - B_HEAD_DIM, MOBA_BLOCK), 0)
        ones_row = jnp.where(row == 0, 1.0, 0.0).astype(_BF16)
        for r in range(blocks):
            keys = slice(r * MOBA_BLOCK, (r + 1) * MOBA_BLOCK)
            for h in range(B_HEADS):
                vt_ref[0, r, h * _V_ROWS:h * _V_ROWS + B_HEAD_DIM, :] = (
                    vtb[h * B_HEAD_DIM:(h + 1) * B_HEAD_DIM, keys])
                vt_ref[0, r, h * _V_ROWS + B_HEAD_DIM:(h + 1) * _V_ROWS, :] = ones_row
    else:
        k_ref[...] = k
        v_ref[...] = v
    sga_ref[...] = proj(_C_GA, D_MODEL).astype(_BF16)
    sgb_ref[...] = proj(_C_GB, D_MODEL).astype(_BF16)


def _proj_call(x2, g1, w_in, lng, lnb, wmix, bmix, *, tm, prompt, layer=0, depth=1, seq=None,
               kv_prev=None):
    n = x2.shape[0]
    nt = n // tm
    row = lambda width: pl.BlockSpec((tm, width), lambda i: (i, 0))
    in_specs = [row(D_MODEL), _const_spec((1, D_MODEL)), _layer_spec((D_MODEL, N_IN), layer),
                _const_spec((1, A_WIDTH)), _const_spec((1, A_WIDTH)),
                _layer_spec((A_GROUPS, CHUNK, CHUNK), layer),
                _layer_spec((A_GROUPS, CHUNK, CHUNK), layer)]
    args = [x2, g1, w_in, lng, lnb, wmix, bmix]
    aliases = {}
    tail_shape = [jax.ShapeDtypeStruct((n, A_WIDTH), _BF16),
                  jax.ShapeDtypeStruct((n, D_MODEL), _BF16),
                  jax.ShapeDtypeStruct((n, D_MODEL), _BF16)]
    tail_specs = [row(A_WIDTH), row(D_MODEL), row(D_MODEL)]
    nb = None
    if prompt:
        nb = seq // MOBA_BLOCK
        blocks = tm // MOBA_BLOCK
        tps = seq // tm
        assert tm % MOBA_BLOCK == 0 and seq % tm == 0 and nb <= _K_LANES - _PAIR_LANES
        nseq = n // seq
        k_cols = B_HEADS // _HEADS_PER_PAIR * _K_LANES
        nat = jax.ShapeDtypeStruct((depth * nseq, B_WIDTH, seq), _F32)
        nat_spec = pl.BlockSpec((1, B_WIDTH, tm), lambda i: (layer * nseq + i // tps, 0, i % tps))
        out_shape = [jax.ShapeDtypeStruct((n, B_WIDTH), _F32),
                     jax.ShapeDtypeStruct((n, k_cols), _BF16),
                     jax.ShapeDtypeStruct((nseq, nb, B_HEADS * _V_ROWS, MOBA_BLOCK), _BF16),
                     jax.ShapeDtypeStruct((nt, blocks, B_WIDTH), _F32), nat, nat] + tail_shape
        out_specs = [row(B_WIDTH), row(k_cols),
                     pl.BlockSpec((1, blocks, B_HEADS * _V_ROWS, MOBA_BLOCK),
                                  lambda i: (i // tps, i % tps, 0, 0)),
                     pl.BlockSpec((1, blocks, B_WIDTH), lambda i: (i, 0, 0)), nat_spec, nat_spec
                     ] + tail_specs
        if kv_prev is not None:
            in_specs += [pl.BlockSpec(memory_space=pl.ANY)] * 2
            args += list(kv_prev)
            aliases = {7: 4, 8: 5}
    else:
        out_shape = [jax.ShapeDtypeStruct((n, B_WIDTH), _F32)] * 3 + tail_shape + [
            jax.ShapeDtypeStruct((n, A_WIDTH), _F32)]
        out_specs = [row(B_WIDTH)] * 3 + tail_specs + [row(A_WIDTH)]
    return pl.pallas_call(
        functools.partial(_proj_kernel, tm=tm, prompt=prompt, nb=nb),
        grid=(nt,),
        in_specs=in_specs,
        out_specs=out_specs,
        out_shape=out_shape,
        input_output_aliases=aliases,
        compiler_params=pltpu.CompilerParams(dimension_semantics=("arbitrary",),
                                             vmem_limit_bytes=_VMEM_LIMIT),
        name="proj_mixer_a",
    )(*args)


def _topk_block_mask(gate, own, nb, width):
    blk = lax.broadcasted_iota(jnp.int32, (nb, width), 0)
    gm = jnp.where(blk < own, gate, -jnp.inf)
    rank = jnp.zeros((nb, width), _F32)
    for jp in range(nb):
        r = gm[jp:jp + 1, :]
        beats = jnp.where(r > gm, 1.0, jnp.where(r == gm, jnp.where(blk > jp, 1.0, 0.0), 0.0))
        rank = rank + beats
    past_sel = jnp.where(blk < own, jnp.where(rank < MOBA_TOPK, 1.0, 0.0), 0.0)
    return jnp.where(blk == own, 1.0, past_sel)


def _moba_prompt_kernel(q_ref, k_ref, vt_ref, kmean_ref, btab_ref, o_ref, s_scr, *, nb):
    i = pl.program_id(2)
    blk = MOBA_BLOCK
    heads = range(_PAIRS_PER_STEP * _HEADS_PER_PAIR)

    qt = q_ref[0].T
    feat = lax.broadcasted_iota(jnp.int32, (_PAIR_LANES, blk), 0)
    qz = []
    for hh in heads:
        pair, sub = divmod(hh, _HEADS_PER_PAIR)
        in_head = (feat >= sub * B_HEAD_DIM) & (feat < (sub + 1) * B_HEAD_DIM)
        qh = jnp.where(in_head, qt[pair * _PAIR_LANES:(pair + 1) * _PAIR_LANES], 0.0)
        gate = jnp.dot(kmean_ref[0, :, pair * _PAIR_LANES:(pair + 1) * _PAIR_LANES], qh,
                       preferred_element_type=_F32, precision=lax.Precision.HIGHEST)
        mask_rows = jnp.where(_topk_block_mask(gate, i, nb, blk) > 0.5, 0.0, _NEG)
        pad = jnp.zeros((_K_LANES - _PAIR_LANES - nb, blk), _F32)
        qz.append(jnp.concatenate([qh * (_SCALE * _LOG2E), mask_rows, pad], axis=0).astype(_BF16))

    sweeps, done = [], 0
    for width in _SWEEP_WIDTHS:
        last_width = width == _SWEEP_WIDTHS[-1]
        n = lax.div(i + (width if last_width else 1) - done, width)
        sweeps.append((n, done, width))
        done = done + n * width
    first_head = pl.program_id(1) * len(heads)

    def slots(first, it, width):
        js = [first + it * width + u for u in range(width)]
        kinds = [jnp.where(j == i, _TAB_OWN, jnp.where(j == i - 1, _TAB_PREV,
                                                       jnp.where(j > i, _TAB_DEAD, _TAB_FAR)))
                 for j in js]
        return js, [jnp.minimum(j, i) for j in js], kinds

    def table_row(hh, kind):
        per_head = (first_head + hh) * _HEAD_TABLES + kind
        return jnp.where(kind < _HEAD_TABLES, per_head, B_HEADS * _HEAD_TABLES + kind - _HEAD_TABLES)

    def logits_pass(it, ms, group, first, width):
        js, jcs, tabs = slots(first, it, width)
        out = []
        for hh, m in zip(group, ms):
            pair = hh // _HEADS_PER_PAIR
            for j, jc, tab in zip(js, jcs, tabs):
                kj = k_ref[0, pl.ds(pl.multiple_of(jc * blk, blk), blk),
                           pair * _K_LANES:(pair + 1) * _K_LANES]
                s = jnp.dot(kj, qz[hh], preferred_element_type=_F32) + btab_ref[table_row(hh, tab)]
                s_scr[hh, j] = s
                m = jnp.maximum(m, jnp.max(s, axis=0, keepdims=True))
            out.append(m)
        return tuple(out)

    def value_pass(it, accs, group, ms, first, width):
        js, jcs, _ = slots(first, it, width)
        out = []
        for hh, acc, m in zip(group, accs, ms):
            for j, jc in zip(js, jcs):
                p = jnp.exp2(s_scr[hh, j] - m).astype(_BF16)
                acc = acc + jnp.dot(vt_ref[0, jc, hh * _V_ROWS:(hh + 1) * _V_ROWS, :], p,
                                    preferred_element_type=_F32)
            out.append(acc)
        return tuple(out)

    group = tuple(heads)
    ms = tuple(jnp.full((1, blk), _NEG, _F32) for _ in group)
    for n, first, width in sweeps:
        ms = lax.fori_loop(0, n, lambda it, c, f=first, w=width: logits_pass(it, c, group, f, w), ms)
    accs = tuple(jnp.zeros((_V_ROWS, blk), _F32) for _ in group)
    for n, first, width in sweeps:
        accs = lax.fori_loop(
            0, n, lambda it, c, f=first, w=width: value_pass(it, c, group, ms, f, w), accs)
    outs = [acc[:B_HEAD_DIM] / acc[B_HEAD_DIM:B_HEAD_DIM + 1] for acc in accs]
    o_ref[0] = jnp.concatenate(outs, axis=0).T.astype(_BF16)


def _moba_prompt_call(q3, kb3, vt4, kmean3, btab):
    b, t, _ = q3.shape
    nb = t // MOBA_BLOCK
    g = _PAIRS_PER_STEP
    heads = g * _HEADS_PER_PAIR
    return pl.pallas_call(
        functools.partial(_moba_prompt_kernel, nb=nb),
        grid=(b, B_HEADS // heads, nb),
        in_specs=[pl.BlockSpec((1, MOBA_BLOCK, g * _PAIR_LANES), lambda bb, hp, i: (bb, i, hp)),
                  pl.BlockSpec((1, t, g * _K_LANES), lambda bb, hp, i: (bb, 0, hp)),
                  pl.BlockSpec((1, nb, heads * _V_ROWS, MOBA_BLOCK),
                               lambda bb, hp, i: (bb, 0, hp, 0)),
                  pl.BlockSpec((1, nb, g * _PAIR_LANES), lambda bb, hp, i: (bb, 0, hp)),
                  _const_spec(btab.shape)],
        out_specs=pl.BlockSpec((1, MOBA_BLOCK, g * _PAIR_LANES), lambda bb, hp, i: (bb, i, hp)),
        out_shape=jax.ShapeDtypeStruct((b, t, B_WIDTH), _BF16),
        scratch_shapes=[pltpu.VMEM((heads, nb + _SWEEP_WIDTHS[-1] - 1, MOBA_BLOCK, MOBA_BLOCK),
                                   _F32)],
        compiler_params=pltpu.CompilerParams(
            dimension_semantics=("arbitrary", "arbitrary", "arbitrary"),
            vmem_limit_bytes=_VMEM_LIMIT),
        name="moba_prompt",
    )(q3, kb3, vt4, kmean3, btab)


def _moba_sample_kernel(pt_ref, qbd_ref, knew_ref, vnew_ref, bnear_ref, bfar_ref,
                        bownt_ref, hmaskt_ref, rsum_ref, ck_hbm, cv_hbm, o_ref,
                        gate_scr, m_scr, l_scr, acc_scr, st_scr, qbdt_scr, kbuf, vbuf, sems,
                        *, nb_past, page_base):
    pages = _PAGES_PER_STEP
    c = pl.program_id(1)
    step = pl.program_id(0) * pl.num_programs(1) + c
    last_step = pl.num_programs(0) * pl.num_programs(1) - 1
    rows = qbd_ref.shape[1]
    tdec = o_ref.shape[1]
    blocks_per_step = pages // _PAGES_PER_BLOCK

    ahead = _PAGE_SLOTS - 1

    def slot_of(s):
        return lax.rem(s, _PAGE_SLOTS)

    def page_copy(kind, of_step, p):
        src_step = jnp.minimum(of_step, last_step)
        src, buf = ((ck_hbm, kbuf), (cv_hbm, vbuf))[kind]
        page = page_base + pt_ref[src_step * pages + p]
        slot = slot_of(of_step)
        return pltpu.make_async_copy(src.at[page], buf.at[slot, p], sems.at[slot, kind])

    def block_pages(bl):
        return range(bl * _PAGES_PER_BLOCK, (bl + 1) * _PAGES_PER_BLOCK)

    @pl.when(step == 0)
    def _():
        for s in range(ahead):
            for kind in range(2):
                for pg in range(pages):
                    page_copy(kind, s, pg).start()

    for kind in range(2):
        for pg in range(pages):
            page_copy(kind, step, pg).wait()
    slot = slot_of(step)

    def lanes_t(x):
        return jnp.concatenate([x, jnp.zeros((128 - x.shape[0], x.shape[1]), x.dtype)], axis=0).T

    qbd = qbd_ref[0]
    qz = (qbd * _SCALE).astype(_BF16)

    @pl.when(c == 0)
    def _():
        qbdt_scr[...] = lanes_t(qbd)

    qbdt = qbdt_scr[...]
    row_pad = jnp.zeros((128 - rows, MOBA_BLOCK), _F32)

    for bl in range(blocks_per_step):
        jg = c * blocks_per_step + bl
        kts = [kbuf[slot, pg] for pg in block_pages(bl)]
        kt = jnp.concatenate(kts, axis=1)
        kmean = jnp.sum(functools.reduce(lambda a, b: a + b, kts), axis=-1,
                        keepdims=True) * (1.0 / MOBA_BLOCK)
        gate_scr[pl.ds(jg, 1), :] = jnp.sum(qbdt * kmean, axis=0, keepdims=True)
        s = jnp.dot(qz, kt.astype(_BF16), preferred_element_type=_F32)
        s = s + jnp.where(jg == nb_past - 1, bnear_ref[...], bfar_ref[:, 0:1])
        st_scr[bl] = jnp.concatenate([s, row_pad], axis=0).T
        for pg in block_pages(bl):
            page_copy(0, step + ahead, pg).start()
    for bl in range(blocks_per_step):
        jg = c * blocks_per_step + bl
        vt = jnp.concatenate([vbuf[slot, pg] for pg in block_pages(bl)], axis=1)
        st = st_scr[bl]
        m = jnp.max(st, axis=0, keepdims=True)
        p = jnp.exp(st - m)
        m_scr[pl.ds(jg, 1), :] = m
        l_scr[pl.ds(jg, 1), :] = jnp.sum(p, axis=0, keepdims=True)
        acc_scr[jg] = jnp.dot(vt.astype(_BF16), p.astype(_BF16), preferred_element_type=_F32)
        for pg in block_pages(bl):
            page_copy(1, step + ahead, pg).start()

    @pl.when(step == last_step)
    def _():
        for s in range(1, ahead + 1):
            for kind in range(2):
                for pg in range(pages):
                    page_copy(kind, last_step + s, pg).wait()

    @pl.when(c == pl.num_programs(1) - 1)
    def _():
        sel = _topk_block_mask(gate_scr[...], nb_past, nb_past, 128) > 0.5
        m_blk = m_scr[...]
        s_own = jnp.dot(knew_ref[0], qbdt * _SCALE, preferred_element_type=_F32) + bownt_ref[...]
        m_tot = jnp.maximum(jnp.max(s_own, axis=0, keepdims=True),
                            jnp.max(jnp.where(sel, m_blk, _NEG), axis=0, keepdims=True))
        w = jnp.where(sel, jnp.exp(m_blk - m_tot), 0.0)
        p_own = jnp.exp(s_own - m_tot)
        l_tot = jnp.sum(p_own, axis=0, keepdims=True) + jnp.sum(w * l_scr[...], axis=0, keepdims=True)
        p_own = jnp.concatenate([p_own, jnp.zeros((128 - p_own.shape[0], 128), _F32)], axis=0)
        accs = [jnp.dot(lanes_t(vnew_ref[0]), p_own, preferred_element_type=_F32)]
        accs += [jnp.zeros_like(accs[0]) for _ in range(3)]
        for j in range(nb_past):
            accs[j % 4] = accs[j % 4] + w[j:j + 1, :] * acc_scr[j]
        out_t = ((accs[0] + accs[1]) + (accs[2] + accs[3])) / l_tot
        folded = lax.dot_general(rsum_ref[...], out_t * hmaskt_ref[...], (((1,), (1,)), ((), ())),
                                 preferred_element_type=_F32, precision=lax.Precision.HIGHEST)
        o_ref[0] = folded[:tdec]


def _moba_sample_call(page_flat, qbd, knew, vnew, bnear, bfar, bownt, hmaskt, rsum,
                      cache_kt, cache_vt, *, layer, n_pool, n_pages, tdec):
    nbatch, rows, _ = qbd.shape
    pages = _PAGES_PER_STEP
    nb_past = n_pages // _PAGES_PER_BLOCK
    assert nb_past % 8 == 0 and rows <= 128
    assert nbatch * (n_pages // pages) >= _PAGE_SLOTS

    per_batch = lambda arr: pl.BlockSpec((1,) + arr.shape[1:], lambda bb, cc, pt: (bb, 0, 0))
    const = lambda arr: pl.BlockSpec(arr.shape, lambda bb, cc, pt: (0,) * arr.ndim)
    page_ring = pltpu.VMEM((_PAGE_SLOTS, pages, B_WIDTH, CHUNK), _F32)
    grid_spec = pltpu.PrefetchScalarGridSpec(
        num_scalar_prefetch=1,
        grid=(nbatch, n_pages // pages),
        in_specs=[per_batch(qbd), per_batch(knew), per_batch(vnew),
                  const(bnear), const(bfar), const(bownt), const(hmaskt), const(rsum),
                  pl.BlockSpec(memory_space=pl.ANY), pl.BlockSpec(memory_space=pl.ANY)],
        out_specs=pl.BlockSpec((1, tdec, B_WIDTH), lambda bb, cc, pt: (bb, 0, 0)),
        scratch_shapes=[pltpu.VMEM((nb_past, 128), _F32),
                        pltpu.VMEM((nb_past, 128), _F32),
                        pltpu.VMEM((nb_past, 128), _F32),
                        pltpu.VMEM((nb_past, B_WIDTH, 128), _F32),
                        pltpu.VMEM((pages // _PAGES_PER_BLOCK, MOBA_BLOCK, 128), _F32),
                        pltpu.VMEM((B_WIDTH, 128), _F32),
                        page_ring, page_ring,
                        pltpu.SemaphoreType.DMA((_PAGE_SLOTS, 2))])
    return pl.pallas_call(
        functools.partial(_moba_sample_kernel, nb_past=nb_past, page_base=layer * n_pool),
        grid_spec=grid_spec,
        out_shape=jax.ShapeDtypeStruct((nbatch, tdec, B_WIDTH), _F32),
        compiler_params=pltpu.CompilerParams(dimension_semantics=("arbitrary", "arbitrary"),
                                             vmem_limit_bytes=_VMEM_LIMIT),
        name="moba_sample",
    )(page_flat, qbd, knew, vnew, bnear, bfar, bownt, hmaskt, rsum, cache_kt, cache_vt)


def _post_kernel(x_ref, ya_ref, yb_ref, sga_ref, sgb_ref, wpa_ref, wpb_ref, wo_ref, g2_ref,
                 wup_ref, cw_ref, cb_ref, wdn_ref, gf_ref, *rest, tm, tiles_per_seq, per_row_state,
                 final_norm):
    if per_row_state:
        s1_ref, s2_ref, tpos_ref = rest[:3]
        rest = rest[3:]
    else:
        st_ref = rest[0]
        rest = rest[1:]
    xo_ref, a_ref = rest[:2]
    rest = rest[2:]
    if final_norm:
        y_ref = rest[0]
        rest = rest[1:]
    if not per_row_state:
        carry_scr = rest[0]

    ma = jnp.dot(ya_ref[...], wpa_ref[...], preferred_element_type=_F32)
    mb = jnp.dot(yb_ref[...], wpb_ref[...], preferred_element_type=_F32)
    merged = (_sigmoid(sga_ref[...].astype(_F32)) * ma
              + _sigmoid(sgb_ref[...].astype(_F32)) * mb)
    x1 = x_ref[...] + jnp.dot(merged.astype(_BF16), wo_ref[...], preferred_element_type=_F32)

    xn = _rms(x1, g2_ref[...]).astype(_BF16)
    if not per_row_state:
        @pl.when(pl.program_id(0) % tiles_per_seq == 0)
        def _():
            carry_scr[...] = st_ref[0]

    x2 = x1
    for lo, hi in zip(_FFN_SPLITS[:-1], _FFN_SPLITS[1:]):
        cols = slice(lo, hi)
        width = hi - lo
        a = jnp.dot(xn, wup_ref[:, cols], preferred_element_type=_F32)
        gate = jnp.dot(xn, wup_ref[:, D_FF + lo:D_FF + hi], preferred_element_type=_F32)
        prev1 = pltpu.roll(a, 1, 0)
        prev2 = pltpu.roll(a, 2, 0)
        if per_row_state:
            tpos = tpos_ref[:, cols]
            prev1 = jnp.where(tpos >= 1, prev1, s1_ref[:, cols])
            prev2 = jnp.where(tpos >= 2, prev2, s2_ref[:, cols])
            a_ref[:, cols] = a
        else:
            row = lax.broadcasted_iota(jnp.int32, (8, width), 0)
            c0 = carry_scr[0:1, cols]
            c1 = carry_scr[1:2, cols]
            top1 = jnp.where(row == 0, c1, prev1[:8])
            top2 = jnp.where(row == 0, c0, jnp.where(row == 1, c1, prev2[:8]))
            prev1 = jnp.concatenate([top1, prev1[8:]], axis=0)
            prev2 = jnp.concatenate([top2, prev2[8:]], axis=0)
            tail = a[tm - (CONV_W - 1):, :]
            carry_scr[:, cols] = tail
            a_ref[0, :, cols] = tail
        conv = (cb_ref[:, cols] + prev2 * cw_ref[0:1, cols] + prev1 * cw_ref[1:2, cols]
                + a * cw_ref[2:3, cols])
        act = (_gelu(conv) * gate).astype(_BF16)
        x2 = x2 + jnp.dot(act, wdn_ref[cols, :], preferred_element_type=_F32)
    xo_ref[...] = x2
    if final_norm:
        y_ref[...] = _rms(x2, gf_ref[...])


def _post_call(x2, ya, yb, sga, sgb, wpa, wpb, wo, g2, wup, cw, cb, wdn, gf, state_args, *,
               layer, tm, tiles_per_seq, per_row_state, final_norm):
    n = x2.shape[0]
    row = lambda width: pl.BlockSpec((tm, width), lambda i: (i, 0))
    in_specs = [row(D_MODEL), row(A_WIDTH), row(B_WIDTH), row(D_MODEL), row(D_MODEL),
                _layer_spec((A_WIDTH, D_MODEL), layer), _layer_spec((B_WIDTH, D_MODEL), layer),
                _layer_spec((D_MODEL, D_MODEL), layer), _const_spec((1, D_MODEL)),
                _layer_spec((D_MODEL, 2 * D_FF), layer), _const_spec((CONV_W, D_FF)),
                _const_spec((1, D_FF)), _layer_spec((D_FF, D_MODEL), layer),
                _const_spec((1, D_MODEL))]
    out_shape = [jax.ShapeDtypeStruct((n, D_MODEL), _F32)]
    out_specs = [row(D_MODEL)]
    scratch = []
    if per_row_state:
        in_specs += [row(D_FF), row(D_FF), row(D_FF)]
        out_shape.append(jax.ShapeDtypeStruct((n, D_FF), _F32))
        out_specs.append(row(D_FF))
    else:
        nseq = n // (tm * tiles_per_seq)
        in_specs.append(pl.BlockSpec((1, CONV_W - 1, D_FF), lambda i: (i // tiles_per_seq, 0, 0)))
        out_shape.append(jax.ShapeDtypeStruct((nseq, CONV_W - 1, D_FF), _F32))
        out_specs.append(pl.BlockSpec((1, CONV_W - 1, D_FF), lambda i: (i // tiles_per_seq, 0, 0)))
        scratch.append(pltpu.VMEM((CONV_W - 1, D_FF), _F32))
    if final_norm:
        out_shape.append(jax.ShapeDtypeStruct((n, D_MODEL), _F32))
        out_specs.append(row(D_MODEL))
    return pl.pallas_call(
        functools.partial(_post_kernel, tm=tm, tiles_per_seq=tiles_per_seq,
                          per_row_state=per_row_state, final_norm=final_norm),
        grid=(n // tm,),
        in_specs=in_specs,
        out_specs=out_specs,
        out_shape=out_shape,
        scratch_shapes=scratch,
        compiler_params=pltpu.CompilerParams(dimension_semantics=("arbitrary",),
                                             vmem_limit_bytes=_VMEM_LIMIT),
        name="merge_convffn",
    )(x2, ya, yb, sga, sgb, wpa, wpb, wo, g2, wup, cw, cb, wdn, gf, *state_args)


def _t5_bucket(rel):
    max_exact = N_BUCKETS // 2
    relf = jnp.maximum(rel, 1).astype(_F32)
    large = max_exact + (jnp.log(relf / max_exact) / math.log(MAX_DISTANCE / max_exact)
                         * (N_BUCKETS - max_exact)).astype(jnp.int32)
    large = jnp.minimum(large, N_BUCKETS - 1)
    return jnp.where(rel < max_exact, rel, large)


def _far_bucket_is_last(min_rel):
    max_exact = N_BUCKETS // 2
    v = max_exact + int(math.log(min_rel / max_exact) / math.log(MAX_DISTANCE / max_exact)
                        * (N_BUCKETS - max_exact) - 1e-6)
    return v >= N_BUCKETS - 1


def _toeplitz(diag, n):
    width = 2 * n
    flat = jnp.tile(diag, (1,) * (diag.ndim - 1) + (n,))
    skew = flat[..., :n * (width - 1)].reshape(diag.shape[:-1] + (n, width - 1))
    return skew[..., n - 1:2 * n - 1]


def _prompt_bias_tables(rel_bias):
    assert _far_bucket_is_last(MOBA_BLOCK + 1)
    assert (_TAB_OWN, _TAB_PREV, _TAB_FAR, _TAB_DEAD) == (0, 1, 2, 3) and _HEAD_TABLES == 2
    bt = rel_bias.T.astype(_F32)
    bt = (bt - bt[:, N_BUCKETS - 1:]) * _LOG2E
    rel = jnp.arange(2 * MOBA_BLOCK, dtype=jnp.int32) - (MOBA_BLOCK - 1)
    own = jnp.where(rel >= 0, bt[:, _t5_bucket(jnp.maximum(rel, 0))], _NEG)
    prev = bt[:, _t5_bucket(rel + MOBA_BLOCK)]
    per_head = jnp.stack([_toeplitz(own, MOBA_BLOCK), _toeplitz(prev, MOBA_BLOCK)], axis=1)
    per_head = per_head.reshape(B_HEADS * _HEAD_TABLES, MOBA_BLOCK, MOBA_BLOCK)
    shared = jnp.stack([jnp.zeros((MOBA_BLOCK, MOBA_BLOCK), _F32),
                        jnp.full((MOBA_BLOCK, MOBA_BLOCK), _NEG, _F32)])
    return jnp.concatenate([per_head, shared], axis=0)


def _sample_bias_tables(rel_bias, tdec, tpad):
    assert _far_bucket_is_last(MOBA_BLOCK + 1)
    bt = rel_bias.T.astype(_F32)
    tok = jnp.arange(tdec, dtype=jnp.int32)[:, None]
    key = jnp.arange(MOBA_BLOCK, dtype=jnp.int32)[None, :]
    near = bt[:, _t5_bucket(MOBA_BLOCK + tok - key)]
    far = jnp.broadcast_to(bt[:, N_BUCKETS - 1][:, None, None], (B_HEADS, tdec, 128))
    new = jnp.arange(tpad, dtype=jnp.int32)[None, :]
    own = jnp.where((tok - new >= 0) & (new < tdec), bt[:, _t5_bucket(jnp.maximum(tok - new, 0))],
                    _NEG)
    flat = lambda a: a.reshape(B_HEADS * tdec, a.shape[-1])
    return flat(near), flat(far), flat(own)


def kernel(x_prompt, x_sample, cache_k, cache_v, state_conv, page_table, w_in, ln_v_g, ln_v_b, w_s,
           b_s, w_pa, w_pb, w_o, norm1_g, norm2_g, w_up, conv_w, conv_b, w_down, rel_bias, norm_f):
    nbp_, seq, _ = x_prompt.shape
    nbs, tdec, _ = x_sample.shape
    depth, n_pool, page_size = cache_k.shape[:3]
    n_pages = page_table.shape[1]
    ns = nbs * tdec
    np_ = nbp_ * seq
    assert page_size == CHUNK and n_pages % _PAGES_PER_STEP == 0 and seq % MOBA_BLOCK == 0
    assert _PAGES_PER_STEP % _PAGES_PER_BLOCK == 0 and seq % _POST_TILE == 0
    assert CONV_W - 1 <= tdec <= CHUNK and ns % CHUNK == 0 and CHUNK % tdec == 0
    tpad = -(-tdec // 8) * 8
    tm_p = _PROJ_TILE
    nb = seq // MOBA_BLOCK
    nb_past = n_pages // _PAGES_PER_BLOCK
    rows = B_HEADS * tdec

    xp = x_prompt.reshape(np_, D_MODEL)
    xs = x_sample.reshape(ns, D_MODEL)
    cache_kt = cache_k.transpose(0, 1, 3, 4, 2).reshape(depth * n_pool, B_WIDTH, page_size)
    cache_vt = cache_v.transpose(0, 1, 3, 4, 2).reshape(depth * n_pool, B_WIDTH, page_size)
    page_flat = page_table.reshape(-1).astype(jnp.int32)

    btab = _prompt_bias_tables(rel_bias)
    bnear_s, bfar_s, bown_s = _sample_bias_tables(rel_bias, tdec, tpad)
    row_head = jnp.arange(rows, dtype=jnp.int32)[:, None] // tdec
    hmask = (jnp.arange(B_WIDTH, dtype=jnp.int32)[None, :] // B_HEAD_DIM == row_head).astype(_F32)
    rsum = (jnp.arange(rows, dtype=jnp.int32)[None, :] % tdec
            == jnp.arange(tdec, dtype=jnp.int32)[:, None]).astype(_F32)
    lane_pad = lambda a: jnp.pad(a, ((0, 0), (0, 128 - rows)))
    bownt_s, hmaskt = lane_pad(bown_s.T), lane_pad(hmask.T)
    rsum_p = lane_pad(jnp.pad(rsum, ((0, tpad - tdec), (0, 0))))
    tpos = jnp.broadcast_to((jnp.arange(ns, dtype=jnp.int32) % tdec)[:, None], (ns, D_FF))
    tril = jnp.tril(jnp.ones((CHUNK, CHUNK), bool))
    idx = jnp.arange(CHUNK, dtype=jnp.int32)
    same_chunk = tril & (idx[:, None] // tdec == idx[None, :] // tdec)
    zero_state = jnp.zeros((nbp_, CONV_W - 1, D_FF), _F32)
    w_in_b, wpa, wpb, wo = (w.astype(_BF16) for w in (w_in, w_pa, w_pb, w_o))
    wup, wdn = w_up.astype(_BF16), w_down.astype(_BF16)
    gf = norm_f[None, :]
    mix_shape = (depth, A_GROUPS, CHUNK, CHUNK)
    reps = CHUNK // tdec
    wmix_p = jnp.where(tril, w_s, 0.0).astype(_BF16)
    bmix_p = jnp.broadcast_to(b_s[..., None], mix_shape)
    wmix_s = jnp.zeros(mix_shape, _F32)
    for t in range(tdec):
        for s in range(t + 1):
            here = same_chunk & (idx[:, None] % tdec == t) & (idx[None, :] % tdec == s)
            wmix_s = jnp.where(here, w_s[:, :, t, s][..., None, None], wmix_s)
    wmix_s = wmix_s.astype(_BF16)
    bmix_s = jnp.broadcast_to(jnp.tile(b_s[:, :, :tdec], (1, 1, reps))[..., None], mix_shape)

    ks_l, vs_l, as_l, cp_l, cs_l = [], [], [], [], []
    yp = ys = kv_nat = None
    for l in range(depth):
        last = l == depth - 1
        g1 = norm1_g[l][None, :]
        lng, lnb = ln_v_g[l][None, :], ln_v_b[l][None, :]
        g2 = norm2_g[l][None, :]
        cw, cb = conv_w[l], conv_b[l][None, :]

        (qp, kbp, vtp, kmeanp, knat, vnat, yap, sgap, sgbp) = _proj_call(
            xp, g1, w_in_b, lng, lnb, wmix_p, bmix_p, tm=tm_p, prompt=True, layer=l, depth=depth,
            seq=seq, kv_prev=kv_nat)
        kv_nat = (knat, vnat)
        ybp = _moba_prompt_call(qp.reshape(nbp_, seq, B_WIDTH), kbp.reshape(nbp_, seq, -1),
                                vtp, kmeanp.reshape(nbp_, nb, B_WIDTH), btab)
        outs = _post_call(xp, yap, ybp.reshape(np_, B_WIDTH), sgap, sgbp, wpa, wpb, wo, g2,
                          wup, cw, cb, wdn, gf, (zero_state,), layer=l, tm=_POST_TILE,
                          tiles_per_seq=seq // _POST_TILE, per_row_state=False, final_norm=last)
        xp, cp = outs[0], outs[1]
        if last:
            yp = outs[2]

        qs, ks, vs, yas, sgas, sgbs, vas = _proj_call(xs, g1, w_in_b, lng, lnb, wmix_s, bmix_s,
                                                      tm=CHUNK, prompt=False, layer=l)
        ks4 = ks.reshape(nbs, tdec, B_HEADS, B_HEAD_DIM)
        vs4 = vs.reshape(nbs, tdec, B_HEADS, B_HEAD_DIM)
        qbd = (qs.reshape(nbs, 1, tdec, B_WIDTH) * hmask.reshape(B_HEADS, tdec, B_WIDTH)[None]
               ).reshape(nbs, rows, B_WIDTH)
        pad = ((0, 0), (0, tpad - tdec), (0, 0))
        knew = jnp.pad(ks.reshape(nbs, tdec, B_WIDTH), pad)
        vnew = jnp.pad(vs.reshape(nbs, tdec, B_WIDTH), pad)
        ybs = _moba_sample_call(page_flat, qbd, knew, vnew, bnear_s, bfar_s, bownt_s, hmaskt,
                                rsum_p, cache_kt, cache_vt, layer=l, n_pool=n_pool, n_pages=n_pages,
                                tdec=tdec)
        ybs = ybs.reshape(ns, B_WIDTH)
        st = state_conv[l]
        s1 = jnp.broadcast_to(st[:, 1:2, :], (nbs, tdec, D_FF)).reshape(ns, D_FF)
        s2 = jnp.concatenate([st, jnp.zeros((nbs, tdec - (CONV_W - 1), D_FF), _F32)],
                             axis=1).reshape(ns, D_FF)
        outs = _post_call(xs, yas, ybs.astype(_BF16), sgas, sgbs, wpa, wpb, wo,
                          g2, wup, cw, cb, wdn, gf, (s1, s2, tpos), layer=l, tm=CHUNK,
                          tiles_per_seq=1, per_row_state=True, final_norm=last)
        xs, a_s = outs[0], outs[1]
        if last:
            ys = outs[2]

        ks_l.append(ks4)
        vs_l.append(vs4)
        as_l.append(vas.reshape(nbs, tdec, A_WIDTH))
        cp_l.append(cp)
        a_ext = jnp.concatenate([st, a_s.reshape(nbs, tdec, D_FF)], axis=1)
        cs_l.append(a_ext[:, -(CONV_W - 1):])

    def kv_out(t):
        return t.reshape(depth, nbp_, B_HEADS, B_HEAD_DIM, seq).transpose(0, 1, 4, 2, 3)

    return (yp.reshape(nbp_, seq, D_MODEL), ys.reshape(nbs, tdec, D_MODEL),
            kv_out(kv_nat[0]), kv_out(kv_nat[1]),
            jnp.stack(ks_l, 0), jnp.stack(vs_l, 0),
            jnp.stack(as_l, 0), jnp.stack(cp_l, 0), jnp.stack(cs_l, 0))
```

```python
import functools
import math

import jax
import jax.numpy as jnp
from jax import lax
from jax.experimental import pallas as pl
from jax.experimental.pallas import tpu as pltpu

D_MODEL = 1024
CHUNK = 128
A_WIDTH = D_MODEL // 2
A_GROUPS = 4
A_GROUP_DIM = A_WIDTH // A_GROUPS
B_HEADS = 8
B_HEAD_DIM = 64
B_WIDTH = B_HEADS * B_HEAD_DIM
MOBA_BLOCK = 256
MOBA_TOPK = 3
N_BUCKETS = 32
MAX_DISTANCE = 128
D_FF = 2816
CONV_W = 3
EPS = 1e-6
N_IN = 2 * A_WIDTH + 3 * B_WIDTH + 2 * D_MODEL

_C_U, _C_VA, _C_Q, _C_K, _C_V, _C_GA, _C_GB = 0, 512, 1024, 1536, 2048, 2560, 3584

_NEG = -1e30
_HEADS_PER_PAIR = 2
_PAIRS_PER_STEP = 2
_SWEEP_WIDTHS = (8, 4, 2, 1)
_TAB_OWN, _TAB_PREV, _TAB_FAR, _TAB_DEAD = 0, 1, 2, 3
_HEAD_TABLES = 2
_PAIR_LANES = _HEADS_PER_PAIR * B_HEAD_DIM
_K_LANES = 2 * _PAIR_LANES
_V_ROWS = B_HEAD_DIM + 16
_LOG2E = math.log2(math.e)
_PAGES_PER_STEP = 16
_PAGES_PER_BLOCK = MOBA_BLOCK // CHUNK
_PAGE_SLOTS = 4
_PROJ_TILE = 512
_POST_TILE = 512
_FFN_SPLITS = (0, 1536, D_FF)
_VMEM_LIMIT = 48 * 1024 * 1024
_VMEM_LIMIT_SAMPLE = 56 * 1024 * 1024
_SCALE = 1.0 / math.sqrt(B_HEAD_DIM)

_F32 = jnp.float32
_BF16 = jnp.bfloat16


def _gelu(x):
    k = -2.0 * 0.7978845608028654 * _LOG2E
    return x / (1.0 + jnp.exp2(x * (k + (k * 0.044715) * (x * x))))


def _sigmoid(x):
    return 1.0 / (1.0 + jnp.exp2(x * -_LOG2E))


def _rms(x, g):
    return x * lax.rsqrt(jnp.mean(x * x, axis=-1, keepdims=True) + EPS) * g


def _const_spec(shape):
    zeros = (0,) * len(shape)
    return pl.BlockSpec(shape, lambda *_: zeros, pipeline_mode=pl.Buffered(1))


def _layer_spec(shape, layer):
    index = (layer,) + (0,) * len(shape)
    return pl.BlockSpec((None,) + shape, lambda *_: index, pipeline_mode=pl.Buffered(1))


def _proj_kernel(x_ref, g1_ref, w_ref, lng_ref, lnb_ref, wmix_ref, bmix_ref, *rest, tm, prompt,
                 nb=None):
    if prompt:
        (q_ref, kb_ref, vt_ref, kmean_ref, knat_ref, vnat_ref, ya_ref, sga_ref, sgb_ref) = rest[-9:]
    else:
        (q_ref, k_ref, v_ref, ya_ref, sga_ref, sgb_ref, va_ref) = rest
    xn = _rms(x_ref[...], g1_ref[...]).astype(_BF16)

    def proj(lo, width):
        return jnp.dot(xn, w_ref[:, lo:lo + width], preferred_element_type=_F32)

    u = _gelu(proj(_C_U, A_WIDTH))
    va = _gelu(proj(_C_VA, A_WIDTH))
    mu = jnp.mean(va, axis=-1, keepdims=True)
    vc = va - mu
    var = jnp.mean(vc * vc, axis=-1, keepdims=True)
    va = vc * lax.rsqrt(var + EPS) * lng_ref[...] + lnb_ref[...]
    if not prompt:
        va_ref[...] = va
    for c in range(tm // CHUNK):
        rows = slice(c * CHUNK, (c + 1) * CHUNK)
        for g in range(A_GROUPS):
            cols = slice(g * A_GROUP_DIM, (g + 1) * A_GROUP_DIM)
            mixed = jnp.dot(wmix_ref[g], va[rows, cols].astype(_BF16),
                            preferred_element_type=_F32) + bmix_ref[g]
            ya_ref[rows, cols] = (u[rows, cols] * mixed).astype(_BF16)

    q_ref[...] = proj(_C_Q, B_WIDTH)
    k = proj(_C_K, B_WIDTH)
    v = proj(_C_V, B_WIDTH)
    if prompt:
        blocks = tm // MOBA_BLOCK
        vt = v.T
        knat_ref[0] = k.T
        vnat_ref[0] = vt
        for r in range(blocks):
            kmean_ref[0, r:r + 1, :] = jnp.mean(k[r * MOBA_BLOCK:(r + 1) * MOBA_BLOCK], axis=0,
                                                keepdims=True)
        kb = k.astype(_BF16)
        first_blk = (pl.program_id(0) * blocks) % nb
        shape = (tm, _K_LANES - _PAIR_LANES)
        blk_of_row = first_blk + lax.broadcasted_iota(jnp.int32, shape, 0) // MOBA_BLOCK
        onehot = jnp.where(lax.broadcasted_iota(jnp.int32, shape, 1) == blk_of_row, 1.0, 0.0
                           ).astype(_BF16)
        for hp in range(B_HEADS // _HEADS_PER_PAIR):
            kb_ref[:, hp * _K_LANES:hp * _K_LANES + _PAIR_LANES] = (
                kb[:, hp * _PAIR_LANES:(hp + 1) * _PAIR_LANES])
            kb_ref[:, hp * _K_LANES + _PAIR_LANES:(hp + 1) * _K_LANES] = onehot
        vtb = vt.astype(_BF16)
        row = lax.broadcasted_iota(jnp.int32, (_V_ROWS - B_HEAD_DIM, MOBA_BLOCK), 0)
        ones_row = jnp.where(row == 0, 1.0, 0.0).astype(_BF16)
        for r in range(blocks):
            keys = slice(r * MOBA_BLOCK, (r + 1) * MOBA_BLOCK)
            for h in range(B_HEADS):
                vt_ref[0, r, h * _V_ROWS:h * _V_ROWS + B_HEAD_DIM, :] = (
                    vtb[h * B_HEAD_DIM:(h + 1) * B_HEAD_DIM, keys])
                vt_ref[0, r, h * _V_ROWS + B_HEAD_DIM:(h + 1) * _V_ROWS, :] = ones_row
    else:
        k_ref[...] = k
        v_ref[...] = v
    sga_ref[...] = proj(_C_GA, D_MODEL).astype(_BF16)
    sgb_ref[...] = proj(_C_GB, D_MODEL).astype(_BF16)


def _proj_call(x2, g1, w_in, lng, lnb, wmix, bmix, *, tm, prompt, layer=0, depth=1, seq=None,
               kv_prev=None):
    n = x2.shape[0]
    nt = n // tm
    row = lambda width: pl.BlockSpec((tm, width), lambda i: (i, 0))
    in_specs = [row(D_MODEL), _const_spec((1, D_MODEL)), _layer_spec((D_MODEL, N_IN), layer),
                _const_spec((1, A_WIDTH)), _const_spec((1, A_WIDTH)),
                _layer_spec((A_GROUPS, CHUNK, CHUNK), layer),
                _layer_spec((A_GROUPS, CHUNK, CHUNK), layer)]
    args = [x2, g1, w_in, lng, lnb, wmix, bmix]
    aliases = {}
    tail_shape = [jax.ShapeDtypeStruct((n, A_WIDTH), _BF16),
                  jax.ShapeDtypeStruct((n, D_MODEL), _BF16),
                  jax.ShapeDtypeStruct((n, D_MODEL), _BF16)]
    tail_specs = [row(A_WIDTH), row(D_MODEL), row(D_MODEL)]
    nb = None
    if prompt:
        nb = seq // MOBA_BLOCK
        blocks = tm // MOBA_BLOCK
        tps = seq // tm
        assert tm % MOBA_BLOCK == 0 and seq % tm == 0 and nb <= _K_LANES - _PAIR_LANES
        nseq = n // seq
        k_cols = B_HEADS // _HEADS_PER_PAIR * _K_LANES
        nat = jax.ShapeDtypeStruct((depth * nseq, B_WIDTH, seq), _F32)
        nat_spec = pl.BlockSpec((1, B_WIDTH, tm), lambda i: (layer * nseq + i // tps, 0, i % tps))
        out_shape = [jax.ShapeDtypeStruct((n, B_WIDTH), _F32),
                     jax.ShapeDtypeStruct((n, k_cols), _BF16),
                     jax.ShapeDtypeStruct((nseq, nb, B_HEADS * _V_ROWS, MOBA_BLOCK), _BF16),
                     jax.ShapeDtypeStruct((nt, blocks, B_WIDTH), _F32), nat, nat] + tail_shape
        out_specs = [row(B_WIDTH), row(k_cols),
                     pl.BlockSpec((1, blocks, B_HEADS * _V_ROWS, MOBA_BLOCK),
                                  lambda i: (i // tps, i % tps, 0, 0)),
                     pl.BlockSpec((1, blocks, B_WIDTH), lambda i: (i, 0, 0)), nat_spec, nat_spec
                     ] + tail_specs
        if kv_prev is not None:
            in_specs += [pl.BlockSpec(memory_space=pl.ANY)] * 2
            args += list(kv_prev)
            aliases = {7: 4, 8: 5}
    else:
        out_shape = [jax.ShapeDtypeStruct((n, B_WIDTH), _F32)] * 3 + tail_shape + [
            jax.ShapeDtypeStruct((n, A_WIDTH), _F32)]
        out_specs = [row(B_WIDTH)] * 3 + tail_specs + [row(A_WIDTH)]
    return pl.pallas_call(
        functools.partial(_proj_kernel, tm=tm, prompt=prompt, nb=nb),
        grid=(nt,),
        in_specs=in_specs,
        out_specs=out_specs,
        out_shape=out_shape,
        input_output_aliases=aliases,
        compiler_params=pltpu.CompilerParams(dimension_semantics=("arbitrary",),
                                             vmem_limit_bytes=_VMEM_LIMIT),
        name="proj_mixer_a",
    )(*args)


def _topk_block_mask(gate, own, nb, width):
    blk = lax.broadcasted_iota(jnp.int32, (nb, width), 0)
    gm = jnp.where(blk < own, gate, -jnp.inf)
    rank = jnp.zeros((nb, width), _F32)
    for jp in range(nb):
        r = gm[jp:jp + 1, :]
        beats = jnp.where(r > gm, 1.0, jnp.where(r == gm, jnp.where(blk > jp, 1.0, 0.0), 0.0))
        rank = rank + beats
    past_sel = jnp.where(blk < own, jnp.where(rank < MOBA_TOPK, 1.0, 0.0), 0.0)
    return jnp.where(blk == own, 1.0, past_sel)


def _moba_prompt_kernel(q_ref, k_ref, vt_ref, kmean_ref, btab_ref, o_ref, s_scr, *, nb):
    i = pl.program_id(2)
    blk = MOBA_BLOCK
    heads = range(_PAIRS_PER_STEP * _HEADS_PER_PAIR)

    qt = q_ref[0].T
    feat = lax.broadcasted_iota(jnp.int32, (_PAIR_LANES, blk), 0)
    qz = []
    for hh in heads:
        pair, sub = divmod(hh, _HEADS_PER_PAIR)
        in_head = (feat >= sub * B_HEAD_DIM) & (feat < (sub + 1) * B_HEAD_DIM)
        qh = jnp.where(in_head, qt[pair * _PAIR_LANES:(pair + 1) * _PAIR_LANES], 0.0)
        gate = jnp.dot(kmean_ref[0, :, pair * _PAIR_LANES:(pair + 1) * _PAIR_LANES], qh,
                       preferred_element_type=_F32, precision=lax.Precision.HIGHEST)
        mask_rows = jnp.where(_topk_block_mask(gate, i, nb, blk) > 0.5, 0.0, _NEG)
        pad = jnp.zeros((_K_LANES - _PAIR_LANES - nb, blk), _F32)
        qz.append(jnp.concatenate([qh * (_SCALE * _LOG2E), mask_rows, pad], axis=0).astype(_BF16))

    sweeps, done = [], 0
    for width in _SWEEP_WIDTHS:
        last_width = width == _SWEEP_WIDTHS[-1]
        n = lax.div(i + (width if last_width else 1) - done, width)
        sweeps.append((n, done, width))
        done = done + n * width
    first_head = pl.program_id(1) * len(heads)

    def slots(first, it, width):
        js = [first + it * width + u for u in range(width)]
        kinds = [jnp.where(j == i, _TAB_OWN, jnp.where(j == i - 1, _TAB_PREV,
                                                       jnp.where(j > i, _TAB_DEAD, _TAB_FAR)))
                 for j in js]
        return js, [jnp.minimum(j, i) for j in js], kinds

    def table_row(hh, kind):
        per_head = (first_head + hh) * _HEAD_TABLES + kind
        return jnp.where(kind < _HEAD_TABLES, per_head, B_HEADS * _HEAD_TABLES + kind - _HEAD_TABLES)

    def logits_pass(it, ms, group, first, width):
        js, jcs, tabs = slots(first, it, width)
        out = []
        for hh, m in zip(group, ms):
            pair = hh // _HEADS_PER_PAIR
            for j, jc, tab in zip(js, jcs, tabs):
                kj = k_ref[0, pl.ds(pl.multiple_of(jc * blk, blk), blk),
                           pair * _K_LANES:(pair + 1) * _K_LANES]
                s = jnp.dot(kj, qz[hh], preferred_element_type=_F32) + btab_ref[table_row(hh, tab)]
                s_scr[hh, j] = s
                m = jnp.maximum(m, jnp.max(s, axis=0, keepdims=True))
            out.append(m)
        return tuple(out)

    def value_pass(it, accs, group, ms, first, width):
        js, jcs, _ = slots(first, it, width)
        out = []
        for hh, acc, m in zip(group, accs, ms):
            for j, jc in zip(js, jcs):
                p = jnp.exp2(s_scr[hh, j] - m).astype(_BF16)
                acc = acc + jnp.dot(vt_ref[0, jc, hh * _V_ROWS:(hh + 1) * _V_ROWS, :], p,
                                    preferred_element_type=_F32)
            out.append(acc)
        return tuple(out)

    group = tuple(heads)
    ms = tuple(jnp.full((1, blk), _NEG, _F32) for _ in group)
    for n, first, width in sweeps:
        ms = lax.fori_loop(0, n, lambda it, c, f=first, w=width: logits_pass(it, c, group, f, w), ms)
    accs = tuple(jnp.zeros((_V_ROWS, blk), _F32) for _ in group)
    for n, first, width in sweeps:
        accs = lax.fori_loop(
            0, n, lambda it, c, f=first, w=width: value_pass(it, c, group, ms, f, w), accs)
    outs = [acc[:B_HEAD_DIM] / acc[B_HEAD_DIM:B_HEAD_DIM + 1] for acc in accs]
    o_ref[0] = jnp.concatenate(outs, axis=0).T.astype(_BF16)


def _moba_prompt_call(q3, kb3, vt4, kmean3, btab):
    b, t, _ = q3.shape
    nb = t // MOBA_BLOCK
    g = _PAIRS_PER_STEP
    heads = g * _HEADS_PER_PAIR
    return pl.pallas_call(
        functools.partial(_moba_prompt_kernel, nb=nb),
        grid=(b, B_HEADS // heads, nb),
        in_specs=[pl.BlockSpec((1, MOBA_BLOCK, g * _PAIR_LANES), lambda bb, hp, i: (bb, i, hp)),
                  pl.BlockSpec((1, t, g * _K_LANES), lambda bb, hp, i: (bb, 0, hp)),
                  pl.BlockSpec((1, nb, heads * _V_ROWS, MOBA_BLOCK),
                               lambda bb, hp, i: (bb, 0, hp, 0)),
                  pl.BlockSpec((1, nb, g * _PAIR_LANES), lambda bb, hp, i: (bb, 0, hp)),
                  _const_spec(btab.shape)],
        out_specs=pl.BlockSpec((1, MOBA_BLOCK, g * _PAIR_LANES), lambda bb, hp, i: (bb, i, hp)),
        out_shape=jax.ShapeDtypeStruct((b, t, B_WIDTH), _BF16),
        scratch_shapes=[pltpu.VMEM((heads, nb + _SWEEP_WIDTHS[-1] - 1, MOBA_BLOCK, MOBA_BLOCK),
                                   _F32)],
        compiler_params=pltpu.CompilerParams(
            dimension_semantics=("arbitrary", "arbitrary", "arbitrary"),
            vmem_limit_bytes=_VMEM_LIMIT),
        name="moba_prompt",
    )(q3, kb3, vt4, kmean3, btab)


def _moba_sample_kernel(pt_ref, qbd_ref, knew_ref, vnew_ref, bnear_ref, bfar_ref,
                        bownt_ref, hmaskt_ref, rsum_ref, ck_hbm, cv_hbm, o_ref,
                        gate_scr, m_scr, l_scr, acc_scr, st_scr, qbdt_scr, kbuf, vbuf, sems,
                        *, nb_past, page_base):
    pages = _PAGES_PER_STEP
    c = pl.program_id(1)
    step = pl.program_id(0) * pl.num_programs(1) + c
    last_step = pl.num_programs(0) * pl.num_programs(1) - 1
    rows = qbd_ref.shape[1]
    tdec = o_ref.shape[1]
    blocks_per_step = pages // _PAGES_PER_BLOCK

    ahead = _PAGE_SLOTS - 1

    def slot_of(s):
        return lax.rem(s, _PAGE_SLOTS)

    def page_copy(kind, of_step, p):
        src_step = jnp.minimum(of_step, last_step)
        src, buf = ((ck_hbm, kbuf), (cv_hbm, vbuf))[kind]
        page = page_base + pt_ref[src_step * pages + p]
        slot = slot_of(of_step)
        return pltpu.make_async_copy(src.at[page], buf.at[slot, p], sems.at[slot, kind])

    def block_pages(bl):
        return range(bl * _PAGES_PER_BLOCK, (bl + 1) * _PAGES_PER_BLOCK)

    @pl.when(step == 0)
    def _():
        for s in range(ahead):
            for kind in range(2):
                for pg in range(pages):
                    page_copy(kind, s, pg).start()

    for kind in range(2):
        for pg in range(pages):
            page_copy(kind, step, pg).wait()
    slot = slot_of(step)

    def lanes_t(x):
        return jnp.concatenate([x, jnp.zeros((128 - x.shape[0], x.shape[1]), x.dtype)], axis=0).T

    qbd = qbd_ref[0]
    qz = (qbd * _SCALE).astype(_BF16)

    @pl.when(c == 0)
    def _():
        qbdt_scr[...] = lanes_t(qbd)

    qbdt = qbdt_scr[...]
    row_pad = jnp.zeros((128 - rows, MOBA_BLOCK), _F32)

    for bl in range(blocks_per_step):
        jg = c * blocks_per_step + bl
        kts = [kbuf[slot, pg] for pg in block_pages(bl)]
        kt = jnp.concatenate(kts, axis=1)
        kmean = jnp.sum(functools.reduce(lambda a, b: a + b, kts), axis=-1,
                        keepdims=True) * (1.0 / MOBA_BLOCK)
        gate_scr[pl.ds(jg, 1), :] = jnp.sum(qbdt * kmean, axis=0, keepdims=True)
        s = jnp.dot(qz, kt.astype(_BF16), preferred_element_type=_F32)
        s = s + jnp.where(jg == nb_past - 1, bnear_ref[...], bfar_ref[:, 0:1])
        st_scr[bl] = jnp.concatenate([s, row_pad], axis=0).T
        for pg in block_pages(bl):
            page_copy(0, step + ahead, pg).start()
    for bl in range(blocks_per_step):
        jg = c * blocks_per_step + bl
        vt = jnp.concatenate([vbuf[slot, pg] for pg in block_pages(bl)], axis=1)
        st = st_scr[bl]
        m = jnp.max(st, axis=0, keepdims=True)
        p = jnp.exp(st - m)
        m_scr[pl.ds(jg, 1), :] = m
        l_scr[pl.ds(jg, 1), :] = jnp.sum(p, axis=0, keepdims=True)
        acc_scr[jg] = jnp.dot(vt.astype(_BF16), p.astype(_BF16), preferred_element_type=_F32)
        for pg in block_pages(bl):
            page_copy(1, step + ahead, pg).start()

    @pl.when(step == last_step)
    def _():
        for s in range(1, ahead + 1):
            for kind in range(2):
                for pg in range(pages):
                    page_copy(kind, last_step + s, pg).wait()

    @pl.when(c == pl.num_programs(1) - 1)
    def _():
        sel = _topk_block_mask(gate_scr[...], nb_past, nb_past, 128) > 0.5
        m_blk = m_scr[...]
        s_own = jnp.dot(knew_ref[0], qbdt * _SCALE, preferred_element_type=_F32) + bownt_ref[...]
        m_tot = jnp.maximum(jnp.max(s_own, axis=0, keepdims=True),
                            jnp.max(jnp.where(sel, m_blk, _NEG), axis=0, keepdims=True))
        w = jnp.where(sel, jnp.exp(m_blk - m_tot), 0.0)
        p_own = jnp.exp(s_own - m_tot)
        l_tot = jnp.sum(p_own, axis=0, keepdims=True) + jnp.sum(w * l_scr[...], axis=0, keepdims=True)
        p_own = jnp.concatenate([p_own, jnp.zeros((128 - p_own.shape[0], 128), _F32)], axis=0)
        accs = [jnp.dot(lanes_t(vnew_ref[0]), p_own, preferred_element_type=_F32)]
        accs += [jnp.zeros_like(accs[0]) for _ in range(3)]
        for j in range(nb_past):
            accs[j % 4] = accs[j % 4] + w[j:j + 1, :] * acc_scr[j]
        out_t = ((accs[0] + accs[1]) + (accs[2] + accs[3])) / l_tot
        folded = lax.dot_general(rsum_ref[...], out_t * hmaskt_ref[...], (((1,), (1,)), ((), ())),
                                 preferred_element_type=_F32, precision=lax.Precision.HIGHEST)
        o_ref[0] = folded[:tdec]


def _moba_sample_call(page_flat, qbd, knew, vnew, bnear, bfar, bownt, hmaskt, rsum,
                      cache_kt, cache_vt, *, layer, n_pool, n_pages, tdec):
    nbatch, rows, _ = qbd.shape
    pages = _PAGES_PER_STEP
    nb_past = n_pages // _PAGES_PER_BLOCK
    assert nb_past % 8 == 0 and rows <= 128
    assert nbatch * (n_pages // pages) >= _PAGE_SLOTS

    per_batch = lambda arr: pl.BlockSpec((1,) + arr.shape[1:], lambda bb, cc, pt: (bb, 0, 0))
    const = lambda arr: pl.BlockSpec(arr.shape, lambda bb, cc, pt: (0,) * arr.ndim)
    page_ring = pltpu.VMEM((_PAGE_SLOTS, pages, B_WIDTH, CHUNK), _F32)
    grid_spec = pltpu.PrefetchScalarGridSpec(
        num_scalar_prefetch=1,
        grid=(nbatch, n_pages // pages),
        in_specs=[per_batch(qbd), per_batch(knew), per_batch(vnew),
                  const(bnear), const(bfar), const(bownt), const(hmaskt), const(rsum),
                  pl.BlockSpec(memory_space=pl.ANY), pl.BlockSpec(memory_space=pl.ANY)],
        out_specs=pl.BlockSpec((1, tdec, B_WIDTH), lambda bb, cc, pt: (bb, 0, 0)),
        scratch_shapes=[pltpu.VMEM((nb_past, 128), _F32),
                        pltpu.VMEM((nb_past, 128), _F32),
                        pltpu.VMEM((nb_past, 128), _F32),
                        pltpu.VMEM((nb_past, B_WIDTH, 128), _F32),
                        pltpu.VMEM((pages // _PAGES_PER_BLOCK, MOBA_BLOCK, 128), _F32),
                        pltpu.VMEM((B_WIDTH, 128), _F32),
                        page_ring, page_ring,
                        pltpu.SemaphoreType.DMA((_PAGE_SLOTS, 2))])
    return pl.pallas_call(
        functools.partial(_moba_sample_kernel, nb_past=nb_past, page_base=layer * n_pool),
        grid_spec=grid_spec,
        out_shape=jax.ShapeDtypeStruct((nbatch, tdec, B_WIDTH), _F32),
        compiler_params=pltpu.CompilerParams(dimension_semantics=("arbitrary", "arbitrary"),
                                             vmem_limit_bytes=_VMEM_LIMIT_SAMPLE),
        name="moba_sample",
    )(page_flat, qbd, knew, vnew, bnear, bfar, bownt, hmaskt, rsum, cache_kt, cache_vt)


def _post_kernel(x_ref, ya_ref, yb_ref, sga_ref, sgb_ref, wpa_ref, wpb_ref, wo_ref, g2_ref,
                 wup_ref, cw_ref, cb_ref, wdn_ref, gf_ref, *rest, tm, tiles_per_seq, per_row_state,
                 final_norm):
    if per_row_state:
        s1_ref, s2_ref, tpos_ref = rest[:3]
        rest = rest[3:]
    else:
        st_ref = rest[0]
        rest = rest[1:]
    xo_ref, a_ref = rest[:2]
    rest = rest[2:]
    if final_norm:
        y_ref = rest[0]
        rest = rest[1:]
    if not per_row_state:
        carry_scr = rest[0]

    ma = jnp.dot(ya_ref[...], wpa_ref[...], preferred_element_type=_F32)
    mb = jnp.dot(yb_ref[...], wpb_ref[...], preferred_element_type=_F32)
    merged = (_sigmoid(sga_ref[...].astype(_F32)) * ma
              + _sigmoid(sgb_ref[...].astype(_F32)) * mb)
    x1 = x_ref[...] + jnp.dot(merged.astype(_BF16), wo_ref[...], preferred_element_type=_F32)

    xn = _rms(x1, g2_ref[...]).astype(_BF16)
    if not per_row_state:
        @pl.when(pl.program_id(0) % tiles_per_seq == 0)
        def _():
            carry_scr[...] = st_ref[0]

    x2 = x1
    for lo, hi in zip(_FFN_SPLITS[:-1], _FFN_SPLITS[1:]):
        cols = slice(lo, hi)
        width = hi - lo
        a = jnp.dot(xn, wup_ref[:, cols], preferred_element_type=_F32)
        gate = jnp.dot(xn, wup_ref[:, D_FF + lo:D_FF + hi], preferred_element_type=_F32)
        prev1 = pltpu.roll(a, 1, 0)
        prev2 = pltpu.roll(a, 2, 0)
        if per_row_state:
            tpos = tpos_ref[:, cols]
            prev1 = jnp.where(tpos >= 1, prev1, s1_ref[:, cols])
            prev2 = jnp.where(tpos >= 2, prev2, s2_ref[:, cols])
            a_ref[:, cols] = a
        else:
            row = lax.broadcasted_iota(jnp.int32, (8, width), 0)
            c0 = carry_scr[0:1, cols]
            c1 = carry_scr[1:2, cols]
            top1 = jnp.where(row == 0, c1, prev1[:8])
            top2 = jnp.where(row == 0, c0, jnp.where(row == 1, c1, prev2[:8]))
            prev1 = jnp.concatenate([top1, prev1[8:]], axis=0)
            prev2 = jnp.concatenate([top2, prev2[8:]], axis=0)
            tail = a[tm - (CONV_W - 1):, :]
            carry_scr[:, cols] = tail
            a_ref[0, :, cols] = tail
        conv = (cb_ref[:, cols] + prev2 * cw_ref[0:1, cols] + prev1 * cw_ref[1:2, cols]
                + a * cw_ref[2:3, cols])
        act = (_gelu(conv) * gate).astype(_BF16)
        x2 = x2 + jnp.dot(act, wdn_ref[cols, :], preferred_element_type=_F32)
    xo_ref[...] = x2
    if final_norm:
        y_ref[...] = _rms(x2, gf_ref[...])


def _post_call(x2, ya, yb, sga, sgb, wpa, wpb, wo, g2, wup, cw, cb, wdn, gf, state_args, *,
               layer, tm, tiles_per_seq, per_row_state, final_norm):
    n = x2.shape[0]
    row = lambda width: pl.BlockSpec((tm, width), lambda i: (i, 0))
    in_specs = [row(D_MODEL), row(A_WIDTH), row(B_WIDTH), row(D_MODEL), row(D_MODEL),
                _layer_spec((A_WIDTH, D_MODEL), layer), _layer_spec((B_WIDTH, D_MODEL), layer),
                _layer_spec((D_MODEL, D_MODEL), layer), _const_spec((1, D_MODEL)),
                _layer_spec((D_MODEL, 2 * D_FF), layer), _const_spec((CONV_W, D_FF)),
                _const_spec((1, D_FF)), _layer_spec((D_FF, D_MODEL), layer),
                _const_spec((1, D_MODEL))]
    out_shape = [jax.ShapeDtypeStruct((n, D_MODEL), _F32)]
    out_specs = [row(D_MODEL)]
    scratch = []
    if per_row_state:
        in_specs += [row(D_FF), row(D_FF), row(D_FF)]
        out_shape.append(jax.ShapeDtypeStruct((n, D_FF), _F32))
        out_specs.append(row(D_FF))
    else:
        nseq = n // (tm * tiles_per_seq)
        in_specs.append(pl.BlockSpec((1, CONV_W - 1, D_FF), lambda i: (i // tiles_per_seq, 0, 0)))
        out_shape.append(jax.ShapeDtypeStruct((nseq, CONV_W - 1, D_FF), _F32))
        out_specs.append(pl.BlockSpec((1, CONV_W - 1, D_FF), lambda i: (i // tiles_per_seq, 0, 0)))
        scratch.append(pltpu.VMEM((CONV_W - 1, D_FF), _F32))
    if final_norm:
        out_shape.append(jax.ShapeDtypeStruct((n, D_MODEL), _F32))
        out_specs.append(row(D_MODEL))
    return pl.pallas_call(
        functools.partial(_post_kernel, tm=tm, tiles_per_seq=tiles_per_seq,
                          per_row_state=per_row_state, final_norm=final_norm),
        grid=(n // tm,),
        in_specs=in_specs,
        out_specs=out_specs,
        out_shape=out_shape,
        scratch_shapes=scratch,
        compiler_params=pltpu.CompilerParams(dimension_semantics=("arbitrary",),
                                             vmem_limit_bytes=_VMEM_LIMIT),
        name="merge_convffn",
    )(x2, ya, yb, sga, sgb, wpa, wpb, wo, g2, wup, cw, cb, wdn, gf, *state_args)


def _t5_bucket(rel):
    max_exact = N_BUCKETS // 2
    relf = jnp.maximum(rel, 1).astype(_F32)
    large = max_exact + (jnp.log(relf / max_exact) / math.log(MAX_DISTANCE / max_exact)
                         * (N_BUCKETS - max_exact)).astype(jnp.int32)
    large = jnp.minimum(large, N_BUCKETS - 1)
    return jnp.where(rel < max_exact, rel, large)


def _far_bucket_is_last(min_rel):
    max_exact = N_BUCKETS // 2
    v = max_exact + int(math.log(min_rel / max_exact) / math.log(MAX_DISTANCE / max_exact)
                        * (N_BUCKETS - max_exact) - 1e-6)
    return v >= N_BUCKETS - 1


def _toeplitz(diag, n):
    width = 2 * n
    flat = jnp.tile(diag, (1,) * (diag.ndim - 1) + (n,))
    skew = flat[..., :n * (width - 1)].reshape(diag.shape[:-1] + (n, width - 1))
    return skew[..., n - 1:2 * n - 1]


def _prompt_bias_tables(rel_bias):
    assert _far_bucket_is_last(MOBA_BLOCK + 1)
    assert (_TAB_OWN, _TAB_PREV, _TAB_FAR, _TAB_DEAD) == (0, 1, 2, 3) and _HEAD_TABLES == 2
    bt = rel_bias.T.astype(_F32)
    bt = (bt - bt[:, N_BUCKETS - 1:]) * _LOG2E
    rel = jnp.arange(2 * MOBA_BLOCK, dtype=jnp.int32) - (MOBA_BLOCK - 1)
    own = jnp.where(rel >= 0, bt[:, _t5_bucket(jnp.maximum(rel, 0))], _NEG)
    prev = bt[:, _t5_bucket(rel + MOBA_BLOCK)]
    per_head = jnp.stack([_toeplitz(own, MOBA_BLOCK), _toeplitz(prev, MOBA_BLOCK)], axis=1)
    per_head = per_head.reshape(B_HEADS * _HEAD_TABLES, MOBA_BLOCK, MOBA_BLOCK)
    shared = jnp.stack([jnp.zeros((MOBA_BLOCK, MOBA_BLOCK), _F32),
                        jnp.full((MOBA_BLOCK, MOBA_BLOCK), _NEG, _F32)])
    return jnp.concatenate([per_head, shared], axis=0)


def _sample_bias_tables(rel_bias, tdec, tpad):
    assert _far_bucket_is_last(MOBA_BLOCK + 1)
    bt = rel_bias.T.astype(_F32)
    tok = jnp.arange(tdec, dtype=jnp.int32)[:, None]
    key = jnp.arange(MOBA_BLOCK, dtype=jnp.int32)[None, :]
    near = bt[:, _t5_bucket(MOBA_BLOCK + tok - key)]
    far = jnp.broadcast_to(bt[:, N_BUCKETS - 1][:, None, None], (B_HEADS, tdec, 128))
    new = jnp.arange(tpad, dtype=jnp.int32)[None, :]
    own = jnp.where((tok - new >= 0) & (new < tdec), bt[:, _t5_bucket(jnp.maximum(tok - new, 0))],
                    _NEG)
    flat = lambda a: a.reshape(B_HEADS * tdec, a.shape[-1])
    return flat(near), flat(far), flat(own)


def kernel(x_prompt, x_sample, cache_k, cache_v, state_conv, page_table, w_in, ln_v_g, ln_v_b, w_s,
           b_s, w_pa, w_pb, w_o, norm1_g, norm2_g, w_up, conv_w, conv_b, w_down, rel_bias, norm_f):
    nbp_, seq, _ = x_prompt.shape
    nbs, tdec, _ = x_sample.shape
    depth, n_pool, page_size = cache_k.shape[:3]
    n_pages = page_table.shape[1]
    ns = nbs * tdec
    np_ = nbp_ * seq
    assert page_size == CHUNK and n_pages % _PAGES_PER_STEP == 0 and seq % MOBA_BLOCK == 0
    assert _PAGES_PER_STEP % _PAGES_PER_BLOCK == 0 and seq % _POST_TILE == 0
    assert CONV_W - 1 <= tdec <= CHUNK and ns % CHUNK == 0 and CHUNK % tdec == 0
    tpad = -(-tdec // 8) * 8
    tm_p = _PROJ_TILE
    nb = seq // MOBA_BLOCK
    nb_past = n_pages // _PAGES_PER_BLOCK
    rows = B_HEADS * tdec

    xp = x_prompt.reshape(np_, D_MODEL)
    xs = x_sample.reshape(ns, D_MODEL)
    cache_kt = cache_k.transpose(0, 1, 3, 4, 2).reshape(depth * n_pool, B_WIDTH, page_size)
    cache_vt = cache_v.transpose(0, 1, 3, 4, 2).reshape(depth * n_pool, B_WIDTH, page_size)
    page_flat = page_table.reshape(-1).astype(jnp.int32)

    btab = _prompt_bias_tables(rel_bias)
    bnear_s, bfar_s, bown_s = _sample_bias_tables(rel_bias, tdec, tpad)
    row_head = jnp.arange(rows, dtype=jnp.int32)[:, None] // tdec
    hmask = (jnp.arange(B_WIDTH, dtype=jnp.int32)[None, :] // B_HEAD_DIM == row_head).astype(_F32)
    rsum = (jnp.arange(rows, dtype=jnp.int32)[None, :] % tdec
            == jnp.arange(tdec, dtype=jnp.int32)[:, None]).astype(_F32)
    lane_pad = lambda a: jnp.pad(a, ((0, 0), (0, 128 - rows)))
    bownt_s, hmaskt = lane_pad(bown_s.T), lane_pad(hmask.T)
    rsum_p = lane_pad(jnp.pad(rsum, ((0, tpad - tdec), (0, 0))))
    tpos = jnp.broadcast_to((jnp.arange(ns, dtype=jnp.int32) % tdec)[:, None], (ns, D_FF))
    tril = jnp.tril(jnp.ones((CHUNK, CHUNK), bool))
    idx = jnp.arange(CHUNK, dtype=jnp.int32)
    same_chunk = tril & (idx[:, None] // tdec == idx[None, :] // tdec)
    zero_state = jnp.zeros((nbp_, CONV_W - 1, D_FF), _F32)
    w_in_b, wpa, wpb, wo = (w.astype(_BF16) for w in (w_in, w_pa, w_pb, w_o))
    wup, wdn = w_up.astype(_BF16), w_down.astype(_BF16)
    gf = norm_f[None, :]
    mix_shape = (depth, A_GROUPS, CHUNK, CHUNK)
    reps = CHUNK // tdec
    wmix_p = jnp.where(tril, w_s, 0.0).astype(_BF16)
    bmix_p = jnp.broadcast_to(b_s[..., None], mix_shape)
    wmix_s = jnp.zeros(mix_shape, _F32)
    for t in range(tdec):
        for s in range(t + 1):
            here = same_chunk & (idx[:, None] % tdec == t) & (idx[None, :] % tdec == s)
            wmix_s = jnp.where(here, w_s[:, :, t, s][..., None, None], wmix_s)
    wmix_s = wmix_s.astype(_BF16)
    bmix_s = jnp.broadcast_to(jnp.tile(b_s[:, :, :tdec], (1, 1, reps))[..., None], mix_shape)

    ks_l, vs_l, as_l, cp_l, cs_l = [], [], [], [], []
    yp = ys = kv_nat = None
    for l in range(depth):
        last = l == depth - 1
        g1 = norm1_g[l][None, :]
        lng, lnb = ln_v_g[l][None, :], ln_v_b[l][None, :]
        g2 = norm2_g[l][None, :]
        cw, cb = conv_w[l], conv_b[l][None, :]

        (qp, kbp, vtp, kmeanp, knat, vnat, yap, sgap, sgbp) = _proj_call(
            xp, g1, w_in_b, lng, lnb, wmix_p, bmix_p, tm=tm_p, prompt=True, layer=l, depth=depth,
            seq=seq, kv_prev=kv_nat)
        kv_nat = (knat, vnat)
        ybp = _moba_prompt_call(qp.reshape(nbp_, seq, B_WIDTH), kbp.reshape(nbp_, seq, -1),
                                vtp, kmeanp.reshape(nbp_, nb, B_WIDTH), btab)
        outs = _post_call(xp, yap, ybp.reshape(np_, B_WIDTH), sgap, sgbp, wpa, wpb, wo, g2,
                          wup, cw, cb, wdn, gf, (zero_state,), layer=l, tm=_POST_TILE,
                          tiles_per_seq=seq // _POST_TILE, per_row_state=False, final_norm=last)
        xp, cp = outs[0], outs[1]
        if last:
            yp = outs[2]

        qs, ks, vs, yas, sgas, sgbs, vas = _proj_call(xs, g1, w_in_b, lng, lnb, wmix_s, bmix_s,
                                                      tm=CHUNK, prompt=False, layer=l)
        ks4 = ks.reshape(nbs, tdec, B_HEADS, B_HEAD_DIM)
        vs4 = vs.reshape(nbs, tdec, B_HEADS, B_HEAD_DIM)
        qbd = (qs.reshape(nbs, 1, tdec, B_WIDTH) * hmask.reshape(B_HEADS, tdec, B_WIDTH)[None]
               ).reshape(nbs, rows, B_WIDTH)
        pad = ((0, 0), (0, tpad - tdec), (0, 0))
        knew = jnp.pad(ks.reshape(nbs, tdec, B_WIDTH), pad)
        vnew = jnp.pad(vs.reshape(nbs, tdec, B_WIDTH), pad)
        ybs = _moba_sample_call(page_flat, qbd, knew, vnew, bnear_s, bfar_s, bownt_s, hmaskt,
                                rsum_p, cache_kt, cache_vt, layer=l, n_pool=n_pool, n_pages=n_pages,
                                tdec=tdec)
        ybs = ybs.reshape(ns, B_WIDTH)
        st = state_conv[l]
        s1 = jnp.broadcast_to(st[:, 1:2, :], (nbs, tdec, D_FF)).reshape(ns, D_FF)
        s2 = jnp.concatenate([st, jnp.zeros((nbs, tdec - (CONV_W - 1), D_FF), _F32)],
                             axis=1).reshape(ns, D_FF)
        outs = _post_call(xs, yas, ybs.astype(_BF16), sgas, sgbs, wpa, wpb, wo,
                          g2, wup, cw, cb, wdn, gf, (s1, s2, tpos), layer=l, tm=CHUNK,
                          tiles_per_seq=1, per_row_state=True, final_norm=last)
        xs, a_s = outs[0], outs[1]
        if last:
            ys = outs[2]

        ks_l.append(ks4)
        vs_l.append(vs4)
        as_l.append(vas.reshape(nbs, tdec, A_WIDTH))
        cp_l.append(cp)
        a_ext = jnp.concatenate([st, a_s.reshape(nbs, tdec, D_FF)], axis=1)
        cs_l.append(a_ext[:, -(CONV_W - 1):])

    def kv_out(t):
        return t.reshape(depth, nbp_, B_HEADS, B_HEAD_DIM, seq).transpose(0, 1, 4, 2, 3)

    return (yp.reshape(nbp_, seq, D_MODEL), ys.reshape(nbs, tdec, D_MODEL),
            kv_out(kv_nat[0]), kv_out(kv_nat[1]),
            jnp.stack(ks_l, 0), jnp.stack(vs_l, 0),
            jnp.stack(as_l, 0), jnp.stack(cp_l, 0), jnp.stack(cs_l, 0))
```

```python
import functools
import math

import jax
import jax.numpy as jnp
from jax import lax
from jax.experimental import pallas as pl
from jax.experimental.pallas import tpu as pltpu

D_MODEL = 1024
CHUNK = 128
A_WIDTH = D_MODEL // 2
A_GROUPS = 4
A_GROUP_DIM = A_WIDTH // A_GROUPS
B_HEADS = 8
B_HEAD_DIM = 64
B_WIDTH = B_HEADS * B_HEAD_DIM
MOBA_BLOCK = 256
MOBA_TOPK = 3
N_BUCKETS = 32
MAX_DISTANCE = 128
D_FF = 2816
CONV_W = 3
EPS = 1e-6
N_IN = 2 * A_WIDTH + 3 * B_WIDTH + 2 * D_MODEL

_C_U, _C_VA, _C_Q, _C_K, _C_V, _C_GA, _C_GB = 0, 512, 1024, 1536, 2048, 2560, 3584

_NEG = -1e30
_HEADS_PER_PAIR = 2
_PAIRS_PER_STEP = 2
_SWEEP_WIDTHS = (8, 4, 2, 1)
_TAB_OWN, _TAB_PREV, _TAB_FAR, _TAB_DEAD = 0, 1, 2, 3
_HEAD_TABLES = 2
_PAIR_LANES = _HEADS_PER_PAIR * B_HEAD_DIM
_K_LANES = 2 * _PAIR_LANES
_V_ROWS = B_HEAD_DIM + 16
_LOG2E = math.log2(math.e)
_PAGES_PER_STEP = 16
_PAGES_PER_BLOCK = MOBA_BLOCK // CHUNK
_PAGE_SLOTS = 3
_PROJ_TILE = 512
_POST_TILE = 512
_FFN_SPLITS = (0, 1536, D_FF)
_VMEM_LIMIT = 48 * 1024 * 1024
_SCALE = 1.0 / math.sqrt(B_HEAD_DIM)

_F32 = jnp.float32
_BF16 = jnp.bfloat16


def _gelu(x):
    k = -2.0 * 0.7978845608028654 * _LOG2E
    return x / (1.0 + jnp.exp2(x * (k + (k * 0.044715) * (x * x))))


def _sigmoid(x):
    return 1.0 / (1.0 + jnp.exp2(x * -_LOG2E))


def _rms(x, g):
    return x * lax.rsqrt(jnp.mean(x * x, axis=-1, keepdims=True) + EPS) * g


def _const_spec(shape):
    zeros = (0,) * len(shape)
    return pl.BlockSpec(shape, lambda *_: zeros, pipeline_mode=pl.Buffered(1))


def _layer_spec(shape, layer):
    index = (layer,) + (0,) * len(shape)
    return pl.BlockSpec((None,) + shape, lambda *_: index, pipeline_mode=pl.Buffered(1))


def _proj_kernel(x_ref, g1_ref, w_ref, lng_ref, lnb_ref, wmix_ref, bmix_ref, *rest, tm, prompt,
                 nb=None):
    if prompt:
        (q_ref, kb_ref, vt_ref, kmean_ref, knat_ref, vnat_ref, ya_ref, sga_ref, sgb_ref) = rest[-9:]
    else:
        (q_ref, k_ref, v_ref, ya_ref, sga_ref, sgb_ref, va_ref) = rest
    xn = _rms(x_ref[...], g1_ref[...]).astype(_BF16)

    def proj(lo, width):
        return jnp.dot(xn, w_ref[:, lo:lo + width], preferred_element_type=_F32)

    u = _gelu(proj(_C_U, A_WIDTH))
    va = _gelu(proj(_C_VA, A_WIDTH))
    mu = jnp.mean(va, axis=-1, keepdims=True)
    vc = va - mu
    var = jnp.mean(vc * vc, axis=-1, keepdims=True)
    va = vc * lax.rsqrt(var + EPS) * lng_ref[...] + lnb_ref[...]
    if not prompt:
        va_ref[...] = va
    for c in range(tm // CHUNK):
        rows = slice(c * CHUNK, (c + 1) * CHUNK)
        for g in range(A_GROUPS):
            cols = slice(g * A_GROUP_DIM, (g + 1) * A_GROUP_DIM)
            mixed = jnp.dot(wmix_ref[g], va[rows, cols].astype(_BF16),
                            preferred_element_type=_F32) + bmix_ref[g]
            ya_ref[rows, cols] = (u[rows, cols] * mixed).astype(_BF16)

    q_ref[...] = proj(_C_Q, B_WIDTH)
    k = proj(_C_K, B_WIDTH)
    v = proj(_C_V, B_WIDTH)
    if prompt:
        blocks = tm // MOBA_BLOCK
        vt = v.T
        knat_ref[0] = k.T
        vnat_ref[0] = vt
        for r in range(blocks):
            kmean_ref[0, r:r + 1, :] = jnp.mean(k[r * MOBA_BLOCK:(r + 1) * MOBA_BLOCK], axis=0,
                                                keepdims=True)
        kb = k.astype(_BF16)
        first_blk = (pl.program_id(0) * blocks) % nb
        shape = (tm, _K_LANES - _PAIR_LANES)
        blk_of_row = first_blk + lax.broadcasted_iota(jnp.int32, shape, 0) // MOBA_BLOCK
        onehot = jnp.where(lax.broadcasted_iota(jnp.int32, shape, 1) == blk_of_row, 1.0, 0.0
                           ).astype(_BF16)
        for hp in range(B_HEADS // _HEADS_PER_PAIR):
            kb_ref[:, hp * _K_LANES:hp * _K_LANES + _PAIR_LANES] = (
                kb[:, hp * _PAIR_LANES:(hp + 1) * _PAIR_LANES])
            kb_ref[:, hp * _K_LANES + _PAIR_LANES:(hp + 1) * _K_LANES] = onehot
        vtb = vt.astype(_BF16)
        row = lax.broadcasted_iota(jnp.int32, (_V_ROWS - B_HEAD_DIM, MOBA_BLOCK), 0)
        ones_row = jnp.where(row == 0, 1.0, 0.0).astype(_BF16)
        for r in range(blocks):
            keys = slice(r * MOBA_BLOCK, (r + 1) * MOBA_BLOCK)
            for h in range(B_HEADS):
                vt_ref[0, r, h * _V_ROWS:h * _V_ROWS + B_HEAD_DIM, :] = (
                    vtb[h * B_HEAD_DIM:(h + 1) * B_HEAD_DIM, keys])
                vt_ref[0, r, h * _V_ROWS + B_HEAD_DIM:(h + 1) * _V_ROWS, :] = ones_row
    else:
        k_ref[...] = k
        v_ref[...] = v
    sga_ref[...] = proj(_C_GA, D_MODEL).astype(_BF16)
    sgb_ref[...] = proj(_C_GB, D_MODEL).astype(_BF16)


def _proj_call(x2, g1, w_in, lng, lnb, wmix, bmix, *, tm, prompt, layer=0, depth=1, seq=None,
               kv_prev=None):
    n = x2.shape[0]
    nt = n // tm
    row = lambda width: pl.BlockSpec((tm, width), lambda i: (i, 0))
    in_specs = [row(D_MODEL), _const_spec((1, D_MODEL)), _layer_spec((D_MODEL, N_IN), layer),
                _const_spec((1, A_WIDTH)), _const_spec((1, A_WIDTH)),
                _layer_spec((A_GROUPS, CHUNK, CHUNK), layer),
                _layer_spec((A_GROUPS, CHUNK, CHUNK), layer)]
    args = [x2, g1, w_in, lng, lnb, wmix, bmix]
    aliases = {}
    tail_shape = [jax.ShapeDtypeStruct((n, A_WIDTH), _BF16),
                  jax.ShapeDtypeStruct((n, D_MODEL), _BF16),
                  jax.ShapeDtypeStruct((n, D_MODEL), _BF16)]
    tail_specs = [row(A_WIDTH), row(D_MODEL), row(D_MODEL)]
    nb = None
    if prompt:
        nb = seq // MOBA_BLOCK
        blocks = tm // MOBA_BLOCK
        tps = seq // tm
        assert tm % MOBA_BLOCK == 0 and seq % tm == 0 and nb <= _K_LANES - _PAIR_LANES
        nseq = n // seq
        k_cols = B_HEADS // _HEADS_PER_PAIR * _K_LANES
        nat = jax.ShapeDtypeStruct((depth * nseq, B_WIDTH, seq), _F32)
        nat_spec = pl.BlockSpec((1, B_WIDTH, tm), lambda i: (layer * nseq + i // tps, 0, i % tps))
        out_shape = [jax.ShapeDtypeStruct((n, B_WIDTH), _F32),
                     jax.ShapeDtypeStruct((n, k_cols), _BF16),
                     jax.ShapeDtypeStruct((nseq, nb, B_HEADS * _V_ROWS, MOBA_BLOCK), _BF16),
                     jax.ShapeDtypeStruct((nt, blocks, B_WIDTH), _F32), nat, nat] + tail_shape
        out_specs = [row(B_WIDTH), row(k_cols),
                     pl.BlockSpec((1, blocks, B_HEADS * _V_ROWS, MOBA_BLOCK),
                                  lambda i: (i // tps, i % tps, 0, 0)),
                     pl.BlockSpec((1, blocks, B_WIDTH), lambda i: (i, 0, 0)), nat_spec, nat_spec
                     ] + tail_specs
        if kv_prev is not None:
            in_specs += [pl.BlockSpec(memory_space=pl.ANY)] * 2
            args += list(kv_prev)
            aliases = {7: 4, 8: 5}
    else:
        out_shape = [jax.ShapeDtypeStruct((n, B_WIDTH), _F32)] * 3 + tail_shape + [
            jax.ShapeDtypeStruct((n, A_WIDTH), _F32)]
        out_specs = [row(B_WIDTH)] * 3 + tail_specs + [row(A_WIDTH)]
    return pl.pallas_call(
        functools.partial(_proj_kernel, tm=tm, prompt=prompt, nb=nb),
        grid=(nt,),
        in_specs=in_specs,
        out_specs=out_specs,
        out_shape=out_shape,
        input_output_aliases=aliases,
        compiler_params=pltpu.CompilerParams(dimension_semantics=("arbitrary",),
                                             vmem_limit_bytes=_VMEM_LIMIT),
        name="proj_mixer_a",
    )(*args)


def _topk_block_mask(gate, own, nb, width):
    blk = lax.broadcasted_iota(jnp.int32, (nb, width), 0)
    gm = jnp.where(blk < own, gate, -jnp.inf)
    rank = jnp.zeros((nb, width), _F32)
    for jp in range(nb):
        r = gm[jp:jp + 1, :]
        beats = jnp.where(r > gm, 1.0, jnp.where(r == gm, jnp.where(blk > jp, 1.0, 0.0), 0.0))
        rank = rank + beats
    past_sel = jnp.where(blk < own, jnp.where(rank < MOBA_TOPK, 1.0, 0.0), 0.0)
    return jnp.where(blk == own, 1.0, past_sel)


def _moba_prompt_kernel(q_ref, k_ref, vt_ref, kmean_ref, btab_ref, o_ref, s_scr, *, nb):
    i = pl.program_id(2)
    blk = MOBA_BLOCK
    heads = range(_PAIRS_PER_STEP * _HEADS_PER_PAIR)

    qt = q_ref[0].T
    feat = lax.broadcasted_iota(jnp.int32, (_PAIR_LANES, blk), 0)
    qz = []
    for hh in heads:
        pair, sub = divmod(hh, _HEADS_PER_PAIR)
        in_head = (feat >= sub * B_HEAD_DIM) & (feat < (sub + 1) * B_HEAD_DIM)
        qh = jnp.where(in_head, qt[pair * _PAIR_LANES:(pair + 1) * _PAIR_LANES], 0.0)
        gate = jnp.dot(kmean_ref[0, :, pair * _PAIR_LANES:(pair + 1) * _PAIR_LANES], qh,
                       preferred_element_type=_F32, precision=lax.Precision.HIGHEST)
        mask_rows = jnp.where(_topk_block_mask(gate, i, nb, blk) > 0.5, 0.0, _NEG)
        pad = jnp.zeros((_K_LANES - _PAIR_LANES - nb, blk), _F32)
        qz.append(jnp.concatenate([qh * (_SCALE * _LOG2E), mask_rows, pad], axis=0).astype(_BF16))

    sweeps, done = [], 0
    for width in _SWEEP_WIDTHS:
        last_width = width == _SWEEP_WIDTHS[-1]
        n = lax.div(i + (width if last_width else 1) - done, width)
        sweeps.append((n, done, width))
        done = done + n * width
    first_head = pl.program_id(1) * len(heads)

    def slots(first, it, width):
        js = [first + it * width + u for u in range(width)]
        kinds = [jnp.where(j == i, _TAB_OWN, jnp.where(j == i - 1, _TAB_PREV,
                                                       jnp.where(j > i, _TAB_DEAD, _TAB_FAR)))
                 for j in js]
        return js, [jnp.minimum(j, i) for j in js], kinds

    def table_row(hh, kind):
        per_head = (first_head + hh) * _HEAD_TABLES + kind
        return jnp.where(kind < _HEAD_TABLES, per_head, B_HEADS * _HEAD_TABLES + kind - _HEAD_TABLES)

    def logits_pass(it, ms, group, first, width):
        js, jcs, tabs = slots(first, it, width)
        out = []
        for hh, m in zip(group, ms):
            pair = hh // _HEADS_PER_PAIR
            for j, jc, tab in zip(js, jcs, tabs):
                kj = k_ref[0, pl.ds(pl.multiple_of(jc * blk, blk), blk),
                           pair * _K_LANES:(pair + 1) * _K_LANES]
                s = jnp.dot(kj, qz[hh], preferred_element_type=_F32) + btab_ref[table_row(hh, tab)]
                s_scr[hh, j] = s
                m = jnp.maximum(m, jnp.max(s, axis=0, keepdims=True))
            out.append(m)
        return tuple(out)

    def value_pass(it, accs, group, ms, first, width):
        js, jcs, _ = slots(first, it, width)
        out = []
        for hh, acc, m in zip(group, accs, ms):
            for j, jc in zip(js, jcs):
                p = jnp.exp2(s_scr[hh, j] - m).astype(_BF16)
                acc = acc + jnp.dot(vt_ref[0, jc, hh * _V_ROWS:(hh + 1) * _V_ROWS, :], p,
                                    preferred_element_type=_F32)
            out.append(acc)
        return tuple(out)

    group = tuple(heads)
    ms = tuple(jnp.full((1, blk), _NEG, _F32) for _ in group)
    for n, first, width in sweeps:
        ms = lax.fori_loop(0, n, lambda it, c, f=first, w=width: logits_pass(it, c, group, f, w), ms)
    accs = tuple(jnp.zeros((_V_ROWS, blk), _F32) for _ in group)
    for n, first, width in sweeps:
        accs = lax.fori_loop(
            0, n, lambda it, c, f=first, w=width: value_pass(it, c, group, ms, f, w), accs)
    outs = [acc[:B_HEAD_DIM] / acc[B_HEAD_DIM:B_HEAD_DIM + 1] for acc in accs]
    o_ref[0] = jnp.concatenate(outs, axis=0).T.astype(_BF16)


def _moba_prompt_call(q3, kb3, vt4, kmean3, btab):
    b, t, _ = q3.shape
    nb = t // MOBA_BLOCK
    g = _PAIRS_PER_STEP
    heads = g * _HEADS_PER_PAIR
    return pl.pallas_call(
        functools.partial(_moba_prompt_kernel, nb=nb),
        grid=(b, B_HEADS // heads, nb),
        in_specs=[pl.BlockSpec((1, MOBA_BLOCK, g * _PAIR_LANES), lambda bb, hp, i: (bb, i, hp)),
                  pl.BlockSpec((1, t, g * _K_LANES), lambda bb, hp, i: (bb, 0, hp)),
                  pl.BlockSpec((1, nb, heads * _V_ROWS, MOBA_BLOCK),
                               lambda bb, hp, i: (bb, 0, hp, 0)),
                  pl.BlockSpec((1, nb, g * _PAIR_LANES), lambda bb, hp, i: (bb, 0, hp)),
                  _const_spec(btab.shape)],
        out_specs=pl.BlockSpec((1, MOBA_BLOCK, g * _PAIR_LANES), lambda bb, hp, i: (bb, i, hp)),
        out_shape=jax.ShapeDtypeStruct((b, t, B_WIDTH), _BF16),
        scratch_shapes=[pltpu.VMEM((heads, nb + _SWEEP_WIDTHS[-1] - 1, MOBA_BLOCK, MOBA_BLOCK),
                                   _F32)],
        compiler_params=pltpu.CompilerParams(
            dimension_semantics=("arbitrary", "arbitrary", "arbitrary"),
            vmem_limit_bytes=_VMEM_LIMIT),
        name="moba_prompt",
    )(q3, kb3, vt4, kmean3, btab)


def _moba_sample_kernel(pt_ref, qbd_ref, knew_ref, vnew_ref, bnear_ref, bfar_ref,
                        bownt_ref, hmaskt_ref, rsum_ref, ck_hbm, cv_hbm, o_ref,
                        gate_scr, m_scr, l_scr, acc_scr, st_scr, qbdt_scr, kbuf, vbuf, sems,
                        *, nb_past, page_base):
    pages = _PAGES_PER_STEP
    c = pl.program_id(1)
    step = pl.program_id(0) * pl.num_programs(1) + c
    last_step = pl.num_programs(0) * pl.num_programs(1) - 1
    rows = qbd_ref.shape[1]
    tdec = o_ref.shape[1]
    blocks_per_step = pages // _PAGES_PER_BLOCK

    ahead = _PAGE_SLOTS - 1

    def slot_of(s):
        return lax.rem(s, _PAGE_SLOTS)

    def page_copy(kind, of_step, p):
        src_step = jnp.minimum(of_step, last_step)
        src, buf = ((ck_hbm, kbuf), (cv_hbm, vbuf))[kind]
        page = page_base + pt_ref[src_step * pages + p]
        slot = slot_of(of_step)
        return pltpu.make_async_copy(src.at[page], buf.at[slot, p], sems.at[slot, kind])

    def block_pages(bl):
        return range(bl * _PAGES_PER_BLOCK, (bl + 1) * _PAGES_PER_BLOCK)

    @pl.when(step == 0)
    def _():
        for s in range(ahead):
            for kind in range(2):
                for pg in range(pages):
                    page_copy(kind, s, pg).start()

    for kind in range(2):
        for pg in range(pages):
            page_copy(kind, step, pg).wait()
    slot = slot_of(step)

    def lanes_t(x):
        return jnp.concatenate([x, jnp.zeros((128 - x.shape[0], x.shape[1]), x.dtype)], axis=0).T

    qbd = qbd_ref[0]
    qz = (qbd * _SCALE).astype(_BF16)

    @pl.when(c == 0)
    def _():
        qbdt_scr[...] = lanes_t(qbd)

    qbdt = qbdt_scr[...]
    row_pad = jnp.zeros((128 - rows, MOBA_BLOCK), _F32)

    for bl in range(blocks_per_step):
        jg = c * blocks_per_step + bl
        kts = [kbuf[slot, pg] for pg in block_pages(bl)]
        kt = jnp.concatenate(kts, axis=1)
        kmean = jnp.sum(functools.reduce(lambda a, b: a + b, kts), axis=-1,
                        keepdims=True) * (1.0 / MOBA_BLOCK)
        gate_scr[pl.ds(jg, 1), :] = jnp.sum(qbdt * kmean, axis=0, keepdims=True)
        s = jnp.dot(qz, kt.astype(_BF16), preferred_element_type=_F32)
        s = s + jnp.where(jg == nb_past - 1, bnear_ref[...], bfar_ref[:, 0:1])
        st_scr[bl] = jnp.concatenate([s, row_pad], axis=0).T
    for bl in range(blocks_per_step):
        jg = c * blocks_per_step + bl
        vt = jnp.concatenate([vbuf[slot, pg] for pg in block_pages(bl)], axis=1)
        st = st_scr[bl]
        m = jnp.max(st, axis=0, keepdims=True)
        p = jnp.exp(st - m)
        m_scr[pl.ds(jg, 1), :] = m
        l_scr[pl.ds(jg, 1), :] = jnp.sum(p, axis=0, keepdims=True)
        acc_scr[jg] = jnp.dot(vt.astype(_BF16), p.astype(_BF16), preferred_element_type=_F32)
    for kind in range(2):
        for pg in range(pages):
            page_copy(kind, step + ahead, pg).start()

    @pl.when(step == last_step)
    def _():
        for s in range(1, ahead + 1):
            for kind in range(2):
                for pg in range(pages):
                    page_copy(kind, last_step + s, pg).wait()

    @pl.when(c == pl.num_programs(1) - 1)
    def _():
        sel = _topk_block_mask(gate_scr[...], nb_past, nb_past, 128) > 0.5
        m_blk = m_scr[...]
        s_own = jnp.dot(knew_ref[0], qbdt * _SCALE, preferred_element_type=_F32) + bownt_ref[...]
        m_tot = jnp.maximum(jnp.max(s_own, axis=0, keepdims=True),
                            jnp.max(jnp.where(sel, m_blk, _NEG), axis=0, keepdims=True))
        w = jnp.where(sel, jnp.exp(m_blk - m_tot), 0.0)
        p_own = jnp.exp(s_own - m_tot)
        l_tot = jnp.sum(p_own, axis=0, keepdims=True) + jnp.sum(w * l_scr[...], axis=0, keepdims=True)
        p_own = jnp.concatenate([p_own, jnp.zeros((128 - p_own.shape[0], 128), _F32)], axis=0)
        accs = [jnp.dot(lanes_t(vnew_ref[0]), p_own, preferred_element_type=_F32)]
        accs += [jnp.zeros_like(accs[0]) for _ in range(3)]
        for j in range(nb_past):
            accs[j % 4] = accs[j % 4] + w[j:j + 1, :] * acc_scr[j]
        out_t = ((accs[0] + accs[1]) + (accs[2] + accs[3])) / l_tot
        folded = lax.dot_general(rsum_ref[...], out_t * hmaskt_ref[...], (((1,), (1,)), ((), ())),
                                 preferred_element_type=_F32, precision=lax.Precision.HIGHEST)
        o_ref[0] = folded[:tdec]


def _moba_sample_call(page_flat, qbd, knew, vnew, bnear, bfar, bownt, hmaskt, rsum,
                      cache_kt, cache_vt, *, layer, n_pool, n_pages, tdec):
    nbatch, rows, _ = qbd.shape
    pages = _PAGES_PER_STEP
    nb_past = n_pages // _PAGES_PER_BLOCK
    assert nb_past % 8 == 0 and rows <= 128
    assert nbatch * (n_pages // pages) >= _PAGE_SLOTS

    per_batch = lambda arr: pl.BlockSpec((1,) + arr.shape[1:], lambda bb, cc, pt: (bb, 0, 0))
    const = lambda arr: pl.BlockSpec(arr.shape, lambda bb, cc, pt: (0,) * arr.ndim)
    page_ring = pltpu.VMEM((_PAGE_SLOTS, pages, B_WIDTH, CHUNK), _F32)
    grid_spec = pltpu.PrefetchScalarGridSpec(
        num_scalar_prefetch=1,
        grid=(nbatch, n_pages // pages),
        in_specs=[per_batch(qbd), per_batch(knew), per_batch(vnew),
                  const(bnear), const(bfar), const(bownt), const(hmaskt), const(rsum),
                  pl.BlockSpec(memory_space=pl.ANY), pl.BlockSpec(memory_space=pl.ANY)],
        out_specs=pl.BlockSpec((1, tdec, B_WIDTH), lambda bb, cc, pt: (bb, 0, 0)),
        scratch_shapes=[pltpu.VMEM((nb_past, 128), _F32),
                        pltpu.VMEM((nb_past, 128), _F32),
                        pltpu.VMEM((nb_past, 128), _F32),
                        pltpu.VMEM((nb_past, B_WIDTH, 128), _F32),
                        pltpu.VMEM((pages // _PAGES_PER_BLOCK, MOBA_BLOCK, 128), _F32),
                        pltpu.VMEM((B_WIDTH, 128), _F32),
                        page_ring, page_ring,
                        pltpu.SemaphoreType.DMA((_PAGE_SLOTS, 2))])
    return pl.pallas_call(
        functools.partial(_moba_sample_kernel, nb_past=nb_past, page_base=layer * n_pool),
        grid_spec=grid_spec,
        out_shape=jax.ShapeDtypeStruct((nbatch, tdec, B_WIDTH), _F32),
        compiler_params=pltpu.CompilerParams(dimension_semantics=("arbitrary", "arbitrary"),
                                             vmem_limit_bytes=_VMEM_LIMIT),
        name="moba_sample",
    )(page_flat, qbd, knew, vnew, bnear, bfar, bownt, hmaskt, rsum, cache_kt, cache_vt)


def _post_kernel(x_ref, ya_ref, yb_ref, sga_ref, sgb_ref, wpa_ref, wpb_ref, wo_ref, g2_ref,
                 wup_ref, cw_ref, cb_ref, wdn_ref, gf_ref, *rest, tm, tiles_per_seq, per_row_state,
                 final_norm):
    if per_row_state:
        s1_ref, s2_ref, tpos_ref = rest[:3]
        rest = rest[3:]
    else:
        st_ref = rest[0]
        rest = rest[1:]
    xo_ref, a_ref = rest[:2]
    rest = rest[2:]
    if final_norm:
        y_ref = rest[0]
        rest = rest[1:]
    if not per_row_state:
        carry_scr = rest[0]

    ma = jnp.dot(ya_ref[...], wpa_ref[...], preferred_element_type=_F32)
    mb = jnp.dot(yb_ref[...], wpb_ref[...], preferred_element_type=_F32)
    merged = (_sigmoid(sga_ref[...].astype(_F32)) * ma
              + _sigmoid(sgb_ref[...].astype(_F32)) * mb)
    x1 = x_ref[...] + jnp.dot(merged.astype(_BF16), wo_ref[...], preferred_element_type=_F32)

    xn = _rms(x1, g2_ref[...]).astype(_BF16)
    if not per_row_state:
        @pl.when(pl.program_id(0) % tiles_per_seq == 0)
        def _():
            carry_scr[...] = st_ref[0]

    x2 = x1
    for lo, hi in zip(_FFN_SPLITS[:-1], _FFN_SPLITS[1:]):
        cols = slice(lo, hi)
        width = hi - lo
        a = jnp.dot(xn, wup_ref[:, cols], preferred_element_type=_F32)
        gate = jnp.dot(xn, wup_ref[:, D_FF + lo:D_FF + hi], preferred_element_type=_F32)
        prev1 = pltpu.roll(a, 1, 0)
        prev2 = pltpu.roll(a, 2, 0)
        if per_row_state:
            tpos = tpos_ref[:, cols]
            prev1 = jnp.where(tpos >= 1, prev1, s1_ref[:, cols])
            prev2 = jnp.where(tpos >= 2, prev2, s2_ref[:, cols])
            a_ref[:, cols] = a
        else:
            row = lax.broadcasted_iota(jnp.int32, (8, width), 0)
            c0 = carry_scr[0:1, cols]
            c1 = carry_scr[1:2, cols]
            top1 = jnp.where(row == 0, c1, prev1[:8])
            top2 = jnp.where(row == 0, c0, jnp.where(row == 1, c1, prev2[:8]))
            prev1 = jnp.concatenate([top1, prev1[8:]], axis=0)
            prev2 = jnp.concatenate([top2, prev2[8:]], axis=0)
            tail = a[tm - (CONV_W - 1):, :]
            carry_scr[:, cols] = tail
            a_ref[0, :, cols] = tail
        conv = (cb_ref[:, cols] + prev2 * cw_ref[0:1, cols] + prev1 * cw_ref[1:2, cols]
                + a * cw_ref[2:3, cols])
        act = (_gelu(conv) * gate).astype(_BF16)
        x2 = x2 + jnp.dot(act, wdn_ref[cols, :], preferred_element_type=_F32)
    xo_ref[...] = x2
    if final_norm:
        y_ref[...] = _rms(x2, gf_ref[...])


def _post_call(x2, ya, yb, sga, sgb, wpa, wpb, wo, g2, wup, cw, cb, wdn, gf, state_args, *,
               layer, tm, tiles_per_seq, per_row_state, final_norm):
    n = x2.shape[0]
    row = lambda width: pl.BlockSpec((tm, width), lambda i: (i, 0))
    in_specs = [row(D_MODEL), row(A_WIDTH), row(B_WIDTH), row(D_MODEL), row(D_MODEL),
                _layer_spec((A_WIDTH, D_MODEL), layer), _layer_spec((B_WIDTH, D_MODEL), layer),
                _layer_spec((D_MODEL, D_MODEL), layer), _const_spec((1, D_MODEL)),
                _layer_spec((D_MODEL, 2 * D_FF), layer), _const_spec((CONV_W, D_FF)),
                _const_spec((1, D_FF)), _layer_spec((D_FF, D_MODEL), layer),
                _const_spec((1, D_MODEL))]
    out_shape = [jax.ShapeDtypeStruct((n, D_MODEL), _F32)]
    out_specs = [row(D_MODEL)]
    scratch = []
    if per_row_state:
        in_specs += [row(D_FF), row(D_FF), row(D_FF)]
        out_shape.append(jax.ShapeDtypeStruct((n, D_FF), _F32))
        out_specs.append(row(D_FF))
    else:
        nseq = n // (tm * tiles_per_seq)
        in_specs.append(pl.BlockSpec((1, CONV_W - 1, D_FF), lambda i: (i // tiles_per_seq, 0, 0)))
        out_shape.append(jax.ShapeDtypeStruct((nseq, CONV_W - 1, D_FF), _F32))
        out_specs.append(pl.BlockSpec((1, CONV_W - 1, D_FF), lambda i: (i // tiles_per_seq, 0, 0)))
        scratch.append(pltpu.VMEM((CONV_W - 1, D_FF), _F32))
    if final_norm:
        out_shape.append(jax.ShapeDtypeStruct((n, D_MODEL), _F32))
        out_specs.append(row(D_MODEL))
    return pl.pallas_call(
        functools.partial(_post_kernel, tm=tm, tiles_per_seq=tiles_per_seq,
                          per_row_state=per_row_state, final_norm=final_norm),
        grid=(n // tm,),
        in_specs=in_specs,
        out_specs=out_specs,
        out_shape=out_shape,
        scratch_shapes=scratch,
        compiler_params=pltpu.CompilerParams(dimension_semantics=("arbitrary",),
                                             vmem_limit_bytes=_VMEM_LIMIT),
        name="merge_convffn",
    )(x2, ya, yb, sga, sgb, wpa, wpb, wo, g2, wup, cw, cb, wdn, gf, *state_args)


def _t5_bucket(rel):
    max_exact = N_BUCKETS // 2
    relf = jnp.maximum(rel, 1).astype(_F32)
    large = max_exact + (jnp.log(relf / max_exact) / math.log(MAX_DISTANCE / max_exact)
                         * (N_BUCKETS - max_exact)).astype(jnp.int32)
    large = jnp.minimum(large, N_BUCKETS - 1)
    return jnp.where(rel < max_exact, rel, large)


def _far_bucket_is_last(min_rel):
    max_exact = N_BUCKETS // 2
    v = max_exact + int(math.log(min_rel / max_exact) / math.log(MAX_DISTANCE / max_exact)
                        * (N_BUCKETS - max_exact) - 1e-6)
    return v >= N_BUCKETS - 1


def _toeplitz(diag, n):
    width = 2 * n
    flat = jnp.tile(diag, (1,) * (diag.ndim - 1) + (n,))
    skew = flat[..., :n * (width - 1)].reshape(diag.shape[:-1] + (n, width - 1))
    return skew[..., n - 1:2 * n - 1]


def _prompt_bias_tables(rel_bias):
    assert _far_bucket_is_last(MOBA_BLOCK + 1)
    assert (_TAB_OWN, _TAB_PREV, _TAB_FAR, _TAB_DEAD) == (0, 1, 2, 3) and _HEAD_TABLES == 2
    bt = rel_bias.T.astype(_F32)
    bt = (bt - bt[:, N_BUCKETS - 1:]) * _LOG2E
    rel = jnp.arange(2 * MOBA_BLOCK, dtype=jnp.int32) - (MOBA_BLOCK - 1)
    own = jnp.where(rel >= 0, bt[:, _t5_bucket(jnp.maximum(rel, 0))], _NEG)
    prev = bt[:, _t5_bucket(rel + MOBA_BLOCK)]
    per_head = jnp.stack([_toeplitz(own, MOBA_BLOCK), _toeplitz(prev, MOBA_BLOCK)], axis=1)
    per_head = per_head.reshape(B_HEADS * _HEAD_TABLES, MOBA_BLOCK, MOBA_BLOCK)
    shared = jnp.stack([jnp.zeros((MOBA_BLOCK, MOBA_BLOCK), _F32),
                        jnp.full((MOBA_BLOCK, MOBA_BLOCK), _NEG, _F32)])
    return jnp.concatenate([per_head, shared], axis=0)


def _sample_bias_tables(rel_bias, tdec, tpad):
    assert _far_bucket_is_last(MOBA_BLOCK + 1)
    bt = rel_bias.T.astype(_F32)
    tok = jnp.arange(tdec, dtype=jnp.int32)[:, None]
    key = jnp.arange(MOBA_BLOCK, dtype=jnp.int32)[None, :]
    near = bt[:, _t5_bucket(MOBA_BLOCK + tok - key)]
    far = jnp.broadcast_to(bt[:, N_BUCKETS - 1][:, None, None], (B_HEADS, tdec, 128))
    new = jnp.arange(tpad, dtype=jnp.int32)[None, :]
    own = jnp.where((tok - new >= 0) & (new < tdec), bt[:, _t5_bucket(jnp.maximum(tok - new, 0))],
                    _NEG)
    flat = lambda a: a.reshape(B_HEADS * tdec, a.shape[-1])
    return flat(near), flat(far), flat(own)


def kernel(x_prompt, x_sample, cache_k, cache_v, state_conv, page_table, w_in, ln_v_g, ln_v_b, w_s,
           b_s, w_pa, w_pb, w_o, norm1_g, norm2_g, w_up, conv_w, conv_b, w_down, rel_bias, norm_f):
    nbp_, seq, _ = x_prompt.shape
    nbs, tdec, _ = x_sample.shape
    depth, n_pool, page_size = cache_k.shape[:3]
    n_pages = page_table.shape[1]
    ns = nbs * tdec
    np_ = nbp_ * seq
    assert page_size == CHUNK and n_pages % _PAGES_PER_STEP == 0 and seq % MOBA_BLOCK == 0
    assert _PAGES_PER_STEP % _PAGES_PER_BLOCK == 0 and seq % _POST_TILE == 0
    assert CONV_W - 1 <= tdec <= CHUNK and ns % CHUNK == 0 and CHUNK % tdec == 0
    tpad = -(-tdec // 8) * 8
    tm_p = _PROJ_TILE
    nb = seq // MOBA_BLOCK
    nb_past = n_pages // _PAGES_PER_BLOCK
    rows = B_HEADS * tdec

    xp = x_prompt.reshape(np_, D_MODEL)
    xs = x_sample.reshape(ns, D_MODEL)
    cache_kt = cache_k.transpose(0, 1, 3, 4, 2).reshape(depth * n_pool, B_WIDTH, page_size)
    cache_vt = cache_v.transpose(0, 1, 3, 4, 2).reshape(depth * n_pool, B_WIDTH, page_size)
    page_flat = page_table.reshape(-1).astype(jnp.int32)

    btab = _prompt_bias_tables(rel_bias)
    bnear_s, bfar_s, bown_s = _sample_bias_tables(rel_bias, tdec, tpad)
    row_head = jnp.arange(rows, dtype=jnp.int32)[:, None] // tdec
    hmask = (jnp.arange(B_WIDTH, dtype=jnp.int32)[None, :] // B_HEAD_DIM == row_head).astype(_F32)
    rsum = (jnp.arange(rows, dtype=jnp.int32)[None, :] % tdec
            == jnp.arange(tdec, dtype=jnp.int32)[:, None]).astype(_F32)
    lane_pad = lambda a: jnp.pad(a, ((0, 0), (0, 128 - rows)))
    bownt_s, hmaskt = lane_pad(bown_s.T), lane_pad(hmask.T)
    rsum_p = lane_pad(jnp.pad(rsum, ((0, tpad - tdec), (0, 0))))
    tpos = jnp.broadcast_to((jnp.arange(ns, dtype=jnp.int32) % tdec)[:, None], (ns, D_FF))
    tril = jnp.tril(jnp.ones((CHUNK, CHUNK), bool))
    idx = jnp.arange(CHUNK, dtype=jnp.int32)
    same_chunk = tril & (idx[:, None] // tdec == idx[None, :] // tdec)
    zero_state = jnp.zeros((nbp_, CONV_W - 1, D_FF), _F32)
    w_in_b, wpa, wpb, wo = (w.astype(_BF16) for w in (w_in, w_pa, w_pb, w_o))
    wup, wdn = w_up.astype(_BF16), w_down.astype(_BF16)
    gf = norm_f[None, :]
    mix_shape = (depth, A_GROUPS, CHUNK, CHUNK)
    reps = CHUNK // tdec
    wmix_p = jnp.where(tril, w_s, 0.0).astype(_BF16)
    bmix_p = jnp.broadcast_to(b_s[..., None], mix_shape)
    wmix_s = jnp.zeros(mix_shape, _F32)
    for t in range(tdec):
        for s in range(t + 1):
            here = same_chunk & (idx[:, None] % tdec == t) & (idx[None, :] % tdec == s)
            wmix_s = jnp.where(here, w_s[:, :, t, s][..., None, None], wmix_s)
    wmix_s = wmix_s.astype(_BF16)
    bmix_s = jnp.broadcast_to(jnp.tile(b_s[:, :, :tdec], (1, 1, reps))[..., None], mix_shape)

    ks_l, vs_l, as_l, cp_l, cs_l = [], [], [], [], []
    yp = ys = kv_nat = None
    for l in range(depth):
        last = l == depth - 1
        g1 = norm1_g[l][None, :]
        lng, lnb = ln_v_g[l][None, :], ln_v_b[l][None, :]
        g2 = norm2_g[l][None, :]
        cw, cb = conv_w[l], conv_b[l][None, :]

        (qp, kbp, vtp, kmeanp, knat, vnat, yap, sgap, sgbp) = _proj_call(
            xp, g1, w_in_b, lng, lnb, wmix_p, bmix_p, tm=tm_p, prompt=True, layer=l, depth=depth,
            seq=seq, kv_prev=kv_nat)
        kv_nat = (knat, vnat)
        ybp = _moba_prompt_call(qp.reshape(nbp_, seq, B_WIDTH), kbp.reshape(nbp_, seq, -1),
                                vtp, kmeanp.reshape(nbp_, nb, B_WIDTH), btab)
        outs = _post_call(xp, yap, ybp.reshape(np_, B_WIDTH), sgap, sgbp, wpa, wpb, wo, g2,
                          wup, cw, cb, wdn, gf, (zero_state,), layer=l, tm=_POST_TILE,
                          tiles_per_seq=seq // _POST_TILE, per_row_state=False, final_norm=last)
        xp, cp = outs[0], outs[1]
        if last:
            yp = outs[2]

        qs, ks, vs, yas, sgas, sgbs, vas = _proj_call(xs, g1, w_in_b, lng, lnb, wmix_s, bmix_s,
                                                      tm=CHUNK, prompt=False, layer=l)
        ks4 = ks.reshape(nbs, tdec, B_HEADS, B_HEAD_DIM)
        vs4 = vs.reshape(nbs, tdec, B_HEADS, B_HEAD_DIM)
        qbd = (qs.reshape(nbs, 1, tdec, B_WIDTH) * hmask.reshape(B_HEADS, tdec, B_WIDTH)[None]
               ).reshape(nbs, rows, B_WIDTH)
        pad = ((0, 0), (0, tpad - tdec), (0, 0))
        knew = jnp.pad(ks.reshape(nbs, tdec, B_WIDTH), pad)
        vnew = jnp.pad(vs.reshape(nbs, tdec, B_WIDTH), pad)
        ybs = _moba_sample_call(page_flat, qbd, knew, vnew, bnear_s, bfar_s, bownt_s, hmaskt,
                                rsum_p, cache_kt, cache_vt, layer=l, n_pool=n_pool, n_pages=n_pages,
                                tdec=tdec)
        ybs = ybs.reshape(ns, B_WIDTH)
        st = state_conv[l]
        s1 = jnp.broadcast_to(st[:, 1:2, :], (nbs, tdec, D_FF)).reshape(ns, D_FF)
        s2 = jnp.concatenate([st, jnp.zeros((nbs, tdec - (CONV_W - 1), D_FF), _F32)],
                             axis=1).reshape(ns, D_FF)
        outs = _post_call(xs, yas, ybs.astype(_BF16), sgas, sgbs, wpa, wpb, wo,
                          g2, wup, cw, cb, wdn, gf, (s1, s2, tpos), layer=l, tm=CHUNK,
                          tiles_per_seq=1, per_row_state=True, final_norm=last)
        xs, a_s = outs[0], outs[1]
        if last:
            ys = outs[2]

        ks_l.append(ks4)
        vs_l.append(vs4)
        as_l.append(vas.reshape(nbs, tdec, A_WIDTH))
        cp_l.append(cp)
        a_ext = jnp.concatenate([st, a_s.reshape(nbs, tdec, D_FF)], axis=1)
        cs_l.append(a_ext[:, -(CONV_W - 1):])

    def kv_out(t):
        return t.reshape(depth, nbp_, B_HEADS, B_HEAD_DIM, seq).transpose(0, 1, 4, 2, 3)

    return (yp.reshape(nbp_, seq, D_MODEL), ys.reshape(nbs, tdec, D_MODEL),
            kv_out(kv_nat[0]), kv_out(kv_nat[1]),
            jnp.stack(ks_l, 0), jnp.stack(vs_l, 0),
            jnp.stack(as_l, 0), jnp.stack(cp_l, 0), jnp.stack(cs_l, 0))
```
